```python
import math
import numpy as np
import jax
import jax.numpy as jnp
from jax import lax

D_MODEL = 1024
BATCH = 8
SEQ = 8192
DEPTH = 2

PLE_DIM = 256
D_FF = 4 * D_MODEL
HEAD_DIM = 64
ROT_DIM = HEAD_DIM // 4
ROPE_THETA = 500000.0
NORM_EPS = 1e-6
NEG_INF = -1e30
MAX_POS_OFFSET = 4096
CONV_WIDTH = 4
FOX_HEADS = (D_MODEL // 2) // HEAD_DIM
FOX_W = FOX_HEADS * HEAD_DIM
FOX_QBLOCK = 128
GDN_HEAD_DIM = 128
GDN_HEADS = (D_MODEL // 2) // GDN_HEAD_DIM
GDN_W = GDN_HEADS * GDN_HEAD_DIM
GDN_CHUNK = 64
NSA_HEADS = (D_MODEL // 2) // HEAD_DIM
NSA_W = NSA_HEADS * HEAD_DIM
NSA_KV_GROUPS = 2
NSA_KV_W = NSA_KV_GROUPS * HEAD_DIM
CMP_BLOCK = 32
CMP_STRIDE = 16
CMP_HIDDEN = 128
SEL_BLOCK = 64
SEL_TOPK = 16
WINDOW = 512
NSA_QBLOCK = 64
LRU_WIDTH = D_MODEL // 2
LRU_BLOCKS = 8
RG_C = 8.0

EVEN_SPLITS = [FOX_W, FOX_W, FOX_W, FOX_HEADS, GDN_W, GDN_W, GDN_W, GDN_W, GDN_HEADS, GDN_HEADS]
ODD_SPLITS = [NSA_W] + [NSA_KV_W] * 6 + [NSA_HEADS * 3, LRU_WIDTH, LRU_WIDTH]
EVEN_IN = sum(EVEN_SPLITS)
ODD_IN = sum(ODD_SPLITS)

kernel_name = "hybrid_fox_gdn_nsa_rglru_trunk"


def rms_norm(x, w):
    xf = x.astype(jnp.float32)
    y = xf * lax.rsqrt(jnp.mean(xf * xf, axis=-1, keepdims=True) + NORM_EPS)
    return (y * w.astype(jnp.float32)).astype(x.dtype)


def split_cols(y, sizes):
    return jnp.split(y, [int(c) for c in np.cumsum(sizes)[:-1]], axis=-1)


def masked_softmax(s, mask):
    p = jax.nn.softmax(jnp.where(mask, s, NEG_INF), axis=-1)
    return jnp.where(mask, p, 0.0)


def rope_tables(positions):
    half = ROT_DIM // 2
    inv_freq = ROPE_THETA ** (-jnp.arange(half, dtype=jnp.float32) * (2.0 / ROT_DIM))
    ang = positions.astype(jnp.float32)[..., None] * inv_freq
    return jnp.cos(ang)[:, :, None, :], jnp.sin(ang)[:, :, None, :]


def apply_partial_rope(x, cos, sin):
    half = ROT_DIM // 2
    x1 = x[..., :half].astype(jnp.float32)
    x2 = x[..., half:ROT_DIM].astype(jnp.float32)
    rot = jnp.concatenate([x1 * cos - x2 * sin, x2 * cos + x1 * sin], axis=-1).astype(x.dtype)
    return jnp.concatenate([rot, x[..., ROT_DIM:]], axis=-1)


def causal_depthwise_conv(x, w):
    K = w.shape[0]
    S = x.shape[1]
    xp = jnp.pad(x, ((0, 0), (K - 1, 0), (0, 0)))
    y = xp[:, 0:S] * w[0]
    for j in range(1, K):
        y = y + xp[:, j:j + S] * w[j]
    return y


def fox_attention(q, k, v, f_logit):
    B, S, H, Dh = q.shape
    nb = S // FOX_QBLOCK
    logf = jax.nn.log_sigmoid(f_logit.astype(jnp.float32))
    F = jnp.cumsum(logf, axis=1).transpose(0, 2, 1)
    q_blocks = q.reshape(B, nb, FOX_QBLOCK, H, Dh).transpose(1, 0, 2, 3, 4)
    F_blocks = F.reshape(B, H, nb, FOX_QBLOCK).transpose(2, 0, 1, 3)
    key_pos = jnp.arange(S)
    scale = Dh ** -0.5

    def one_block(args):
        i, q_i, F_i = args
        t = i * FOX_QBLOCK + jnp.arange(FOX_QBLOCK)
        s = jnp.einsum('bqhd,bkhd->bhqk', q_i, k).astype(jnp.float32) * scale
        s = s + F_i[..., :, None] - F[:, :, None, :]
        p = masked_softmax(s, key_pos[None, :] <= t[:, None])
        return jnp.einsum('bhqk,bkhd->bqhd', p.astype(v.dtype), v)

    out = lax.map(one_block, (jnp.arange(nb), q_blocks, F_blocks))
    return out.transpose(1, 0, 2, 3, 4).reshape(B, S, H, Dh)


def gated_deltanet(q, k, v, z, b_logit, a_logit, a_log, dt_bias, norm_w):
    B, S, H, Dk = q.shape
    Dv = v.shape[-1]
    C = GDN_CHUNK
    N = S // C
    f32 = jnp.float32

    def l2n(t):
        tf = t.astype(f32)
        return tf * lax.rsqrt(jnp.sum(tf * tf, axis=-1, keepdims=True) + 1e-6)

    qf = l2n(q) * (Dk ** -0.5)
    kf = l2n(k)
    vf = v.astype(f32)
    beta = jax.nn.sigmoid(b_logit.astype(f32))
    g = -jnp.exp(a_log.astype(f32)) * jax.nn.softplus(a_logit.astype(f32) + dt_bias.astype(f32))

    def chunks(t):
        return t.reshape(B, N, C, H, -1).transpose(0, 3, 1, 2, 4)

    qc, kc, vc = chunks(qf), chunks(kf), chunks(vf)
    beta_c = chunks(beta[..., None])[..., 0]
    g_cum = jnp.cumsum(chunks(g[..., None])[..., 0], axis=-1)
    idx = jnp.arange(C)
    incl = idx[:, None] >= idx[None, :]
    strict = idx[:, None] > idx[None, :]
    decay = jnp.where(incl, jnp.exp(jnp.where(incl, g_cum[..., :, None] - g_cum[..., None, :], 0.0)), 0.0)
    k_beta = kc * beta_c[..., None]
    L = jnp.where(strict, jnp.einsum('bhnid,bhnjd->bhnij', k_beta, kc) * decay, 0.0)
    lhs = L + jnp.eye(C, dtype=f32)
    rhs = jnp.concatenate([vc * beta_c[..., None], k_beta * jnp.exp(g_cum)[..., None]], axis=-1)
    sol = lax.linalg.triangular_solve(lhs, rhs, left_side=True, lower=True, unit_diagonal=True)
    u, w = sol[..., :Dv], sol[..., Dv:]
    attn_intra = jnp.where(incl, jnp.einsum('bhnid,bhnjd->bhnij', qc, kc) * decay, 0.0)

    def step(state, inp):
        q_i, k_i, u_i, w_i, g_i, a_i = inp
        v_new = u_i - jnp.einsum('bhck,bhkv->bhcv', w_i, state)
        o_i = (jnp.einsum('bhck,bhkv->bhcv', q_i * jnp.exp(g_i)[..., None], state)
               + jnp.einsum('bhij,bhjv->bhiv', a_i, v_new))
        g_last = g_i[..., -1:]
        state = (state * jnp.exp(g_last)[..., None]
                 + jnp.einsum('bhck,bhcv->bhkv', k_i * jnp.exp(g_last - g_i)[..., None], v_new))
        return state, o_i

    xs = tuple(jnp.moveaxis(t, 2, 0) for t in (qc, kc, u, w, g_cum, attn_intra))
    _, o = lax.scan(step, jnp.zeros((B, H, Dk, Dv), f32), xs)
    o = o.transpose(1, 0, 3, 2, 4).reshape(B, S, H, Dv)
    o = o * lax.rsqrt(jnp.mean(o * o, axis=-1, keepdims=True) + NORM_EPS) * norm_w.astype(f32)
    o = o * jax.nn.silu(z.astype(f32))
    return o.astype(q.dtype)


def compress_tokens(x, pe, w1, w2):
    S = x.shape[1]
    n_cmp = (S - CMP_BLOCK) // CMP_STRIDE + 1
    idx = (np.arange(n_cmp) * CMP_STRIDE)[:, None] + np.arange(CMP_BLOCK)[None, :]
    blocks = x[:, idx] + pe[None, None, :, None, :]
    B, n, L, G, Dh = blocks.shape
    flat = blocks.transpose(0, 1, 3, 2, 4).reshape(B, n, G, L * Dh)
    hid = jax.nn.gelu(jnp.einsum('bngf,fh->bngh', flat, w1))
    return jnp.einsum('bngh,hd->bngd', hid, w2)


def nsa_attention(q, k_cmp, v_cmp, k_slc, v_slc, k_win, v_win, gate_logit):
    B, S, H, Dh = q.shape
    G = k_slc.shape[2]
    hpg = H // G
    f32 = jnp.float32
    scale = Dh ** -0.5
    n_cmp = k_cmp.shape[1]
    n_sel = S // SEL_BLOCK
    top_k = min(SEL_TOPK, n_sel)
    QB = NSA_QBLOCK
    nb = S // QB
    cmp_end = jnp.arange(n_cmp) * CMP_STRIDE + CMP_BLOCK - 1
    cs = np.arange(n_cmp) * CMP_STRIDE
    ss = np.arange(n_sel) * SEL_BLOCK
    overlap = jnp.asarray(((cs[:, None] <= ss[None, :] + SEL_BLOCK - 1)
                           & (cs[:, None] + CMP_BLOCK - 1 >= ss[None, :])).astype(np.float32))
    sel_k = k_slc.reshape(B, n_sel, SEL_BLOCK, G, Dh).transpose(0, 3, 1, 2, 4)
    sel_v = v_slc.reshape(B, n_sel, SEL_BLOCK, G, Dh).transpose(0, 3, 1, 2, 4)
    k_win_p = jnp.pad(k_win, ((0, 0), (WINDOW, 0), (0, 0), (0, 0)))
    v_win_p = jnp.pad(v_win, ((0, 0), (WINDOW, 0), (0, 0), (0, 0)))
    gates = jax.nn.sigmoid(gate_logit.astype(f32)).reshape(B, S, H, 3)
    q_blocks = q.reshape(B, nb, QB, G, hpg, Dh).transpose(1, 0, 3, 4, 2, 5)
    gate_blocks = gates.reshape(B, nb, QB, H, 3).transpose(1, 0, 2, 3, 4)
    gather_blocks = jax.vmap(jax.vmap(lambda blk, ix: blk[ix]))
    blk_ids = jnp.arange(n_sel)
    in_blk = jnp.arange(SEL_BLOCK)
    win_off = jnp.arange(WINDOW + QB) - WINDOW

    def one_block(args):
        i, q_i, g_i = args
        t = i * QB + jnp.arange(QB)
        s_c = jnp.einsum('bghqd,bngd->bghqn', q_i, k_cmp).astype(f32) * scale
        p_c = masked_softmax(s_c, cmp_end[None, :] <= t[:, None])
        o_c = jnp.einsum('bghqn,bngd->bghqd', p_c.astype(v_cmp.dtype), v_cmp)
        imp = jnp.einsum('bgqn,nm->bgqm', p_c.sum(axis=2), overlap)
        cur = t // SEL_BLOCK
        forced = ((blk_ids[None, :] == 0) | (blk_ids[None, :] == cur[:, None])
                  | (blk_ids[None, :] == cur[:, None] - 1))
        future = blk_ids[None, :] * SEL_BLOCK > t[:, None]
        imp = jnp.where(future, NEG_INF, jnp.where(forced, -NEG_INF, imp))
        _, sel = lax.top_k(imp, top_k)
        k_sel = gather_blocks(sel_k, sel).reshape(B, G, QB, top_k * SEL_BLOCK, Dh)
        v_sel = gather_blocks(sel_v, sel).reshape(B, G, QB, top_k * SEL_BLOCK, Dh)
        sel_pos = (sel[..., None] * SEL_BLOCK + in_blk).reshape(B, G, QB, top_k * SEL_BLOCK)
        s_s = jnp.einsum('bghqd,bgqmd->bghqm', q_i, k_sel).astype(f32) * scale
        p_s = masked_softmax(s_s, (sel_pos <= t[:, None])[:, :, None])
        o_s = jnp.einsum('bghqm,bgqmd->bghqd', p_s.astype(v_slc.dtype), v_sel)
        kw = lax.dynamic_slice_in_dim(k_win_p, i * QB, WINDOW + QB, axis=1)
        vw = lax.dynamic_slice_in_dim(v_win_p, i * QB, WINDOW + QB, axis=1)
        wpos = i * QB + win_off
        mask_w = ((wpos[None, :] <= t[:, None]) & (t[:, None] - wpos[None, :] < WINDOW)
                  & (wpos[None, :] >= 0))
        s_w = jnp.einsum('bghqd,bkgd->bghqk', q_i, kw).astype(f32) * scale
        p_w = masked_softmax(s_w, mask_w)
        o_w = jnp.einsum('bghqk,bkgd->bghqd', p_w.astype(v_win.dtype), vw)
        gg = g_i.reshape(B, QB, G, hpg, 3).transpose(0, 2, 3, 1, 4)
        o = gg[..., 0:1] * o_c + gg[..., 1:2] * o_s + gg[..., 2:3] * o_w
        return o.astype(q.dtype)

    out = lax.map(one_block, (jnp.arange(nb), q_blocks, gate_blocks))
    return out.transpose(1, 0, 4, 2, 3, 5).reshape(B, S, H, Dh)


def rg_lru_block(gate_in, x_in, conv_w, conv_b, wa, ba, wx, bx, lam):
    f32 = jnp.float32
    x = causal_depthwise_conv(x_in, conv_w) + conv_b
    B, S, W = x.shape
    nblk = wa.shape[0]
    xb = x.reshape(B, S, nblk, W // nblk)
    r = jax.nn.sigmoid((jnp.einsum('bsnk,nkj->bsnj', xb, wa).reshape(B, S, W) + ba).astype(f32))
    ig = jax.nn.sigmoid((jnp.einsum('bsnk,nkj->bsnj', xb, wx).reshape(B, S, W) + bx).astype(f32))
    log_a = -RG_C * jax.nn.softplus(-lam.astype(f32)) * r
    a = jnp.exp(log_a)
    b = jnp.sqrt(-jnp.expm1(2.0 * log_a)) * (ig * x.astype(f32))

    def combine(left, right):
        a_l, b_l = left
        a_r, b_r = right
        return a_l * a_r, a_r * b_l + b_r

    _, h = lax.associative_scan(combine, (a, b), axis=1)
    y = h * jax.nn.gelu(gate_in.astype(f32))
    return y.astype(x_in.dtype)


def even_mixer(h, w_in, fox_bf, gdn_conv_w, gdn_a_log, gdn_dt_bias, gdn_norm_w, w_out):
    B, S, _ = h.shape
    y = h @ w_in
    fq, fk, fv, ff, gq, gk, gv, gz, gb, ga = split_cols(y, EVEN_SPLITS)
    fox = fox_attention(fq.reshape(B, S, FOX_HEADS, HEAD_DIM), fk.reshape(B, S, FOX_HEADS, HEAD_DIM),
                        fv.reshape(B, S, FOX_HEADS, HEAD_DIM), ff + fox_bf)
    qkv = jax.nn.silu(causal_depthwise_conv(jnp.concatenate([gq, gk, gv], axis=-1), gdn_conv_w))
    cq, ck, cv = jnp.split(qkv, 3, axis=-1)
    hd = lambda t: t.reshape(B, S, GDN_HEADS, GDN_HEAD_DIM)
    gdn = gated_deltanet(hd(cq), hd(ck), hd(cv), hd(gz), gb, ga, gdn_a_log, gdn_dt_bias, gdn_norm_w)
    mixed = jnp.concatenate([fox.reshape(B, S, FOX_W), gdn.reshape(B, S, GDN_W)], axis=-1)
    return mixed @ w_out


def odd_mixer(h, cos, sin, w_in, k_pe, k_w1, k_w2, v_pe, v_w1, v_w2,
              conv_w, conv_b, wa, ba, wx, bx, lam, w_out):
    B, S, _ = h.shape
    y = h @ w_in
    nq, kc, vc, ksl, vsl, kwn, vwn, ng, rg, rx = split_cols(y, ODD_SPLITS)
    hd = lambda t, n: t.reshape(B, S, n, HEAD_DIM)
    G = NSA_KV_GROUPS
    q = apply_partial_rope(hd(nq, NSA_HEADS), cos, sin)
    k_c = compress_tokens(apply_partial_rope(hd(kc, G), cos, sin), k_pe, k_w1, k_w2)
    v_c = compress_tokens(hd(vc, G), v_pe, v_w1, v_w2)
    nsa = nsa_attention(q, k_c, v_c, apply_partial_rope(hd(ksl, G), cos, sin), hd(vsl, G),
                        apply_partial_rope(hd(kwn, G), cos, sin), hd(vwn, G), ng)
    lru = rg_lru_block(rg, rx, conv_w, conv_b, wa, ba, wx, bx, lam)
    mixed = jnp.concatenate([nsa.reshape(B, S, NSA_W), lru], axis=-1)
    return mixed @ w_out


def setup_inputs(seed: int = 0) -> dict:
    key = jax.random.key(seed)
    ks = iter(jax.random.split(key, 48))
    f32 = jnp.float32
    nrm = lambda shape, scale: jax.random.normal(next(ks), shape, f32) * scale
    gain = lambda shape: 1.0 + nrm(shape, 0.02)
    ne, no = (DEPTH + 1) // 2, DEPTH // 2
    D = D_MODEL
    x = nrm((BATCH, SEQ, D), 1.0)
    p = nrm((DEPTH, BATCH, SEQ, PLE_DIM), 1.0)
    positions = (jax.random.randint(next(ks), (BATCH, 1), 0, MAX_POS_OFFSET, jnp.int32)
                 + jnp.arange(SEQ, dtype=jnp.int32)[None, :])
    even_norm_mix = gain((ne, D))
    even_w_in = nrm((ne, D, EVEN_IN), D ** -0.5)
    even_fox_bf = 2.0 + nrm((ne, FOX_HEADS), 0.1)
    even_gdn_conv_w = nrm((ne, CONV_WIDTH, 3 * GDN_W), CONV_WIDTH ** -0.5)
    even_gdn_a_log = jnp.log(jax.random.uniform(next(ks), (ne, GDN_HEADS), f32, 1.0, 16.0))
    dt = jnp.exp(jax.random.uniform(next(ks), (ne, GDN_HEADS), f32, math.log(1e-3), math.log(1e-1)))
    even_gdn_dt_bias = dt + jnp.log(-jnp.expm1(-dt))
    even_gdn_norm_w = gain((ne, GDN_HEAD_DIM))
    even_w_out = nrm((ne, FOX_W + GDN_W, D), (FOX_W + GDN_W) ** -0.5)
    odd_norm_mix = gain((no, D))
    odd_w_in = nrm((no, D, ODD_IN), D ** -0.5)
    odd_cmp_k_pe = nrm((no, CMP_BLOCK, HEAD_DIM), 0.02)
    odd_cmp_k_w1 = nrm((no, CMP_BLOCK * HEAD_DIM, CMP_HIDDEN), (CMP_BLOCK * HEAD_DIM) ** -0.5)
    odd_cmp_k_w2 = nrm((no, CMP_HIDDEN, HEAD_DIM), CMP_HIDDEN ** -0.5)
    odd_cmp_v_pe = nrm((no, CMP_BLOCK, HEAD_DIM), 0.02)
    odd_cmp_v_w1 = nrm((no, CMP_BLOCK * HEAD_DIM, CMP_HIDDEN), (CMP_BLOCK * HEAD_DIM) ** -0.5)
    odd_cmp_v_w2 = nrm((no, CMP_HIDDEN, HEAD_DIM), CMP_HIDDEN ** -0.5)
    odd_rg_conv_w = nrm((no, CONV_WIDTH, LRU_WIDTH), CONV_WIDTH ** -0.5)
    odd_rg_conv_b = nrm((no, LRU_WIDTH), 0.01)
    bw = LRU_WIDTH // LRU_BLOCKS
    odd_rg_wa = nrm((no, LRU_BLOCKS, bw, bw), bw ** -0.5)
    odd_rg_ba = nrm((no, LRU_WIDTH), 0.01)
    odd_rg_wx = nrm((no, LRU_BLOCKS, bw, bw), bw ** -0.5)
    odd_rg_bx = nrm((no, LRU_WIDTH), 0.01)
    a_c = jax.random.uniform(next(ks), (no, LRU_WIDTH), f32, 0.9, 0.999)
    a0 = a_c ** (1.0 / RG_C)
    odd_rg_lambda = jnp.log(a0) - jnp.log1p(-a0)
    odd_w_out = nrm((no, NSA_W + LRU_WIDTH, D), (NSA_W + LRU_WIDTH) ** -0.5)
    mlp_norm = gain((DEPTH, D))
    mlp_w_up = nrm((DEPTH, D, D_FF), D ** -0.5)
    mlp_w_down = nrm((DEPTH, D_FF, D), D_FF ** -0.5)
    ple_norm = gain((DEPTH, D))
    ple_w_gate = nrm((DEPTH, D, D), D ** -0.5)
    ple_w_proj = nrm((DEPTH, PLE_DIM, D), PLE_DIM ** -0.5)
    final_norm = gain((D,))
    return {"x": x, "p": p, "positions": positions,
            "even_norm_mix": even_norm_mix, "even_w_in": even_w_in, "even_fox_bf": even_fox_bf,
            "even_gdn_conv_w": even_gdn_conv_w, "even_gdn_a_log": even_gdn_a_log,
            "even_gdn_dt_bias": even_gdn_dt_bias, "even_gdn_norm_w": even_gdn_norm_w,
            "even_w_out": even_w_out,
            "odd_norm_mix": odd_norm_mix, "odd_w_in": odd_w_in,
            "odd_cmp_k_pe": odd_cmp_k_pe, "odd_cmp_k_w1": odd_cmp_k_w1, "odd_cmp_k_w2": odd_cmp_k_w2,
            "odd_cmp_v_pe": odd_cmp_v_pe, "odd_cmp_v_w1": odd_cmp_v_w1, "odd_cmp_v_w2": odd_cmp_v_w2,
            "odd_rg_conv_w": odd_rg_conv_w, "odd_rg_conv_b": odd_rg_conv_b,
            "odd_rg_wa": odd_rg_wa, "odd_rg_ba": odd_rg_ba, "odd_rg_wx": odd_rg_wx, "odd_rg_bx": odd_rg_bx,
            "odd_rg_lambda": odd_rg_lambda, "odd_w_out": odd_w_out,
            "mlp_norm": mlp_norm, "mlp_w_up": mlp_w_up, "mlp_w_down": mlp_w_down,
            "ple_norm": ple_norm, "ple_w_gate": ple_w_gate, "ple_w_proj": ple_w_proj,
            "final_norm": final_norm}


def reference(x, p, positions,
              even_norm_mix, even_w_in, even_fox_bf, even_gdn_conv_w, even_gdn_a_log,
              even_gdn_dt_bias, even_gdn_norm_w, even_w_out,
              odd_norm_mix, odd_w_in, odd_cmp_k_pe, odd_cmp_k_w1, odd_cmp_k_w2,
              odd_cmp_v_pe, odd_cmp_v_w1, odd_cmp_v_w2, odd_rg_conv_w, odd_rg_conv_b,
              odd_rg_wa, odd_rg_ba, odd_rg_wx, odd_rg_bx, odd_rg_lambda, odd_w_out,
              mlp_norm, mlp_w_up, mlp_w_down, ple_norm, ple_w_gate, ple_w_proj, final_norm):
    cos, sin = rope_tables(positions)
    h = x
    for i in range(DEPTH):
        j = i // 2
        if i % 2 == 0:
            hn = rms_norm(h, even_norm_mix[j])
            h = h + even_mixer(hn, even_w_in[j], even_fox_bf[j], even_gdn_conv_w[j], even_gdn_a_log[j],
                               even_gdn_dt_bias[j], even_gdn_norm_w[j], even_w_out[j])
        else:
            hn = rms_norm(h, odd_norm_mix[j])
            h = h + odd_mixer(hn, cos, sin, odd_w_in[j], odd_cmp_k_pe[j], odd_cmp_k_w1[j], odd_cmp_k_w2[j],
                              odd_cmp_v_pe[j], odd_cmp_v_w1[j], odd_cmp_v_w2[j], odd_rg_conv_w[j],
                              odd_rg_conv_b[j], odd_rg_wa[j], odd_rg_ba[j], odd_rg_wx[j], odd_rg_bx[j],
                              odd_rg_lambda[j], odd_w_out[j])
        hn = rms_norm(h, mlp_norm[i])
        u = jax.nn.relu(hn @ mlp_w_up[i])
        h = h + (u * u) @ mlp_w_down[i]
        hn = rms_norm(h, ple_norm[i])
        h = h + jax.nn.sigmoid(hn @ ple_w_gate[i]) * (p[i] @ ple_w_proj[i])
    return rms_norm(h, final_norm)
```

```python
import functools

import numpy as np
import jax
import jax.numpy as jnp
from jax import lax
from jax.experimental import pallas as pl
from jax.experimental.pallas import tpu as pltpu

F32 = jnp.float32
BF16 = jnp.bfloat16
HIGHEST = lax.Precision.HIGHEST

NORM_EPS = 1e-6
NEG = -1e30
REMOVED = -3e38
LANES = 128
SUBLANES = 8
VMEM_LIMIT = 56 * 1024 * 1024

HEAD_DIM = 64
ROT_DIM = 16
ROPE_THETA = 500000.0
FOX_HEADS = 8
GDN_HEADS = 4
GDN_DIM = 128
GDN_CHUNK = 64
NSA_HEADS = 8
NSA_GROUPS = 2
NSA_HPG = NSA_HEADS // NSA_GROUPS
CMP_BLOCK = 32
CMP_STRIDE = 16
CMP_HIDDEN = 128
SEL_BLOCK = 64
SEL_TOPK = 16
WINDOW = 512
LRU_W = 512
RG_C = 8.0
CONV_W = 4


def _cparams(*sem):
    return pltpu.CompilerParams(dimension_semantics=sem, vmem_limit_bytes=VMEM_LIMIT)


def _dot(a, b, **kw):
    return jnp.dot(a, b, preferred_element_type=F32, **kw)


def _dot_nt(a, b, **kw):
    return lax.dot_general(a, b, (((1,), (1,)), ((), ())), preferred_element_type=F32, **kw)


def _dot_tn(a, b, **kw):
    return lax.dot_general(a, b, (((0,), (0,)), ((), ())), preferred_element_type=F32, **kw)


def _softplus(x):
    return jnp.maximum(x, 0.0) + jnp.log1p(jnp.exp(-jnp.abs(x)))


def _sigmoid(x):
    return 1.0 / (1.0 + jnp.exp(-x))


def _silu(x):
    return x * _sigmoid(x)


def _gelu_tanh(x):
    return 0.5 * x * (1.0 + jnp.tanh(np.float32(np.sqrt(2.0 / np.pi)) * (x + 0.044715 * (x * x * x))))


def _rms(x, w):
    ms = jnp.mean(x * x, axis=-1, keepdims=True)
    return x * lax.rsqrt(ms + NORM_EPS) * w


def _apply_rope(y, c, s1, s2):
    outs = []
    for g in range(y.shape[1] // LANES):
        yg = y[:, g * LANES:(g + 1) * LANES]
        outs.append(yg * c + pltpu.roll(yg, LANES - ROT_DIM // 2, 1) * s1 + pltpu.roll(yg, ROT_DIM // 2, 1) * s2)
    return outs[0] if len(outs) == 1 else jnp.concatenate(outs, axis=1)


def _norm_proj_kernel(*refs, segs, n_out, rope):
    if rope:
        x_ref, nw_ref, w_ref, c_ref, s1_ref, s2_ref = refs[:6]
        out_refs = refs[6:6 + n_out]
    else:
        x_ref, nw_ref, w_ref = refs[:3]
        out_refs = refs[3:3 + n_out]
    xn = _rms(x_ref[...], nw_ref[...]).astype(BF16)
    for (oi, oc, wc, width, do_rope, scale) in segs:
        y = _dot(xn, w_ref[:, wc:wc + width])
        if do_rope:
            y = _apply_rope(y, c_ref[...], s1_ref[...], s2_ref[...])
        if scale != 1.0:
            y = y * scale
        out_refs[oi][:, oc:oc + width] = y.astype(out_refs[oi].dtype)


def _norm_proj(x, nw, w, segs, out_defs, rope_tabs=None, tm=512):
    T, D = x.shape
    N = w.shape[1]
    tm = min(tm, T)
    rope = rope_tabs is not None
    in_specs = [pl.BlockSpec((tm, D), lambda i: (i, 0)),
                pl.BlockSpec((1, D), lambda i: (0, 0)),
                pl.BlockSpec((D, N), lambda i: (0, 0))]
    args = [x, nw.reshape(1, D), w]
    if rope:
        in_specs += [pl.BlockSpec((tm, LANES), lambda i: (i, 0))] * 3
        args += list(rope_tabs)
    out_shape = [jax.ShapeDtypeStruct((T, wd), dt) for wd, dt in out_defs]
    out_specs = [pl.BlockSpec((tm, wd), lambda i: (i, 0)) for wd, _ in out_defs]
    return pl.pallas_call(
        functools.partial(_norm_proj_kernel, segs=tuple(segs), n_out=len(out_defs), rope=rope),
        grid=(T // tm,), in_specs=in_specs, out_specs=out_specs, out_shape=out_shape,
        compiler_params=_cparams("parallel"), name="norm_proj",
    )(*args)


def _rope_table_kernel(pos_ref, f_ref, cos_ref, sin_ref, nsin_ref):
    ang = pos_ref[0] * f_ref[...]
    c = jnp.cos(ang)
    s = jnp.sin(ang)
    cos_ref[0] = c
    sin_ref[0] = s
    nsin_ref[0] = -s


def _rope_tables(positions):
    B, S = positions.shape
    half = ROT_DIM // 2
    inv_freq = ROPE_THETA ** (-jnp.arange(half, dtype=F32) * (2.0 / ROT_DIM))
    pos = positions.astype(F32).reshape(B, 1, S)
    sh = jax.ShapeDtypeStruct((B, half, S), F32)
    spec = pl.BlockSpec((1, half, S), lambda b: (b, 0, 0))
    cos, sin, nsin = pl.pallas_call(
        _rope_table_kernel, grid=(B,),
        in_specs=[pl.BlockSpec((1, 1, S), lambda b: (b, 0, 0)), pl.BlockSpec((half, 1), lambda b: (0, 0))],
        out_specs=[spec, spec, spec], out_shape=[sh, sh, sh],
        compiler_params=_cparams("parallel"), name="rope_tables",
    )(pos, inv_freq.reshape(half, 1))
    tr = lambda t: t.transpose(0, 2, 1).reshape(B * S, half)
    cos, sin, nsin = tr(cos), tr(sin), tr(nsin)
    T = B * S
    ones = jnp.ones((T, HEAD_DIM - ROT_DIM), F32)
    zeros = jnp.zeros((T, HEAD_DIM - ROT_DIM), F32)
    z8 = jnp.zeros((T, half), F32)
    c = jnp.tile(jnp.concatenate([cos, cos, ones], axis=1), (1, LANES // HEAD_DIM))
    s1 = jnp.tile(jnp.concatenate([nsin, z8, zeros], axis=1), (1, LANES // HEAD_DIM))
    s2 = jnp.tile(jnp.concatenate([z8, sin, zeros], axis=1), (1, LANES // HEAD_DIM))
    return c, s1, s2


def _out_proj_kernel(h_ref, a1_ref, a2_ref, w1_ref, w2_ref, o_ref):
    y = _dot(a1_ref[...].astype(BF16), w1_ref[...]) + _dot(a2_ref[...].astype(BF16), w2_ref[...])
    o_ref[...] = h_ref[...] + y


def _out_proj(h, a1, a2, w1, w2, tm=512):
    T, D = h.shape
    tm = min(tm, T)
    K1, K2 = a1.shape[1], a2.shape[1]
    return pl.pallas_call(
        _out_proj_kernel, grid=(T // tm,),
        in_specs=[pl.BlockSpec((tm, D), lambda i: (i, 0)),
                  pl.BlockSpec((tm, K1), lambda i: (i, 0)),
                  pl.BlockSpec((tm, K2), lambda i: (i, 0)),
                  pl.BlockSpec((K1, D), lambda i: (0, 0)),
                  pl.BlockSpec((K2, D), lambda i: (0, 0))],
        out_specs=pl.BlockSpec((tm, D), lambda i: (i, 0)),
        out_shape=jax.ShapeDtypeStruct((T, D), F32),
        compiler_params=_cparams("parallel"), name="out_proj",
    )(h, a1, a2, w1, w2)


def _mlp_kernel(h_ref, nw_ref, wu_ref, wd_ref, o_ref, xn_ref, acc_ref):
    f = pl.program_id(1)

    @pl.when(f == 0)
    def _():
        xn_ref[...] = _rms(h_ref[...], nw_ref[...]).astype(BF16)
        acc_ref[...] = h_ref[...]

    u = jnp.maximum(_dot(xn_ref[...], wu_ref[...]), 0.0)
    acc_ref[...] += _dot((u * u).astype(BF16), wd_ref[...])

    @pl.when(f == pl.num_programs(1) - 1)
    def _():
        o_ref[...] = acc_ref[...]


def _mlp(h, nw, wu, wd, tm=512, tf=1024):
    T, D = h.shape
    FF = wu.shape[1]
    tm = min(tm, T)
    return pl.pallas_call(
        _mlp_kernel, grid=(T // tm, FF // tf),
        in_specs=[pl.BlockSpec((tm, D), lambda i, f: (i, 0)),
                  pl.BlockSpec((1, D), lambda i, f: (0, 0)),
                  pl.BlockSpec((D, tf), lambda i, f: (0, f)),
                  pl.BlockSpec((tf, D), lambda i, f: (f, 0))],
        out_specs=pl.BlockSpec((tm, D), lambda i, f: (i, 0)),
        out_shape=jax.ShapeDtypeStruct((T, D), F32),
        scratch_shapes=[pltpu.VMEM((tm, D), BF16), pltpu.VMEM((tm, D), F32)],
        compiler_params=_cparams("parallel", "arbitrary"), name="mlp",
    )(h, nw.reshape(1, D), wu, wd)


def _ple_kernel(h_ref, p_ref, nw_ref, wg_ref, wp_ref, fw_ref, o_ref, *, final):
    h = h_ref[...]
    xn = _rms(h, nw_ref[...]).astype(BF16)
    gate = _sigmoid(_dot(xn, wg_ref[...]))
    y = h + gate * _dot(p_ref[...].astype(BF16), wp_ref[...])
    if final:
        y = _rms(y, fw_ref[...])
    o_ref[...] = y


def _ple(h, p, nw, wg, wp, fw, final, tm=512):
    T, D = h.shape
    P = p.shape[1]
    tm = min(tm, T)
    return pl.pallas_call(
        functools.partial(_ple_kernel, final=final), grid=(T // tm,),
        in_specs=[pl.BlockSpec((tm, D), lambda i: (i, 0)),
                  pl.BlockSpec((tm, P), lambda i: (i, 0)),
                  pl.BlockSpec((1, D), lambda i: (0, 0)),
                  pl.BlockSpec((D, D), lambda i: (0, 0)),
                  pl.BlockSpec((P, D), lambda i: (0, 0)),
                  pl.BlockSpec((1, D), lambda i: (0, 0))],
        out_specs=pl.BlockSpec((tm, D), lambda i: (i, 0)),
        out_shape=jax.ShapeDtypeStruct((T, D), F32),
        compiler_params=_cparams("parallel"), name="ple",
    )(h, p, nw.reshape(1, D), wg, wp, fw.reshape(1, D))


def _fox_gate_kernel(x_ref, b_ref, o_ref):
    x = x_ref[0] + b_ref[0]
    logf = jnp.minimum(x, 0.0) - jnp.log1p(jnp.exp(-jnp.abs(x)))
    nr = x.shape[0]
    r = lax.broadcasted_iota(jnp.int32, (LANES, LANES), 0)
    c = lax.broadcasted_iota(jnp.int32, (LANES, LANES), 1)
    upper = (r <= c).astype(F32)
    within = _dot(logf, upper, precision=HIGHEST)
    tot = jnp.broadcast_to(within[:, LANES - 1:LANES], (nr, LANES))
    rr = lax.broadcasted_iota(jnp.int32, (nr, nr), 0)
    cc = lax.broadcasted_iota(jnp.int32, (nr, nr), 1)
    strict = (cc < rr).astype(F32)
    o_ref[0] = within + _dot(strict, tot, precision=HIGHEST)


def _fox_gate(f_logit, bias):
    B, S, H = f_logit.shape
    nr = S // LANES
    x = f_logit.transpose(0, 2, 1).reshape(B * H, nr, LANES)
    b = jnp.broadcast_to(jnp.tile(bias.astype(F32), B)[:, None, None], (B * H, 1, LANES))
    out = pl.pallas_call(
        _fox_gate_kernel, grid=(B * H,),
        in_specs=[pl.BlockSpec((1, nr, LANES), lambda i: (i, 0, 0)),
                  pl.BlockSpec((1, 1, LANES), lambda i: (i, 0, 0))],
        out_specs=pl.BlockSpec((1, nr, LANES), lambda i: (i, 0, 0)),
        out_shape=jax.ShapeDtypeStruct((B * H, nr, LANES), F32),
        compiler_params=_cparams("parallel"), name="fox_gate",
    )(x, b)
    return out.reshape(B, H, S)


def _fox_attn_kernel(q_ref, k_ref, v_ref, f_ref, o_ref, *, tq):
    p = pl.program_id(1)
    i = pl.program_id(2)
    outs = []
    for hh in range(2):
        lo, hi = hh * HEAD_DIM, (hh + 1) * HEAD_DIM
        q = q_ref[:, lo:hi]
        row = 2 * p + hh

        def step(j, carry, masked):
            m, l, acc = carry
            start = pl.multiple_of(j * tq, tq)
            k = k_ref[pl.ds(start, tq), lo:hi]
            v = v_ref[pl.ds(start, tq), lo:hi]
            fk = f_ref[0, row, pl.ds(j, 1), :]
            s = _dot_nt(q, k) - fk
            if masked:
                r = lax.broadcasted_iota(jnp.int32, (tq, tq), 0)
                c = lax.broadcasted_iota(jnp.int32, (tq, tq), 1)
                s = jnp.where(c <= r, s, NEG)
            m_new = jnp.maximum(m, jnp.max(s, axis=1, keepdims=True))
            alpha = jnp.exp(m - m_new)
            pr = jnp.exp(s - m_new)
            l = alpha * l + jnp.sum(pr, axis=1, keepdims=True)
            acc = alpha * acc + _dot(pr.astype(BF16), v)
            return m_new, l, acc

        init = (jnp.full((tq, 1), NEG, F32), jnp.zeros((tq, 1), F32), jnp.zeros((tq, HEAD_DIM), F32))
        carry = lax.fori_loop(0, i, functools.partial(step, masked=False), init)
        m, l, acc = step(i, carry, True)
        outs.append(acc / l)
    o_ref[...] = jnp.concatenate(outs, axis=1).astype(o_ref.dtype)


def _fox_attn(qkv, F, B, S, tq=512):
    T = B * S
    tq = min(tq, S)
    nq = S // tq
    W = FOX_HEADS * HEAD_DIM
    npair = W // LANES
    Fr = F.reshape(B, FOX_HEADS, nq, tq)
    return pl.pallas_call(
        functools.partial(_fox_attn_kernel, tq=tq), grid=(B, npair, nq),
        in_specs=[pl.BlockSpec((tq, LANES), lambda b, p, i: (b * nq + i, p)),
                  pl.BlockSpec((S, LANES), lambda b, p, i: (b, npair + p)),
                  pl.BlockSpec((S, LANES), lambda b, p, i: (b, 2 * npair + p)),
                  pl.BlockSpec((1, FOX_HEADS, nq, tq), lambda b, p, i: (b, 0, 0, 0))],
        out_specs=pl.BlockSpec((tq, LANES), lambda b, p, i: (b * nq + i, p)),
        out_shape=jax.ShapeDtypeStruct((T, W), BF16),
        compiler_params=_cparams("parallel", "parallel", "arbitrary"), name="fox_attn",
    )(qkv, qkv, qkv, Fr)


def _gdn_kernel(x_ref, z_ref, sm_ref, cw_ref, par_ref, nw_ref, o_ref, state_ref, xp_ref):
    C = GDN_CHUNK
    D = GDN_DIM
    W = GDN_HEADS * D
    n = pl.program_id(1)

    @pl.when(n == 0)
    def _():
        state_ref[...] = jnp.zeros_like(state_ref)
        xp_ref[0:SUBLANES, :] = jnp.zeros((SUBLANES, 3 * W), F32)

    x = x_ref[...]
    xp_ref[SUBLANES:SUBLANES + C, :] = x
    cw = cw_ref[...]
    y = x * cw[CONV_W - 1:CONV_W]
    for j in range(CONV_W - 1):
        off = SUBLANES - (CONV_W - 1) + j
        y = y + xp_ref[off:off + C, :] * cw[j:j + 1]
    xp_ref[0:SUBLANES, :] = x[C - SUBLANES:C]
    y = _silu(y)

    sm = sm_ref[...]
    beta_all = _sigmoid(sm)
    g_all = -jnp.exp(par_ref[0:1, :]) * _softplus(sm + par_ref[1:2, :])
    ri = lax.broadcasted_iota(jnp.int32, (C, C), 0)
    ci = lax.broadcasted_iota(jnp.int32, (C, C), 1)
    incl = ri >= ci
    strict = ri > ci
    gc = _dot(incl.astype(F32), g_all, precision=HIGHEST)
    gct = gc.T
    eye = (ri == ci).astype(F32)

    outs = []
    for h in range(GDN_HEADS):
        q = y[:, h * D:(h + 1) * D]
        k = y[:, W + h * D:W + (h + 1) * D]
        v = y[:, 2 * W + h * D:2 * W + (h + 1) * D]
        qn = q * lax.rsqrt(jnp.sum(q * q, axis=-1, keepdims=True) + 1e-6) * (D ** -0.5)
        kn = k * lax.rsqrt(jnp.sum(k * k, axis=-1, keepdims=True) + 1e-6)
        beta = beta_all[:, 8 + h:9 + h]
        gcol = gc[:, 12 + h:13 + h]
        grow = gct[12 + h:13 + h, :]
        decay = jnp.where(incl, jnp.exp(jnp.where(incl, gcol - grow, 0.0)), 0.0)
        kb = kn * beta
        kn16 = kn.astype(BF16)
        L = jnp.where(strict, _dot_nt(kb.astype(BF16), kn16) * decay, 0.0)
        P = -L
        Tm = eye + P
        for _ in range(5):
            P = _dot(P, P, precision=HIGHEST)
            Tm = Tm + _dot(Tm, P, precision=HIGHEST)
        rhs = jnp.concatenate([v * beta, kb * jnp.exp(gcol)], axis=1)
        sol = _dot(Tm, rhs, precision=HIGHEST)
        u, w = sol[:, :D], sol[:, D:]
        attn = jnp.where(incl, _dot_nt(qn.astype(BF16), kn16) * decay, 0.0)
        st = state_ref[h]
        st16 = st.astype(BF16)
        v_new = u - _dot(w.astype(BF16), st16)
        o = _dot((qn * jnp.exp(gcol)).astype(BF16), st16) + _dot(attn.astype(BF16), v_new.astype(BF16))
        g_last = gcol[C - 1:C, :]
        state_ref[h] = st * jnp.exp(g_last) + _dot_tn((kn * jnp.exp(g_last - gcol)).astype(BF16), v_new.astype(BF16))
        o = _rms(o, nw_ref[...]) * _silu(z_ref[:, h * D:(h + 1) * D])
        outs.append(o)
    o_ref[...] = jnp.concatenate(outs, axis=1).astype(o_ref.dtype)


def _gdn(gqkv, gz, small, conv_w, a_log, dt_bias, norm_w, B, S):
    T = B * S
    C = GDN_CHUNK
    N = S // C
    W = GDN_HEADS * GDN_DIM
    par = jnp.zeros((SUBLANES, LANES), F32)
    par = par.at[0, 12:16].set(a_log.astype(F32)).at[1, 12:16].set(dt_bias.astype(F32))
    return pl.pallas_call(
        _gdn_kernel, grid=(B, N),
        in_specs=[pl.BlockSpec((C, 3 * W), lambda b, n: (b * N + n, 0)),
                  pl.BlockSpec((C, W), lambda b, n: (b * N + n, 0)),
                  pl.BlockSpec((C, LANES), lambda b, n: (b * N + n, 0)),
                  pl.BlockSpec((CONV_W, 3 * W), lambda b, n: (0, 0)),
                  pl.BlockSpec((SUBLANES, LANES), lambda b, n: (0, 0)),
                  pl.BlockSpec((1, GDN_DIM), lambda b, n: (0, 0))],
        out_specs=pl.BlockSpec((C, W), lambda b, n: (b * N + n, 0)),
        out_shape=jax.ShapeDtypeStruct((T, W), BF16),
        scratch_shapes=[pltpu.VMEM((GDN_HEADS, GDN_DIM, GDN_DIM), F32),
                        pltpu.VMEM((SUBLANES + C, 3 * W), F32)],
        compiler_params=_cparams("parallel", "arbitrary"), name="gdn",
    )(gqkv, gz, small, conv_w.astype(F32), par, norm_w.reshape(1, GDN_DIM).astype(F32))


def _compress_kernel(x_ref, pe_ref, w1_ref, w2_ref, o_ref):
    half = w1_ref.shape[1] // 2
    x = x_ref[0, 0].astype(BF16)
    w1 = w1_ref[0]
    a = _dot(x, w1[:half])
    b = _dot(x, w1[half:])
    c = _dot(pe_ref[0].astype(BF16), w1)
    nh = x.shape[0]
    b_next = pltpu.roll(b, nh - 1, 0)
    hid = _gelu_tanh(a + b_next + c[0:1])
    o_ref[0, 0] = _dot(hid.astype(BF16), w2_ref[0]).astype(o_ref.dtype)


def _compress(xkv, pe, w1, w2):
    B, four, nh, fw = xkv.shape
    return pl.pallas_call(
        _compress_kernel, grid=(B, four),
        in_specs=[pl.BlockSpec((1, 1, nh, fw), lambda b, j: (b, j, 0, 0)),
                  pl.BlockSpec((1, SUBLANES, 2 * fw), lambda b, j: (j // 2, 0, 0)),
                  pl.BlockSpec((1, 2 * fw, CMP_HIDDEN), lambda b, j: (j // 2, 0, 0)),
                  pl.BlockSpec((1, CMP_HIDDEN, HEAD_DIM), lambda b, j: (j // 2, 0, 0))],
        out_specs=pl.BlockSpec((1, 1, nh, HEAD_DIM), lambda b, j: (b, j, 0, 0)),
        out_shape=jax.ShapeDtypeStruct((B, four, nh, HEAD_DIM), BF16),
        compiler_params=_cparams("parallel", "parallel"), name="nsa_compress",
    )(xkv, pe, w1, w2)


def _stack_heads(qb):
    return jnp.concatenate([qb[:, h * HEAD_DIM:(h + 1) * HEAD_DIM] for h in range(NSA_HPG)], axis=0)


def _unstack_heads(o, tq):
    return jnp.concatenate([o[h * tq:(h + 1) * tq] for h in range(NSA_HPG)], axis=1)


def _nsa_cmp_kernel(q_ref, kc_ref, vc_ref, ov_ref, oc_ref, bias_ref, *, tq, n_sel, top_k):
    i = pl.program_id(2)
    q4 = _stack_heads(q_ref[...])
    ncp = kc_ref.shape[2]
    s = _dot_nt(q4, kc_ref[0, 0])
    row = lax.broadcasted_iota(jnp.int32, (NSA_HPG * tq, 1), 0)
    t4 = i * tq + (row & (tq - 1))
    cmp_end = lax.broadcasted_iota(jnp.int32, (1, ncp), 1) * CMP_STRIDE + (CMP_BLOCK - 1)
    mask = cmp_end <= t4
    s = jnp.where(mask, s, NEG)
    m = jnp.max(s, axis=1, keepdims=True)
    p = jnp.where(mask, jnp.exp(s - m), 0.0)
    l = jnp.sum(p, axis=1, keepdims=True)
    p = p * jnp.where(l > 0.0, 1.0 / l, 0.0)
    oc = _dot(p.astype(BF16), vc_ref[0, 0])
    oc_ref[...] = _unstack_heads(oc, tq)
    psum = p[0:tq] + p[tq:2 * tq] + p[2 * tq:3 * tq] + p[3 * tq:4 * tq]
    imp = _dot(psum, ov_ref[...], precision=HIGHEST)
    blk = lax.broadcasted_iota(jnp.int32, (1, LANES), 1)
    t = i * tq + lax.broadcasted_iota(jnp.int32, (tq, 1), 0)
    cur = t >> (SEL_BLOCK.bit_length() - 1)
    forced = (blk == 0) | (blk == cur) | (blk == cur - 1)
    future = blk * SEL_BLOCK > t
    imp = jnp.where(future, NEG, jnp.where(forced, -NEG, imp))
    imp = jnp.where(blk < n_sel, imp, REMOVED)
    blk_f = blk.astype(F32)
    sel = jnp.zeros((tq, LANES), F32)
    for _ in range(top_k):
        mx = jnp.max(imp, axis=1, keepdims=True)
        first = jnp.min(jnp.where(imp == mx, blk_f, float(LANES)), axis=1, keepdims=True)
        pick = blk_f == first
        sel = jnp.where(pick, 1.0, sel)
        imp = jnp.where(pick, REMOVED, imp)
    bias_ref[0, 0] = jnp.where((sel > 0.0) & jnp.logical_not(future), 0.0, NEG).astype(bias_ref.dtype)


def _nsa_cmp(q, kc, vc, overlap, B, S, tq=256):
    T = B * S
    tq = min(tq, S)
    nq = S // tq
    ncp = kc.shape[2]
    n_sel = S // SEL_BLOCK
    GW = NSA_HPG * HEAD_DIM
    return pl.pallas_call(
        functools.partial(_nsa_cmp_kernel, tq=tq, n_sel=n_sel, top_k=min(SEL_TOPK, n_sel)),
        grid=(B, NSA_GROUPS, nq),
        in_specs=[pl.BlockSpec((tq, GW), lambda b, g, i: (b * nq + i, g)),
                  pl.BlockSpec((1, 1, ncp, HEAD_DIM), lambda b, g, i: (b, g, 0, 0)),
                  pl.BlockSpec((1, 1, ncp, HEAD_DIM), lambda b, g, i: (b, NSA_GROUPS + g, 0, 0)),
                  pl.BlockSpec((ncp, LANES), lambda b, g, i: (0, 0))],
        out_specs=[pl.BlockSpec((tq, GW), lambda b, g, i: (b * nq + i, g)),
                   pl.BlockSpec((1, 1, tq, LANES), lambda b, g, i: (b, g, i, 0))],
        out_shape=[jax.ShapeDtypeStruct((T, NSA_HEADS * HEAD_DIM), F32),
                   jax.ShapeDtypeStruct((B, NSA_GROUPS, S, LANES), BF16)],
        compiler_params=_cparams("parallel", "parallel", "parallel"), name="nsa_cmp",
    )(q, kc, vc, overlap)


def _nsa_main_kernel(q_ref, bias_ref, ka_ref, vs_ref, kw_ref, vw_ref, oc_ref, g_ref, o_ref, *, tk, wspan):
    QB = SEL_BLOCK
    R = NSA_HPG * QB
    i = pl.program_id(2)
    q4 = _stack_heads(q_ref[...])
    b4 = jnp.concatenate([bias_ref[0, 0]] * NSA_HPG, axis=0)
    qa = jnp.concatenate([b4, q4], axis=1)
    t4 = i * QB + (lax.broadcasted_iota(jnp.int32, (R, 1), 0) & (QB - 1))

    def step(j, carry, masked):
        m, l, acc = carry
        start = pl.multiple_of(j * tk, tk)
        ka = ka_ref[0, 0, pl.ds(start, tk), :]
        s = _dot_nt(qa, ka)
        if masked:
            kpos = start + lax.broadcasted_iota(jnp.int32, (1, tk), 1)
            s = jnp.where(kpos <= t4, s, NEG)
        m_new = jnp.maximum(m, jnp.max(s, axis=1, keepdims=True))
        alpha = jnp.exp(m - m_new)
        pr = jnp.exp(s - m_new)
        l = alpha * l + jnp.sum(pr, axis=1, keepdims=True)
        acc = alpha * acc + _dot(pr.astype(BF16), vs_ref[0, 0, pl.ds(start, tk), :])
        return m_new, l, acc

    init = (jnp.full((R, 1), NEG, F32), jnp.zeros((R, 1), F32), jnp.zeros((R, HEAD_DIM), F32))
    last = (i * QB) // tk
    carry = lax.fori_loop(0, last, functools.partial(step, masked=False), init)
    m, l, acc = step(last, carry, True)
    o_s = acc / l

    wstart = pl.multiple_of(jnp.maximum(i * QB + QB - wspan, 0), QB)
    kw = kw_ref[0, 0, pl.ds(wstart, wspan), :]
    s = _dot_nt(q4, kw)
    kpos = wstart + lax.broadcasted_iota(jnp.int32, (1, wspan), 1)
    wmask = (kpos <= t4) & (t4 - kpos < WINDOW)
    s = jnp.where(wmask, s, NEG)
    m = jnp.max(s, axis=1, keepdims=True)
    pr = jnp.where(wmask, jnp.exp(s - m), 0.0)
    l = jnp.sum(pr, axis=1, keepdims=True)
    o_w = _dot(pr.astype(BF16), vw_ref[0, 0, pl.ds(wstart, wspan), :]) / l

    gates = _sigmoid(g_ref[0, 0])
    oc = oc_ref[...]
    outs = []
    for h in range(NSA_HPG):
        o_h = (gates[:, 3 * h:3 * h + 1] * oc[:, h * HEAD_DIM:(h + 1) * HEAD_DIM]
               + gates[:, 3 * h + 1:3 * h + 2] * o_s[h * QB:(h + 1) * QB]
               + gates[:, 3 * h + 2:3 * h + 3] * o_w[h * QB:(h + 1) * QB])
        outs.append(o_h)
    o_ref[...] = jnp.concatenate(outs, axis=1).astype(o_ref.dtype)


def _nsa_main(q, bias, kaug, vs, kw, vw, oc, gates, B, S, tk=512):
    T = B * S
    QB = SEL_BLOCK
    nq = S // QB
    tk = min(tk, S)
    wspan = min(WINDOW + 2 * QB, S)
    GW = NSA_HPG * HEAD_DIM
    KA = kaug.shape[-1]
    res = lambda w: pl.BlockSpec((1, 1, S, w), lambda b, g, i: (b, g, 0, 0))
    return pl.pallas_call(
        functools.partial(_nsa_main_kernel, tk=tk, wspan=wspan), grid=(B, NSA_GROUPS, nq),
        in_specs=[pl.BlockSpec((QB, GW), lambda b, g, i: (b * nq + i, g)),
                  pl.BlockSpec((1, 1, QB, LANES), lambda b, g, i: (b, g, i, 0)),
                  res(KA), res(HEAD_DIM), res(HEAD_DIM), res(HEAD_DIM),
                  pl.BlockSpec((QB, GW), lambda b, g, i: (b * nq + i, g)),
                  pl.BlockSpec((1, 1, QB, LANES), lambda b, g, i: (b, g, i, 0))],
        out_specs=pl.BlockSpec((QB, GW), lambda b, g, i: (b * nq + i, g)),
        out_shape=jax.ShapeDtypeStruct((T, NSA_HEADS * HEAD_DIM), BF16),
        compiler_params=_cparams("parallel", "parallel", "arbitrary"), name="nsa_main",
    )(q, bias, kaug, vs, kw, vw, oc, gates)


def _lru_kernel(g_ref, x_ref, cw_ref, cb_ref, wab_ref, bab_ref, lam_ref, o_ref, h_ref, tail_ref, a_ref, b_ref, *, ts):
    n = pl.program_id(0)
    Bb = x_ref.shape[1]
    W = x_ref.shape[2]

    @pl.when(n == 0)
    def _():
        h_ref[...] = jnp.zeros_like(h_ref)
        tail_ref[...] = jnp.zeros_like(tail_ref)

    xin = x_ref[...]
    xp = jnp.concatenate([tail_ref[...], xin], axis=0)
    cw = cw_ref[...]
    x = cb_ref[...].reshape(1, 1, W)
    for j in range(CONV_W):
        x = x + xp[j:j + ts] * cw[j:j + 1].reshape(1, 1, W)
    tail_ref[...] = xin[ts - (CONV_W - 1):ts]
    x2 = x.reshape(ts * Bb, W)
    pre = _dot(x2.astype(BF16), wab_ref[...]) + bab_ref[...]
    r = _sigmoid(pre[:, :W])
    ig = _sigmoid(pre[:, W:])
    log_a = (-RG_C * _softplus(-lam_ref[...])) * r
    a = jnp.exp(log_a)
    th = jnp.tanh(log_a)
    bb = jnp.sqrt(-2.0 * th / (1.0 - th)) * (ig * x2)
    a_ref[...] = a.reshape(ts, Bb, W)
    b_ref[...] = bb.reshape(ts, Bb, W)

    def scan(t, h):
        h = a_ref[t] * h + b_ref[t]
        b_ref[t] = h
        return h

    h_ref[...] = lax.fori_loop(0, ts, scan, h_ref[...], unroll=8)
    o_ref[...] = (b_ref[...] * _gelu_tanh(g_ref[...])).astype(o_ref.dtype)


def _lru(gate_t, x_t, conv_w, conv_b, wab, bab, lam, ts=128):
    S, Bb, W = x_t.shape
    ts = min(ts, S)
    blk = pl.BlockSpec((ts, Bb, W), lambda n: (n, 0, 0))
    full = lambda shape: pl.BlockSpec(shape, lambda n: (0,) * len(shape))
    return pl.pallas_call(
        functools.partial(_lru_kernel, ts=ts), grid=(S // ts,),
        in_specs=[blk, blk, full((CONV_W, W)), full((1, W)), full((W, 2 * W)), full((1, 2 * W)), full((1, W))],
        out_specs=blk,
        out_shape=jax.ShapeDtypeStruct((S, Bb, W), BF16),
        scratch_shapes=[pltpu.VMEM((Bb, W), F32), pltpu.VMEM((CONV_W - 1, Bb, W), F32),
                        pltpu.VMEM((ts, Bb, W), F32), pltpu.VMEM((ts, Bb, W), F32)],
        compiler_params=_cparams("arbitrary"), name="rg_lru",
    )(gate_t, x_t, conv_w, conv_b, wab, bab, lam)


def _even_mixer(h, B, S, norm_w, w_in, fox_bf, conv_w, a_log, dt_bias, gdn_norm_w, w_out):
    T, D = h.shape
    FW = FOX_HEADS * HEAD_DIM
    GW = GDN_HEADS * GDN_DIM
    o_ff = 3 * FW
    o_g = o_ff + FOX_HEADS
    o_gb = o_g + 4 * GW
    small = jnp.concatenate([w_in[:, o_ff:o_g], w_in[:, o_gb:o_gb + 2 * GDN_HEADS],
                             jnp.zeros((D, LANES - FOX_HEADS - 2 * GDN_HEADS), w_in.dtype)], axis=1)
    w = jnp.concatenate([w_in[:, :o_ff], w_in[:, o_g:o_gb], small], axis=1).astype(BF16)
    segs = [(0, 0, 0, FW, False, HEAD_DIM ** -0.5),
            (0, FW, FW, 2 * FW, False, 1.0),
            (1, 0, 3 * FW, 3 * GW, False, 1.0),
            (2, 0, 3 * FW + 3 * GW, GW, False, 1.0),
            (3, 0, 3 * FW + 4 * GW, LANES, False, 1.0)]
    fqkv, gqkv, gz, sm = _norm_proj(h, norm_w, w, segs,
                                    [(3 * FW, BF16), (3 * GW, F32), (GW, F32), (LANES, F32)])
    F = _fox_gate(sm[:, :FOX_HEADS].reshape(B, S, FOX_HEADS), fox_bf)
    fox = _fox_attn(fqkv, F, B, S)
    gdn = _gdn(gqkv, gz, sm, conv_w, a_log, dt_bias, gdn_norm_w, B, S)
    wo = w_out.astype(BF16)
    return _out_proj(h, fox, gdn, wo[:FW], wo[FW:])


def _odd_mixer(h, B, S, rope_tabs, norm_w, w_in, k_pe, k_w1, k_w2, v_pe, v_w1, v_w2,
               conv_w, conv_b, wa, ba, wx, bx, lam, w_out):
    T, D = h.shape
    QW = NSA_HEADS * HEAD_DIM
    KW = NSA_GROUPS * HEAD_DIM
    G = NSA_GROUPS
    o_ng = QW + 6 * KW
    ngw = NSA_HEADS * 3
    w = jnp.concatenate([w_in[:, :o_ng], w_in[:, o_ng:o_ng + ngw], jnp.zeros((D, LANES - ngw), w_in.dtype),
                         w_in[:, o_ng + ngw:]], axis=1).astype(BF16)
    segs = [(0, 0, 0, QW, True, HEAD_DIM ** -0.5),
            (1, 0, QW, KW, True, 1.0),
            (1, KW, QW + KW, KW, False, 1.0),
            (2, 0, QW + 2 * KW, KW, True, 1.0),
            (2, KW, QW + 3 * KW, KW, False, 1.0),
            (2, 2 * KW, QW + 4 * KW, KW, True, 1.0),
            (2, 3 * KW, QW + 5 * KW, KW, False, 1.0),
            (3, 0, o_ng, LANES, False, 1.0),
            (4, 0, o_ng + LANES, 2 * LRU_W, False, 1.0)]
    q, kvc, kv4, ng, rgx = _norm_proj(h, norm_w, w, segs,
                                      [(QW, BF16), (2 * KW, F32), (4 * KW, BF16), (LANES, F32), (2 * LRU_W, F32)],
                                      rope_tabs=rope_tabs)
    nh = S // CMP_STRIDE
    xkv = kvc.reshape(B, S, 2 * G, HEAD_DIM).transpose(0, 2, 1, 3).reshape(B, 2 * G, nh, CMP_STRIDE * HEAD_DIM)
    pe = jnp.stack([k_pe.reshape(-1), v_pe.reshape(-1)]).astype(F32)
    pe = jnp.broadcast_to(pe[:, None, :], (2, SUBLANES, pe.shape[-1]))
    w1 = jnp.stack([k_w1, v_w1]).astype(BF16)
    w2 = jnp.stack([k_w2, v_w2]).astype(BF16)
    cmp = _compress(xkv, pe, w1, w2)
    n_cmp = (S - CMP_BLOCK) // CMP_STRIDE + 1
    ncp = -(-nh // LANES) * LANES
    if ncp != nh:
        cmp = jnp.pad(cmp, ((0, 0), (0, 0), (0, ncp - nh), (0, 0)))
    n_sel = S // SEL_BLOCK
    cs = np.arange(ncp) * CMP_STRIDE
    ss = np.arange(LANES) * SEL_BLOCK
    ov = ((cs[:, None] <= ss[None, :] + SEL_BLOCK - 1) & (cs[:, None] + CMP_BLOCK - 1 >= ss[None, :])
          & (np.arange(ncp)[:, None] < n_cmp) & (np.arange(LANES)[None, :] < n_sel))
    overlap = jnp.asarray(ov.astype(np.float32))
    oc, bias = _nsa_cmp(q, cmp, cmp, overlap, B, S)
    kv = kv4.reshape(B, S, 4, G, HEAD_DIM).transpose(2, 0, 3, 1, 4)
    blk_of_pos = np.arange(S) // SEL_BLOCK
    onehot = jnp.asarray((blk_of_pos[:, None] == np.arange(LANES)[None, :]).astype(np.float32), dtype=BF16)
    kaug = jnp.concatenate([jnp.broadcast_to(onehot, (B, G, S, LANES)), kv[0]], axis=-1)
    gates = ng[:, :ngw].reshape(B, S, G, NSA_HPG * 3).transpose(0, 2, 1, 3)
    gates = jnp.pad(gates, ((0, 0), (0, 0), (0, 0), (0, LANES - NSA_HPG * 3)))
    nsa = _nsa_main(q, bias, kaug, kv[1], kv[2], kv[3], oc, gates, B, S)
    gate_t = rgx[:, :LRU_W].reshape(B, S, LRU_W).transpose(1, 0, 2)
    x_t = rgx[:, LRU_W:].reshape(B, S, LRU_W).transpose(1, 0, 2)
    nblk, bw, _ = wa.shape
    eye = jnp.eye(nblk, dtype=wa.dtype)
    dense = lambda wb: (eye[:, None, :, None] * wb[:, :, None, :]).reshape(nblk * bw, nblk * bw)
    wab = jnp.concatenate([dense(wa), dense(wx)], axis=1).astype(BF16)
    bab = jnp.concatenate([ba, bx]).reshape(1, 2 * LRU_W).astype(F32)
    lru_t = _lru(gate_t, x_t, conv_w.astype(F32), conv_b.reshape(1, LRU_W).astype(F32), wab, bab,
                 lam.reshape(1, LRU_W).astype(F32))
    lru = lru_t.transpose(1, 0, 2).reshape(T, LRU_W)
    wo = w_out.astype(BF16)
    return _out_proj(h, nsa, lru, wo[:QW], wo[QW:])


def kernel(x, p, positions, even_norm_mix, even_w_in, even_fox_bf, even_gdn_conv_w, even_gdn_a_log, even_gdn_dt_bias, even_gdn_norm_w, even_w_out, odd_norm_mix, odd_w_in, odd_cmp_k_pe, odd_cmp_k_w1, odd_cmp_k_w2, odd_cmp_v_pe, odd_cmp_v_w1, odd_cmp_v_w2, odd_rg_conv_w, odd_rg_conv_b, odd_rg_wa, odd_rg_ba, odd_rg_wx, odd_rg_bx, odd_rg_lambda, odd_w_out, mlp_norm, mlp_w_up, mlp_w_down, ple_norm, ple_w_gate, ple_w_proj, final_norm):
    B, S, D = x.shape
    T = B * S
    depth = p.shape[0]
    h = x.reshape(T, D)
    rope_tabs = _rope_tables(positions) if depth > 1 else None
    for i in range(depth):
        j = i // 2
        if i % 2 == 0:
            h = _even_mixer(h, B, S, even_norm_mix[j], even_w_in[j], even_fox_bf[j], even_gdn_conv_w[j],
                            even_gdn_a_log[j], even_gdn_dt_bias[j], even_gdn_norm_w[j], even_w_out[j])
        else:
            h = _odd_mixer(h, B, S, rope_tabs, odd_norm_mix[j], odd_w_in[j], odd_cmp_k_pe[j], odd_cmp_k_w1[j],
                           odd_cmp_k_w2[j], odd_cmp_v_pe[j], odd_cmp_v_w1[j], odd_cmp_v_w2[j], odd_rg_conv_w[j],
                           odd_rg_conv_b[j], odd_rg_wa[j], odd_rg_ba[j], odd_rg_wx[j], odd_rg_bx[j],
                           odd_rg_lambda[j], odd_w_out[j])
        h = _mlp(h, mlp_norm[i], mlp_w_up[i].astype(BF16), mlp_w_down[i].astype(BF16))
        h = _ple(h, p[i].reshape(T, -1), ple_norm[i], ple_w_gate[i].astype(BF16), ple_w_proj[i].astype(BF16),
                 final_norm, final=(i == depth - 1))
    return h.reshape(B, S, D)
```

```python
import functools

import numpy as np
import jax
import jax.numpy as jnp
from jax import lax
from jax.experimental import pallas as pl
from jax.experimental.pallas import tpu as pltpu

F32 = jnp.float32
BF16 = jnp.bfloat16
HIGHEST = lax.Precision.HIGHEST

NORM_EPS = 1e-6
LOG2E = float(np.log2(np.e))
NEG = -1e30
REMOVED = -3e38
LANES = 128
SUBLANES = 8
VMEM_LIMIT = 56 * 1024 * 1024

HEAD_DIM = 64
ROT_DIM = 16
ROPE_THETA = 500000.0
FOX_HEADS = 8
GDN_HEADS = 4
GDN_DIM = 128
GDN_CHUNK = 64
NSA_HEADS = 8
NSA_GROUPS = 2
NSA_HPG = NSA_HEADS // NSA_GROUPS
CMP_BLOCK = 32
CMP_STRIDE = 16
CMP_HIDDEN = 128
SEL_BLOCK = 64
SEL_TOPK = 16
WINDOW = 512
LRU_W = 512
RG_C = 8.0
CONV_W = 4


def _cparams(*sem):
    return pltpu.CompilerParams(dimension_semantics=sem, vmem_limit_bytes=VMEM_LIMIT)


def _dot(a, b, **kw):
    return jnp.dot(a, b, preferred_element_type=F32, **kw)


def _dot_nt(a, b, **kw):
    return lax.dot_general(a, b, (((1,), (1,)), ((), ())), preferred_element_type=F32, **kw)


def _dot_tn(a, b, **kw):
    return lax.dot_general(a, b, (((0,), (0,)), ((), ())), preferred_element_type=F32, **kw)


def _softplus(x):
    return jnp.maximum(x, 0.0) + jnp.log1p(jnp.exp(-jnp.abs(x)))


def _sigmoid(x):
    return 1.0 / (1.0 + jnp.exp(-x))


def _silu(x):
    return x * _sigmoid(x)


def _gelu_tanh(x):
    return 0.5 * x * (1.0 + jnp.tanh(np.float32(np.sqrt(2.0 / np.pi)) * (x + 0.044715 * (x * x * x))))


def _rms(x, w):
    ms = jnp.mean(x * x, axis=-1, keepdims=True)
    return x * lax.rsqrt(ms + NORM_EPS) * w


def _apply_rope(y, c, s1, s2):
    outs = []
    for g in range(y.shape[1] // LANES):
        yg = y[:, g * LANES:(g + 1) * LANES]
        outs.append(yg * c + pltpu.roll(yg, LANES - ROT_DIM // 2, 1) * s1 + pltpu.roll(yg, ROT_DIM // 2, 1) * s2)
    return outs[0] if len(outs) == 1 else jnp.concatenate(outs, axis=1)


def _norm_proj_kernel(*refs, segs, n_out, rope):
    if rope:
        x_ref, nw_ref, w_ref, c_ref, s1_ref, s2_ref = refs[:6]
        out_refs = refs[6:6 + n_out]
    else:
        x_ref, nw_ref, w_ref = refs[:3]
        out_refs = refs[3:3 + n_out]
    xn = _rms(x_ref[...], nw_ref[...]).astype(BF16)
    for (oi, oc, wc, width, do_rope, scale) in segs:
        y = _dot(xn, w_ref[:, wc:wc + width])
        if do_rope:
            y = _apply_rope(y, c_ref[...], s1_ref[...], s2_ref[...])
        if scale != 1.0:
            y = y * scale
        out_refs[oi][:, oc:oc + width] = y.astype(out_refs[oi].dtype)


def _norm_proj(x, nw, w, segs, out_defs, rope_tabs=None, tm=512):
    T, D = x.shape
    N = w.shape[1]
    tm = min(tm, T)
    rope = rope_tabs is not None
    in_specs = [pl.BlockSpec((tm, D), lambda i: (i, 0)),
                pl.BlockSpec((1, D), lambda i: (0, 0)),
                pl.BlockSpec((D, N), lambda i: (0, 0))]
    args = [x, nw.reshape(1, D), w]
    if rope:
        in_specs += [pl.BlockSpec((tm, LANES), lambda i: (i, 0))] * 3
        args += list(rope_tabs)
    out_shape = [jax.ShapeDtypeStruct((T, wd), dt) for wd, dt in out_defs]
    out_specs = [pl.BlockSpec((tm, wd), lambda i: (i, 0)) for wd, _ in out_defs]
    return pl.pallas_call(
        functools.partial(_norm_proj_kernel, segs=tuple(segs), n_out=len(out_defs), rope=rope),
        grid=(T // tm,), in_specs=in_specs, out_specs=out_specs, out_shape=out_shape,
        compiler_params=_cparams("parallel"), name="norm_proj",
    )(*args)


def _rope_table_kernel(pos_ref, f_ref, cos_ref, sin_ref, nsin_ref):
    ang = pos_ref[0] * f_ref[...]
    c = jnp.cos(ang)
    s = jnp.sin(ang)
    cos_ref[0] = c
    sin_ref[0] = s
    nsin_ref[0] = -s


def _rope_tables(positions):
    B, S = positions.shape
    half = ROT_DIM // 2
    inv_freq = ROPE_THETA ** (-jnp.arange(half, dtype=F32) * (2.0 / ROT_DIM))
    pos = positions.astype(F32).reshape(B, 1, S)
    sh = jax.ShapeDtypeStruct((B, half, S), F32)
    spec = pl.BlockSpec((1, half, S), lambda b: (b, 0, 0))
    cos, sin, nsin = pl.pallas_call(
        _rope_table_kernel, grid=(B,),
        in_specs=[pl.BlockSpec((1, 1, S), lambda b: (b, 0, 0)), pl.BlockSpec((half, 1), lambda b: (0, 0))],
        out_specs=[spec, spec, spec], out_shape=[sh, sh, sh],
        compiler_params=_cparams("parallel"), name="rope_tables",
    )(pos, inv_freq.reshape(half, 1))
    tr = lambda t: t.transpose(0, 2, 1).reshape(B * S, half)
    cos, sin, nsin = tr(cos), tr(sin), tr(nsin)
    T = B * S
    ones = jnp.ones((T, HEAD_DIM - ROT_DIM), F32)
    zeros = jnp.zeros((T, HEAD_DIM - ROT_DIM), F32)
    z8 = jnp.zeros((T, half), F32)
    c = jnp.tile(jnp.concatenate([cos, cos, ones], axis=1), (1, LANES // HEAD_DIM))
    s1 = jnp.tile(jnp.concatenate([nsin, z8, zeros], axis=1), (1, LANES // HEAD_DIM))
    s2 = jnp.tile(jnp.concatenate([z8, sin, zeros], axis=1), (1, LANES // HEAD_DIM))
    return c, s1, s2


def _out_proj_kernel(h_ref, a1_ref, a2_ref, w1_ref, w2_ref, o_ref):
    y = _dot(a1_ref[...].astype(BF16), w1_ref[...]) + _dot(a2_ref[...].astype(BF16), w2_ref[...])
    o_ref[...] = h_ref[...] + y


def _out_proj(h, a1, a2, w1, w2, tm=512):
    T, D = h.shape
    tm = min(tm, T)
    K1, K2 = a1.shape[1], a2.shape[1]
    return pl.pallas_call(
        _out_proj_kernel, grid=(T // tm,),
        in_specs=[pl.BlockSpec((tm, D), lambda i: (i, 0)),
                  pl.BlockSpec((tm, K1), lambda i: (i, 0)),
                  pl.BlockSpec((tm, K2), lambda i: (i, 0)),
                  pl.BlockSpec((K1, D), lambda i: (0, 0)),
                  pl.BlockSpec((K2, D), lambda i: (0, 0))],
        out_specs=pl.BlockSpec((tm, D), lambda i: (i, 0)),
        out_shape=jax.ShapeDtypeStruct((T, D), F32),
        compiler_params=_cparams("parallel"), name="out_proj",
    )(h, a1, a2, w1, w2)


def _mlp_kernel(h_ref, nw_ref, wu_ref, wd_ref, o_ref, xn_ref, acc_ref):
    f = pl.program_id(1)

    @pl.when(f == 0)
    def _():
        xn_ref[...] = _rms(h_ref[...], nw_ref[...]).astype(BF16)
        acc_ref[...] = h_ref[...]

    u = jnp.maximum(_dot(xn_ref[...], wu_ref[...]), 0.0)
    acc_ref[...] += _dot((u * u).astype(BF16), wd_ref[...])

    @pl.when(f == pl.num_programs(1) - 1)
    def _():
        o_ref[...] = acc_ref[...]


def _mlp(h, nw, wu, wd, tm=512, tf=1024):
    T, D = h.shape
    FF = wu.shape[1]
    tm = min(tm, T)
    return pl.pallas_call(
        _mlp_kernel, grid=(T // tm, FF // tf),
        in_specs=[pl.BlockSpec((tm, D), lambda i, f: (i, 0)),
                  pl.BlockSpec((1, D), lambda i, f: (0, 0)),
                  pl.BlockSpec((D, tf), lambda i, f: (0, f)),
                  pl.BlockSpec((tf, D), lambda i, f: (f, 0))],
        out_specs=pl.BlockSpec((tm, D), lambda i, f: (i, 0)),
        out_shape=jax.ShapeDtypeStruct((T, D), F32),
        scratch_shapes=[pltpu.VMEM((tm, D), BF16), pltpu.VMEM((tm, D), F32)],
        compiler_params=_cparams("parallel", "arbitrary"), name="mlp",
    )(h, nw.reshape(1, D), wu, wd)


def _ple_kernel(h_ref, p_ref, nw_ref, wg_ref, wp_ref, fw_ref, o_ref, *, final):
    h = h_ref[...]
    xn = _rms(h, nw_ref[...]).astype(BF16)
    gate = _sigmoid(_dot(xn, wg_ref[...]))
    y = h + gate * _dot(p_ref[...].astype(BF16), wp_ref[...])
    if final:
        y = _rms(y, fw_ref[...])
    o_ref[...] = y


def _ple(h, p, nw, wg, wp, fw, final, tm=512):
    T, D = h.shape
    P = p.shape[1]
    tm = min(tm, T)
    return pl.pallas_call(
        functools.partial(_ple_kernel, final=final), grid=(T // tm,),
        in_specs=[pl.BlockSpec((tm, D), lambda i: (i, 0)),
                  pl.BlockSpec((tm, P), lambda i: (i, 0)),
                  pl.BlockSpec((1, D), lambda i: (0, 0)),
                  pl.BlockSpec((D, D), lambda i: (0, 0)),
                  pl.BlockSpec((P, D), lambda i: (0, 0)),
                  pl.BlockSpec((1, D), lambda i: (0, 0))],
        out_specs=pl.BlockSpec((tm, D), lambda i: (i, 0)),
        out_shape=jax.ShapeDtypeStruct((T, D), F32),
        compiler_params=_cparams("parallel"), name="ple",
    )(h, p, nw.reshape(1, D), wg, wp, fw.reshape(1, D))


def _fox_gate_kernel(x_ref, b_ref, o_ref):
    x = x_ref[0] + b_ref[0]
    logf = jnp.minimum(x, 0.0) - jnp.log1p(jnp.exp(-jnp.abs(x)))
    nr = x.shape[0]
    r = lax.broadcasted_iota(jnp.int32, (LANES, LANES), 0)
    c = lax.broadcasted_iota(jnp.int32, (LANES, LANES), 1)
    upper = (r <= c).astype(F32)
    within = _dot(logf, upper, precision=HIGHEST)
    tot = jnp.broadcast_to(within[:, LANES - 1:LANES], (nr, LANES))
    rr = lax.broadcasted_iota(jnp.int32, (nr, nr), 0)
    cc = lax.broadcasted_iota(jnp.int32, (nr, nr), 1)
    strict = (cc < rr).astype(F32)
    o_ref[0] = (within + _dot(strict, tot, precision=HIGHEST)) * LOG2E


def _fox_gate(f_logit, bias):
    B, S, H = f_logit.shape
    nr = S // LANES
    x = f_logit.transpose(0, 2, 1).reshape(B * H, nr, LANES)
    b = jnp.broadcast_to(jnp.tile(bias.astype(F32), B)[:, None, None], (B * H, 1, LANES))
    out = pl.pallas_call(
        _fox_gate_kernel, grid=(B * H,),
        in_specs=[pl.BlockSpec((1, nr, LANES), lambda i: (i, 0, 0)),
                  pl.BlockSpec((1, 1, LANES), lambda i: (i, 0, 0))],
        out_specs=pl.BlockSpec((1, nr, LANES), lambda i: (i, 0, 0)),
        out_shape=jax.ShapeDtypeStruct((B * H, nr, LANES), F32),
        compiler_params=_cparams("parallel"), name="fox_gate",
    )(x, b)
    return out.reshape(B, H, S)


def _fox_attn_kernel(q_ref, k_ref, v_ref, f_ref, o_ref, *, tq, tk, hp):
    p = pl.program_id(1)
    i = pl.program_id(2)
    qs = [q_ref[:, hh * HEAD_DIM:(hh + 1) * HEAD_DIM] for hh in range(hp)]
    nsub = tq // tk

    def step(j, carry, diag):
        start = pl.multiple_of(j * tk, tk)
        ss = [_dot_nt(qs[hh], k_ref[pl.ds(start, tk), hh * HEAD_DIM:(hh + 1) * HEAD_DIM]) for hh in range(hp)]
        out = []
        for hh in range(hp):
            m, l, acc = carry[hh]
            fk = f_ref[0, hp * p + hh, pl.ds(j, 1), :]
            s = ss[hh] - fk
            if diag is not None:
                r = lax.broadcasted_iota(jnp.int32, (tq, tk), 0)
                c = lax.broadcasted_iota(jnp.int32, (tq, tk), 1) + diag * tk
                s = jnp.where(c <= r, s, NEG)
            m_new = jnp.maximum(m, jnp.max(s, axis=1, keepdims=True))
            alpha = jnp.exp2(m - m_new)
            pr = jnp.exp2(s - m_new)
            l = alpha * l + jnp.sum(pr, axis=1, keepdims=True)
            v = v_ref[pl.ds(start, tk), hh * HEAD_DIM:(hh + 1) * HEAD_DIM]
            acc = alpha * acc + _dot(pr.astype(BF16), v)
            out.append((m_new, l, acc))
        return tuple(out)

    init = tuple((jnp.full((tq, 1), NEG, F32), jnp.zeros((tq, 1), F32), jnp.zeros((tq, HEAD_DIM), F32))
                 for _ in range(hp))
    carry = lax.fori_loop(0, i * nsub, functools.partial(step, diag=None), init)
    for d in range(nsub):
        carry = step(i * nsub + d, carry, d)
    o_ref[...] = jnp.concatenate([acc / l for (_, l, acc) in carry], axis=1).astype(o_ref.dtype)


def _fox_attn(qkv, F, B, S, tq=1024, tk=1024, hp=2):
    T = B * S
    tq = min(tq, S)
    tk = min(tk, tq)
    nq = S // tq
    W = FOX_HEADS * HEAD_DIM
    bw = hp * HEAD_DIM
    ng = W // bw
    Fr = F.reshape(B, FOX_HEADS, S // tk, tk)
    return pl.pallas_call(
        functools.partial(_fox_attn_kernel, tq=tq, tk=tk, hp=hp), grid=(B, ng, nq),
        in_specs=[pl.BlockSpec((tq, bw), lambda b, p, i: (b * nq + i, p)),
                  pl.BlockSpec((S, bw), lambda b, p, i: (b, ng + p)),
                  pl.BlockSpec((S, bw), lambda b, p, i: (b, 2 * ng + p)),
                  pl.BlockSpec((1, FOX_HEADS, S // tk, tk), lambda b, p, i: (b, 0, 0, 0))],
        out_specs=pl.BlockSpec((tq, bw), lambda b, p, i: (b * nq + i, p)),
        out_shape=jax.ShapeDtypeStruct((T, W), BF16),
        compiler_params=_cparams("parallel", "parallel", "arbitrary"), name="fox_attn",
    )(qkv, qkv, qkv, Fr)


def _cumsum_rows(x):
    n = x.shape[0]
    row = lax.broadcasted_iota(jnp.int32, x.shape, 0)
    d = 1
    while d < n:
        x = x + jnp.where(row >= d, pltpu.roll(x, d, 0), 0.0)
        d *= 2
    return x


def _gdn_kernel(x_ref, z_ref, sm_ref, cw_ref, par_ref, nw_ref, o_ref, state_ref, xp_ref, *, cps):
    C = GDN_CHUNK
    D = GDN_DIM
    W = GDN_HEADS * D
    R = cps * C
    n = pl.program_id(1)

    @pl.when(n == 0)
    def _():
        state_ref[...] = jnp.zeros_like(state_ref)
        xp_ref[0:SUBLANES, :] = jnp.zeros((SUBLANES, 3 * W), F32)

    x = x_ref[...]
    xp_ref[SUBLANES:SUBLANES + R, :] = x
    cw = cw_ref[...]
    y = x * cw[CONV_W - 1:CONV_W]
    for j in range(CONV_W - 1):
        off = SUBLANES - (CONV_W - 1) + j
        y = y + xp_ref[off:off + R, :] * cw[j:j + 1]
    xp_ref[0:SUBLANES, :] = x[R - SUBLANES:R]
    y = _silu(y)

    sm = sm_ref[...]
    beta_all = _sigmoid(sm)
    g_all = -jnp.exp(par_ref[0:1, :]) * _softplus(sm + par_ref[1:2, :])
    ri = lax.broadcasted_iota(jnp.int32, (C, C), 0)
    ci = lax.broadcasted_iota(jnp.int32, (C, C), 1)
    incl = ri >= ci
    strict = ri > ci

    chains = [(c, h) for c in range(cps) for h in range(GDN_HEADS)]
    gcs = [_cumsum_rows(g_all[c * C:(c + 1) * C]) for c in range(cps)]
    gcts = [gc.T for gc in gcs]
    pre = {}
    for (c, h) in chains:
        r0 = c * C
        q = y[r0:r0 + C, h * D:(h + 1) * D]
        k = y[r0:r0 + C, W + h * D:W + (h + 1) * D]
        v = y[r0:r0 + C, 2 * W + h * D:2 * W + (h + 1) * D]
        qn = q * lax.rsqrt(jnp.sum(q * q, axis=-1, keepdims=True) + 1e-6) * (D ** -0.5)
        kn = k * lax.rsqrt(jnp.sum(k * k, axis=-1, keepdims=True) + 1e-6)
        beta = beta_all[r0:r0 + C, 8 + h:9 + h]
        gcol = gcs[c][:, 12 + h:13 + h]
        grow = gcts[c][12 + h:13 + h, :]
        decay = jnp.where(incl, jnp.exp(jnp.where(incl, gcol - grow, 0.0)), 0.0)
        kb = kn * beta
        g_last = gcol[C - 1:C, :]
        pre[c, h] = dict(qn16=qn.astype(BF16), kn16=kn.astype(BF16), kb16=kb.astype(BF16), decay=decay,
                         rhs=jnp.concatenate([v * beta, kb * jnp.exp(gcol)], axis=1),
                         qg16=(qn * jnp.exp(gcol)).astype(BF16),
                         kg16=(kn * jnp.exp(g_last - gcol)).astype(BF16), e_last=jnp.exp(g_last))
    kk = {ch: _dot_nt(pre[ch]["kb16"], pre[ch]["kn16"]) for ch in chains}
    qk = {ch: _dot_nt(pre[ch]["qn16"], pre[ch]["kn16"]) for ch in chains}
    P = {ch: -jnp.where(strict, kk[ch] * pre[ch]["decay"], 0.0) for ch in chains}
    X = dict(P)
    for _ in range(5):
        P = {ch: _dot(P[ch].astype(BF16), P[ch].astype(BF16)) for ch in chains}
        XP = {ch: _dot(X[ch].astype(BF16), P[ch].astype(BF16)) for ch in chains}
        X = {ch: X[ch] + P[ch] + XP[ch] for ch in chains}
    sol = {ch: pre[ch]["rhs"] + _dot(X[ch].astype(BF16), pre[ch]["rhs"].astype(BF16)) for ch in chains}
    attn16 = {ch: jnp.where(incl, qk[ch] * pre[ch]["decay"], 0.0).astype(BF16) for ch in chains}

    heads = range(GDN_HEADS)
    states = [state_ref[h] for h in heads]
    for c in range(cps):
        r0 = c * C
        st16 = [states[h].astype(BF16) for h in heads]
        ws = [_dot(sol[c, h][:, D:].astype(BF16), st16[h]) for h in heads]
        qs = [_dot(pre[c, h]["qg16"], st16[h]) for h in heads]
        v_new = [(sol[c, h][:, :D] - ws[h]).astype(BF16) for h in heads]
        av = [_dot(attn16[c, h], v_new[h]) for h in heads]
        kv = [_dot_tn(pre[c, h]["kg16"], v_new[h]) for h in heads]
        outs = []
        for h in heads:
            states[h] = states[h] * pre[c, h]["e_last"] + kv[h]
            o = _rms(qs[h] + av[h], nw_ref[...]) * _silu(z_ref[r0:r0 + C, h * D:(h + 1) * D])
            outs.append(o)
        o_ref[r0:r0 + C, :] = jnp.concatenate(outs, axis=1).astype(o_ref.dtype)
    for h in heads:
        state_ref[h] = states[h]


def _gdn(gqkv, gz, small, conv_w, a_log, dt_bias, norm_w, B, S, cps=2):
    T = B * S
    C = GDN_CHUNK
    R = cps * C
    N = S // R
    W = GDN_HEADS * GDN_DIM
    par = jnp.zeros((SUBLANES, LANES), F32)
    par = par.at[0, 12:16].set(a_log.astype(F32)).at[1, 12:16].set(dt_bias.astype(F32))
    return pl.pallas_call(
        functools.partial(_gdn_kernel, cps=cps), grid=(B, N),
        in_specs=[pl.BlockSpec((R, 3 * W), lambda b, n: (b * N + n, 0)),
                  pl.BlockSpec((R, W), lambda b, n: (b * N + n, 0)),
                  pl.BlockSpec((R, LANES), lambda b, n: (b * N + n, 0)),
                  pl.BlockSpec((CONV_W, 3 * W), lambda b, n: (0, 0)),
                  pl.BlockSpec((SUBLANES, LANES), lambda b, n: (0, 0)),
                  pl.BlockSpec((1, GDN_DIM), lambda b, n: (0, 0))],
        out_specs=pl.BlockSpec((R, W), lambda b, n: (b * N + n, 0)),
        out_shape=jax.ShapeDtypeStruct((T, W), BF16),
        scratch_shapes=[pltpu.VMEM((GDN_HEADS, GDN_DIM, GDN_DIM), F32),
                        pltpu.VMEM((SUBLANES + R, 3 * W), F32)],
        compiler_params=_cparams("parallel", "arbitrary"), name="gdn",
    )(gqkv, gz, small, conv_w.astype(F32), par, norm_w.reshape(1, GDN_DIM).astype(F32))


def _compress_kernel(x_ref, pe_ref, w1_ref, w2_ref, o_ref):
    half = w1_ref.shape[1] // 2
    x = x_ref[0, 0].astype(BF16)
    w1 = w1_ref[0]
    a = _dot(x, w1[:half])
    b = _dot(x, w1[half:])
    c = _dot(pe_ref[0].astype(BF16), w1)
    nh = x.shape[0]
    b_next = pltpu.roll(b, nh - 1, 0)
    hid = _gelu_tanh(a + b_next + c[0:1])
    o_ref[0, 0] = _dot(hid.astype(BF16), w2_ref[0]).astype(o_ref.dtype)


def _compress(xkv, pe, w1, w2):
    B, four, nh, fw = xkv.shape
    return pl.pallas_call(
        _compress_kernel, grid=(B, four),
        in_specs=[pl.BlockSpec((1, 1, nh, fw), lambda b, j: (b, j, 0, 0)),
                  pl.BlockSpec((1, SUBLANES, 2 * fw), lambda b, j: (j // 2, 0, 0)),
                  pl.BlockSpec((1, 2 * fw, CMP_HIDDEN), lambda b, j: (j // 2, 0, 0)),
                  pl.BlockSpec((1, CMP_HIDDEN, HEAD_DIM), lambda b, j: (j // 2, 0, 0))],
        out_specs=pl.BlockSpec((1, 1, nh, HEAD_DIM), lambda b, j: (b, j, 0, 0)),
        out_shape=jax.ShapeDtypeStruct((B, four, nh, HEAD_DIM), BF16),
        compiler_params=_cparams("parallel", "parallel"), name="nsa_compress",
    )(xkv, pe, w1, w2)


def _stack_heads(qb):
    return jnp.concatenate([qb[:, h * HEAD_DIM:(h + 1) * HEAD_DIM] for h in range(NSA_HPG)], axis=0)


def _unstack_heads(o, tq):
    return jnp.concatenate([o[h * tq:(h + 1) * tq] for h in range(NSA_HPG)], axis=1)


def _nsa_cmp_kernel(q_ref, kc_ref, vc_ref, ov_ref, oc_ref, bias_ref, *, tq, n_sel, top_k):
    i = pl.program_id(2)
    q4 = _stack_heads(q_ref[...])
    ncp = kc_ref.shape[2]
    s = _dot_nt(q4, kc_ref[0, 0])
    row = lax.broadcasted_iota(jnp.int32, (NSA_HPG * tq, 1), 0)
    t4 = i * tq + (row & (tq - 1))
    cmp_end = lax.broadcasted_iota(jnp.int32, (1, ncp), 1) * CMP_STRIDE + (CMP_BLOCK - 1)
    mask = cmp_end <= t4
    s = jnp.where(mask, s, NEG)
    m = jnp.max(s, axis=1, keepdims=True)
    p = jnp.where(mask, jnp.exp2(s - m), 0.0)
    l = jnp.sum(p, axis=1, keepdims=True)
    p = p * jnp.where(l > 0.0, 1.0 / l, 0.0)
    oc = _dot(p.astype(BF16), vc_ref[0, 0])
    oc_ref[...] = _unstack_heads(oc, tq)
    psum = p[0:tq] + p[tq:2 * tq] + p[2 * tq:3 * tq] + p[3 * tq:4 * tq]
    imp = _dot(psum, ov_ref[...], precision=HIGHEST)
    blk = lax.broadcasted_iota(jnp.int32, (1, LANES), 1)
    t = i * tq + lax.broadcasted_iota(jnp.int32, (tq, 1), 0)
    cur = t >> (SEL_BLOCK.bit_length() - 1)
    forced = (blk == 0) | (blk == cur) | (blk == cur - 1)
    future = blk * SEL_BLOCK > t
    imp = jnp.where(future, NEG, jnp.where(forced, -NEG, imp))
    imp = jnp.where(blk < n_sel, imp, REMOVED)
    imp_t = imp.T
    blk_t = lax.broadcasted_iota(jnp.int32, (LANES, tq), 0).astype(F32)
    sel_t = jnp.zeros((LANES, tq), F32)
    for _ in range(top_k):
        mx = jnp.max(imp_t, axis=0, keepdims=True)
        first = jnp.min(jnp.where(imp_t == mx, blk_t, float(LANES)), axis=0, keepdims=True)
        pick = blk_t == first
        sel_t = jnp.where(pick, 1.0, sel_t)
        imp_t = jnp.where(pick, REMOVED, imp_t)
    sel = sel_t.T
    bias_ref[0, 0] = jnp.where((sel > 0.0) & jnp.logical_not(future), 0.0, NEG).astype(bias_ref.dtype)


def _nsa_cmp(q, kc, vc, overlap, B, S, tq=256):
    T = B * S
    tq = min(tq, S)
    nq = S // tq
    ncp = kc.shape[2]
    n_sel = S // SEL_BLOCK
    GW = NSA_HPG * HEAD_DIM
    return pl.pallas_call(
        functools.partial(_nsa_cmp_kernel, tq=tq, n_sel=n_sel, top_k=min(SEL_TOPK, n_sel)),
        grid=(B, NSA_GROUPS, nq),
        in_specs=[pl.BlockSpec((tq, GW), lambda b, g, i: (b * nq + i, g)),
                  pl.BlockSpec((1, 1, ncp, HEAD_DIM), lambda b, g, i: (b, g, 0, 0)),
                  pl.BlockSpec((1, 1, ncp, HEAD_DIM), lambda b, g, i: (b, NSA_GROUPS + g, 0, 0)),
                  pl.BlockSpec((ncp, LANES), lambda b, g, i: (0, 0))],
        out_specs=[pl.BlockSpec((tq, GW), lambda b, g, i: (b * nq + i, g)),
                   pl.BlockSpec((1, 1, tq, LANES), lambda b, g, i: (b, g, i, 0))],
        out_shape=[jax.ShapeDtypeStruct((T, NSA_HEADS * HEAD_DIM), F32),
                   jax.ShapeDtypeStruct((B, NSA_GROUPS, S, LANES), BF16)],
        compiler_params=_cparams("parallel", "parallel", "parallel"), name="nsa_cmp",
    )(q, kc, vc, overlap)


def _nsa_main_kernel(q_ref, bias_ref, ka_ref, vs_ref, kw_ref, vw_ref, oc_ref, g_ref, o_ref, *, qn, tk, wspan):
    R = NSA_HPG * qn
    tb = 2 * tk
    i = pl.program_id(2)
    q4 = _stack_heads(q_ref[...])
    b4 = jnp.concatenate([bias_ref[0, 0]] * NSA_HPG, axis=0)
    qa = jnp.concatenate([b4, q4], axis=1)
    t4 = i * qn + (lax.broadcasted_iota(jnp.int32, (R, 1), 0) & (qn - 1))

    def step(start, size, carry, masked):
        m, l, acc = carry
        ka = ka_ref[0, 0, pl.ds(start, size), :]
        s = _dot_nt(qa, ka)
        if masked:
            kpos = start + lax.broadcasted_iota(jnp.int32, (1, size), 1)
            s = jnp.where(kpos <= t4, s, NEG)
        m_new = jnp.maximum(m, jnp.max(s, axis=1, keepdims=True))
        alpha = jnp.exp2(m - m_new)
        pr = jnp.exp2(s - m_new)
        l = alpha * l + jnp.sum(pr, axis=1, keepdims=True)
        acc = alpha * acc + _dot(pr.astype(BF16), vs_ref[0, 0, pl.ds(start, size), :])
        return m_new, l, acc

    first = i * qn
    nbig = first // tb
    diag = (first // tk) * tk
    nsmall = (diag - nbig * tb) // tk
    init = (jnp.full((R, 1), NEG, F32), jnp.zeros((R, 1), F32), jnp.zeros((R, HEAD_DIM), F32))
    carry = lax.fori_loop(0, nbig, lambda j, c: step(pl.multiple_of(j * tb, tb), tb, c, False), init)
    carry = lax.fori_loop(0, nsmall, lambda j, c: step(pl.multiple_of(nbig * tb, tk), tk, c, False), carry)
    m, l, acc = step(pl.multiple_of(diag, tk), tk, carry, True)
    o_s = acc / l

    wstart = pl.multiple_of(jnp.maximum(first + qn - wspan, 0), qn)
    kw = kw_ref[0, 0, pl.ds(wstart, wspan), :]
    s = _dot_nt(q4, kw)
    kpos = wstart + lax.broadcasted_iota(jnp.int32, (1, wspan), 1)
    wmask = lax.bitcast_convert_type(t4 - kpos, jnp.uint32) < jnp.uint32(WINDOW)
    s = jnp.where(wmask, s, NEG)
    m = jnp.max(s, axis=1, keepdims=True)
    pr = jnp.exp2(s - m)
    l = jnp.sum(pr, axis=1, keepdims=True)
    o_w = _dot(pr.astype(BF16), vw_ref[0, 0, pl.ds(wstart, wspan), :]) / l

    gates = _sigmoid(g_ref[0, 0])
    oc = oc_ref[...]
    outs = []
    for h in range(NSA_HPG):
        o_h = (gates[:, 3 * h:3 * h + 1] * oc[:, h * HEAD_DIM:(h + 1) * HEAD_DIM]
               + gates[:, 3 * h + 1:3 * h + 2] * o_s[h * qn:(h + 1) * qn]
               + gates[:, 3 * h + 2:3 * h + 3] * o_w[h * qn:(h + 1) * qn])
        outs.append(o_h)
    o_ref[...] = jnp.concatenate(outs, axis=1).astype(o_ref.dtype)


def _nsa_main(q, bias, kaug, vs, kw, vw, oc, gates, B, S, qn=128, tk=512):
    T = B * S
    nq = S // qn
    tk = min(tk, S // 2)
    wspan = min(WINDOW + qn, S)
    GW = NSA_HPG * HEAD_DIM
    KA = kaug.shape[-1]
    res = lambda w: pl.BlockSpec((1, 1, S, w), lambda b, g, i: (b, g, 0, 0))
    return pl.pallas_call(
        functools.partial(_nsa_main_kernel, qn=qn, tk=tk, wspan=wspan), grid=(B, NSA_GROUPS, nq),
        in_specs=[pl.BlockSpec((qn, GW), lambda b, g, i: (b * nq + i, g)),
                  pl.BlockSpec((1, 1, qn, LANES), lambda b, g, i: (b, g, i, 0)),
                  res(KA), res(HEAD_DIM), res(HEAD_DIM), res(HEAD_DIM),
                  pl.BlockSpec((qn, GW), lambda b, g, i: (b * nq + i, g)),
                  pl.BlockSpec((1, 1, qn, LANES), lambda b, g, i: (b, g, i, 0))],
        out_specs=pl.BlockSpec((qn, GW), lambda b, g, i: (b * nq + i, g)),
        out_shape=jax.ShapeDtypeStruct((T, NSA_HEADS * HEAD_DIM), BF16),
        compiler_params=_cparams("parallel", "parallel", "arbitrary"), name="nsa_main",
    )(q, bias, kaug, vs, kw, vw, oc, gates)


def _lru_kernel(g_ref, x_ref, cw_ref, cb_ref, wab_ref, bab_ref, lam_ref, o_ref, h_ref, tail_ref, a_ref, b_ref, *, ts):
    n = pl.program_id(0)
    Bb = x_ref.shape[1]
    W = x_ref.shape[2]

    @pl.when(n == 0)
    def _():
        h_ref[...] = jnp.zeros_like(h_ref)
        tail_ref[...] = jnp.zeros_like(tail_ref)

    xin = x_ref[...]
    xp = jnp.concatenate([tail_ref[...], xin], axis=0)
    cw = cw_ref[...]
    x = cb_ref[...].reshape(1, 1, W)
    for j in range(CONV_W):
        x = x + xp[j:j + ts] * cw[j:j + 1].reshape(1, 1, W)
    tail_ref[...] = xin[ts - (CONV_W - 1):ts]
    x2 = x.reshape(ts * Bb, W)
    pre = _dot(x2.astype(BF16), wab_ref[...]) + bab_ref[...]
    r = _sigmoid(pre[:, :W])
    ig = _sigmoid(pre[:, W:])
    log_a = (-RG_C * _softplus(-lam_ref[...])) * r
    a = jnp.exp(log_a)
    th = jnp.tanh(log_a)
    bb = jnp.sqrt(-2.0 * th / (1.0 - th)) * (ig * x2)
    a_ref[...] = a.reshape(ts, Bb, W)
    b_ref[...] = bb.reshape(ts, Bb, W)

    def scan(t, h):
        h = a_ref[t] * h + b_ref[t]
        b_ref[t] = h
        return h

    h_ref[...] = lax.fori_loop(0, ts, scan, h_ref[...], unroll=8)
    o_ref[...] = (b_ref[...] * _gelu_tanh(g_ref[...])).astype(o_ref.dtype)


def _lru(gate_t, x_t, conv_w, conv_b, wab, bab, lam, ts=128):
    S, Bb, W = x_t.shape
    ts = min(ts, S)
    blk = pl.BlockSpec((ts, Bb, W), lambda n: (n, 0, 0))
    full = lambda shape: pl.BlockSpec(shape, lambda n: (0,) * len(shape))
    return pl.pallas_call(
        functools.partial(_lru_kernel, ts=ts), grid=(S // ts,),
        in_specs=[blk, blk, full((CONV_W, W)), full((1, W)), full((W, 2 * W)), full((1, 2 * W)), full((1, W))],
        out_specs=blk,
        out_shape=jax.ShapeDtypeStruct((S, Bb, W), BF16),
        scratch_shapes=[pltpu.VMEM((Bb, W), F32), pltpu.VMEM((CONV_W - 1, Bb, W), F32),
                        pltpu.VMEM((ts, Bb, W), F32), pltpu.VMEM((ts, Bb, W), F32)],
        compiler_params=_cparams("arbitrary"), name="rg_lru",
    )(gate_t, x_t, conv_w, conv_b, wab, bab, lam)


def _even_mixer(h, B, S, norm_w, w_in, fox_bf, conv_w, a_log, dt_bias, gdn_norm_w, w_out):
    T, D = h.shape
    FW = FOX_HEADS * HEAD_DIM
    GW = GDN_HEADS * GDN_DIM
    o_ff = 3 * FW
    o_g = o_ff + FOX_HEADS
    o_gb = o_g + 4 * GW
    small = jnp.concatenate([w_in[:, o_ff:o_g], w_in[:, o_gb:o_gb + 2 * GDN_HEADS],
                             jnp.zeros((D, LANES - FOX_HEADS - 2 * GDN_HEADS), w_in.dtype)], axis=1)
    w = jnp.concatenate([w_in[:, :o_ff], w_in[:, o_g:o_gb], small], axis=1).astype(BF16)
    segs = [(0, 0, 0, FW, False, HEAD_DIM ** -0.5 * LOG2E),
            (0, FW, FW, 2 * FW, False, 1.0),
            (1, 0, 3 * FW, 3 * GW, False, 1.0),
            (2, 0, 3 * FW + 3 * GW, GW, False, 1.0),
            (3, 0, 3 * FW + 4 * GW, LANES, False, 1.0)]
    fqkv, gqkv, gz, sm = _norm_proj(h, norm_w, w, segs,
                                    [(3 * FW, BF16), (3 * GW, F32), (GW, F32), (LANES, F32)])
    F = _fox_gate(sm[:, :FOX_HEADS].reshape(B, S, FOX_HEADS), fox_bf)
    fox = _fox_attn(fqkv, F, B, S)
    gdn = _gdn(gqkv, gz, sm, conv_w, a_log, dt_bias, gdn_norm_w, B, S)
    wo = w_out.astype(BF16)
    return _out_proj(h, fox, gdn, wo[:FW], wo[FW:])


def _odd_mixer(h, B, S, rope_tabs, norm_w, w_in, k_pe, k_w1, k_w2, v_pe, v_w1, v_w2,
               conv_w, conv_b, wa, ba, wx, bx, lam, w_out):
    T, D = h.shape
    QW = NSA_HEADS * HEAD_DIM
    KW = NSA_GROUPS * HEAD_DIM
    G = NSA_GROUPS
    o_ng = QW + 6 * KW
    ngw = NSA_HEADS * 3
    w = jnp.concatenate([w_in[:, :o_ng], w_in[:, o_ng:o_ng + ngw], jnp.zeros((D, LANES - ngw), w_in.dtype),
                         w_in[:, o_ng + ngw:]], axis=1).astype(BF16)
    segs = [(0, 0, 0, QW, True, HEAD_DIM ** -0.5 * LOG2E),
            (1, 0, QW, KW, True, 1.0),
            (1, KW, QW + KW, KW, False, 1.0),
            (2, 0, QW + 2 * KW, KW, True, 1.0),
            (2, KW, QW + 3 * KW, KW, False, 1.0),
            (2, 2 * KW, QW + 4 * KW, KW, True, 1.0),
            (2, 3 * KW, QW + 5 * KW, KW, False, 1.0),
            (3, 0, o_ng, LANES, False, 1.0),
            (4, 0, o_ng + LANES, 2 * LRU_W, False, 1.0)]
    q, kvc, kv4, ng, rgx = _norm_proj(h, norm_w, w, segs,
                                      [(QW, BF16), (2 * KW, F32), (4 * KW, BF16), (LANES, F32), (2 * LRU_W, F32)],
                                      rope_tabs=rope_tabs)
    nh = S // CMP_STRIDE
    xkv = kvc.reshape(B, S, 2 * G, HEAD_DIM).transpose(0, 2, 1, 3).reshape(B, 2 * G, nh, CMP_STRIDE * HEAD_DIM)
    pe = jnp.stack([k_pe.reshape(-1), v_pe.reshape(-1)]).astype(F32)
    pe = jnp.broadcast_to(pe[:, None, :], (2, SUBLANES, pe.shape[-1]))
    w1 = jnp.stack([k_w1, v_w1]).astype(BF16)
    w2 = jnp.stack([k_w2, v_w2]).astype(BF16)
    cmp = _compress(xkv, pe, w1, w2)
    n_cmp = (S - CMP_BLOCK) // CMP_STRIDE + 1
    ncp = -(-nh // LANES) * LANES
    if ncp != nh:
        cmp = jnp.pad(cmp, ((0, 0), (0, 0), (0, ncp - nh), (0, 0)))
    n_sel = S // SEL_BLOCK
    cs = np.arange(ncp) * CMP_STRIDE
    ss = np.arange(LANES) * SEL_BLOCK
    ov = ((cs[:, None] <= ss[None, :] + SEL_BLOCK - 1) & (cs[:, None] + CMP_BLOCK - 1 >= ss[None, :])
          & (np.arange(ncp)[:, None] < n_cmp) & (np.arange(LANES)[None, :] < n_sel))
    overlap = jnp.asarray(ov.astype(np.float32))
    oc, bias = _nsa_cmp(q, cmp, cmp, overlap, B, S)
    kv = kv4.reshape(B, S, 4, G, HEAD_DIM).transpose(2, 0, 3, 1, 4)
    blk_of_pos = np.arange(S) // SEL_BLOCK
    onehot = jnp.asarray((blk_of_pos[:, None] == np.arange(LANES)[None, :]).astype(np.float32), dtype=BF16)
    kaug = jnp.concatenate([jnp.broadcast_to(onehot, (B, G, S, LANES)), kv[0]], axis=-1)
    gates = ng[:, :ngw].reshape(B, S, G, NSA_HPG * 3).transpose(0, 2, 1, 3)
    gates = jnp.pad(gates, ((0, 0), (0, 0), (0, 0), (0, LANES - NSA_HPG * 3)))
    nsa = _nsa_main(q, bias, kaug, kv[1], kv[2], kv[3], oc, gates, B, S)
    gate_t = rgx[:, :LRU_W].reshape(B, S, LRU_W).transpose(1, 0, 2)
    x_t = rgx[:, LRU_W:].reshape(B, S, LRU_W).transpose(1, 0, 2)
    nblk, bw, _ = wa.shape
    eye = jnp.eye(nblk, dtype=wa.dtype)
    dense = lambda wb: (eye[:, None, :, None] * wb[:, :, None, :]).reshape(nblk * bw, nblk * bw)
    wab = jnp.concatenate([dense(wa), dense(wx)], axis=1).astype(BF16)
    bab = jnp.concatenate([ba, bx]).reshape(1, 2 * LRU_W).astype(F32)
    lru_t = _lru(gate_t, x_t, conv_w.astype(F32), conv_b.reshape(1, LRU_W).astype(F32), wab, bab,
                 lam.reshape(1, LRU_W).astype(F32))
    lru = lru_t.transpose(1, 0, 2).reshape(T, LRU_W)
    wo = w_out.astype(BF16)
    return _out_proj(h, nsa, lru, wo[:QW], wo[QW:])


def kernel(x, p, positions, even_norm_mix, even_w_in, even_fox_bf, even_gdn_conv_w, even_gdn_a_log, even_gdn_dt_bias, even_gdn_norm_w, even_w_out, odd_norm_mix, odd_w_in, odd_cmp_k_pe, odd_cmp_k_w1, odd_cmp_k_w2, odd_cmp_v_pe, odd_cmp_v_w1, odd_cmp_v_w2, odd_rg_conv_w, odd_rg_conv_b, odd_rg_wa, odd_rg_ba, odd_rg_wx, odd_rg_bx, odd_rg_lambda, odd_w_out, mlp_norm, mlp_w_up, mlp_w_down, ple_norm, ple_w_gate, ple_w_proj, final_norm):
    B, S, D = x.shape
    T = B * S
    depth = p.shape[0]
    h = x.reshape(T, D)
    rope_tabs = _rope_tables(positions) if depth > 1 else None
    for i in range(depth):
        j = i // 2
        if i % 2 == 0:
            h = _even_mixer(h, B, S, even_norm_mix[j], even_w_in[j], even_fox_bf[j], even_gdn_conv_w[j],
                            even_gdn_a_log[j], even_gdn_dt_bias[j], even_gdn_norm_w[j], even_w_out[j])
        else:
            h = _odd_mixer(h, B, S, rope_tabs, odd_norm_mix[j], odd_w_in[j], odd_cmp_k_pe[j], odd_cmp_k_w1[j],
                           odd_cmp_k_w2[j], odd_cmp_v_pe[j], odd_cmp_v_w1[j], odd_cmp_v_w2[j], odd_rg_conv_w[j],
                           odd_rg_conv_b[j], odd_rg_wa[j], odd_rg_ba[j], odd_rg_wx[j], odd_rg_bx[j],
                           odd_rg_lambda[j], odd_w_out[j])
        h = _mlp(h, mlp_norm[i], mlp_w_up[i].astype(BF16), mlp_w_down[i].astype(BF16))
        h = _ple(h, p[i].reshape(T, -1), ple_norm[i], ple_w_gate[i].astype(BF16), ple_w_proj[i].astype(BF16),
                 final_norm, final=(i == depth - 1))
    return h.reshape(B, S, D)
```

```python
import functools

import numpy as np
import jax
import jax.numpy as jnp
from jax import lax
from jax.experimental import pallas as pl
from jax.experimental.pallas import tpu as pltpu

F32 = jnp.float32
BF16 = jnp.bfloat16
HIGHEST = lax.Precision.HIGHEST

NORM_EPS = 1e-6
LOG2E = float(np.log2(np.e))
NEG = -1e30
REMOVED = -3e38
LANES = 128
SUBLANES = 8
VMEM_LIMIT = 56 * 1024 * 1024

HEAD_DIM = 64
ROT_DIM = 16
ROPE_THETA = 500000.0
FOX_HEADS = 8
FOX_F_PIECES = 3
GDN_HEADS = 4
GDN_DIM = 128
GDN_CHUNK = 64
NSA_HEADS = 8
NSA_GROUPS = 2
NSA_HPG = NSA_HEADS // NSA_GROUPS
CMP_BLOCK = 32
CMP_STRIDE = 16
CMP_HIDDEN = 128
SEL_BLOCK = 64
SEL_TOPK = 16
WINDOW = 512
LRU_W = 512
RG_C = 8.0
CONV_W = 4


def _cparams(*sem):
    return pltpu.CompilerParams(dimension_semantics=sem, vmem_limit_bytes=VMEM_LIMIT)


def _dot(a, b, **kw):
    return jnp.dot(a, b, preferred_element_type=F32, **kw)


def _dot_nt(a, b, **kw):
    return lax.dot_general(a, b, (((1,), (1,)), ((), ())), preferred_element_type=F32, **kw)


def _dot_tn(a, b, **kw):
    return lax.dot_general(a, b, (((0,), (0,)), ((), ())), preferred_element_type=F32, **kw)


def _softplus(x):
    return jnp.maximum(x, 0.0) + jnp.log1p(jnp.exp(-jnp.abs(x)))


def _sigmoid(x):
    return 1.0 / (1.0 + jnp.exp(-x))


def _silu(x):
    return x * _sigmoid(x)


def _gelu_tanh(x):
    return 0.5 * x * (1.0 + jnp.tanh(np.float32(np.sqrt(2.0 / np.pi)) * (x + 0.044715 * (x * x * x))))


def _rms(x, w):
    ms = jnp.mean(x * x, axis=-1, keepdims=True)
    return x * lax.rsqrt(ms + NORM_EPS) * w


def _apply_rope(y, c, s1, s2):
    outs = []
    for g in range(y.shape[1] // LANES):
        yg = y[:, g * LANES:(g + 1) * LANES]
        outs.append(yg * c + pltpu.roll(yg, LANES - ROT_DIM // 2, 1) * s1 + pltpu.roll(yg, ROT_DIM // 2, 1) * s2)
    return outs[0] if len(outs) == 1 else jnp.concatenate(outs, axis=1)


def _pad_heads(y, consts):
    tm = y.shape[0]
    lane = lax.broadcasted_iota(jnp.int32, (tm, LANES), 1)
    cst = jnp.zeros((tm, LANES), F32)
    for ln, val in consts:
        cst = jnp.where(lane == HEAD_DIM + ln, val, cst)
    low = lane < HEAD_DIM
    outs = []
    for g in range(y.shape[1] // LANES):
        yg = y[:, g * LANES:(g + 1) * LANES]
        outs.append(jnp.where(low, yg, cst))
        outs.append(jnp.where(low, pltpu.roll(yg, HEAD_DIM, 1), cst))
    return jnp.concatenate(outs, axis=1)


def _norm_proj_kernel(*refs, segs, n_out, rope):
    if rope:
        x_ref, nw_ref, w_ref, c_ref, s1_ref, s2_ref = refs[:6]
        out_refs = refs[6:6 + n_out]
    else:
        x_ref, nw_ref, w_ref = refs[:3]
        out_refs = refs[3:3 + n_out]
    xn = _rms(x_ref[...], nw_ref[...]).astype(BF16)
    for (oi, oc, wc, width, do_rope, scale, pad) in segs:
        y = _dot(xn, w_ref[:, wc:wc + width])
        if do_rope:
            y = _apply_rope(y, c_ref[...], s1_ref[...], s2_ref[...])
        if scale != 1.0:
            y = y * scale
        if pad is not None:
            y = _pad_heads(y, pad)
        out_refs[oi][:, oc:oc + y.shape[1]] = y.astype(out_refs[oi].dtype)


def _norm_proj(x, nw, w, segs, out_defs, rope_tabs=None, tm=512):
    T, D = x.shape
    N = w.shape[1]
    tm = min(tm, T)
    rope = rope_tabs is not None
    in_specs = [pl.BlockSpec((tm, D), lambda i: (i, 0)),
                pl.BlockSpec((1, D), lambda i: (0, 0)),
                pl.BlockSpec((D, N), lambda i: (0, 0))]
    args = [x, nw.reshape(1, D), w]
    if rope:
        in_specs += [pl.BlockSpec((tm, LANES), lambda i: (i, 0))] * 3
        args += list(rope_tabs)
    out_shape, out_specs = [], []
    for od in out_defs:
        wd, dt = od[0], od[1]
        if len(od) == 3:
            Bb, Ss = od[2]
            nst = Ss // tm
            out_shape.append(jax.ShapeDtypeStruct((Ss, Bb * wd), dt))
            out_specs.append(pl.BlockSpec((tm, wd), lambda i, nst=nst: (i % nst, i // nst)))
        else:
            out_shape.append(jax.ShapeDtypeStruct((T, wd), dt))
            out_specs.append(pl.BlockSpec((tm, wd), lambda i: (i, 0)))
    return pl.pallas_call(
        functools.partial(_norm_proj_kernel, segs=tuple(segs), n_out=len(out_defs), rope=rope),
        grid=(T // tm,), in_specs=in_specs, out_specs=out_specs, out_shape=out_shape,
        compiler_params=_cparams("parallel"), name="norm_proj",
    )(*args)


def _rope_table_kernel(pos_ref, f_ref, cos_ref, sin_ref, nsin_ref):
    ang = pos_ref[0] * f_ref[...]
    c = jnp.cos(ang)
    s = jnp.sin(ang)
    cos_ref[0] = c
    sin_ref[0] = s
    nsin_ref[0] = -s


def _rope_tables(positions):
    B, S = positions.shape
    half = ROT_DIM // 2
    inv_freq = ROPE_THETA ** (-jnp.arange(half, dtype=F32) * (2.0 / ROT_DIM))
    pos = positions.astype(F32).reshape(B, 1, S)
    sh = jax.ShapeDtypeStruct((B, half, S), F32)
    spec = pl.BlockSpec((1, half, S), lambda b: (b, 0, 0))
    cos, sin, nsin = pl.pallas_call(
        _rope_table_kernel, grid=(B,),
        in_specs=[pl.BlockSpec((1, 1, S), lambda b: (b, 0, 0)), pl.BlockSpec((half, 1), lambda b: (0, 0))],
        out_specs=[spec, spec, spec], out_shape=[sh, sh, sh],
        compiler_params=_cparams("parallel"), name="rope_tables",
    )(pos, inv_freq.reshape(half, 1))
    tr = lambda t: t.transpose(0, 2, 1).reshape(B * S, half)
    cos, sin, nsin = tr(cos), tr(sin), tr(nsin)
    T = B * S
    ones = jnp.ones((T, HEAD_DIM - ROT_DIM), F32)
    zeros = jnp.zeros((T, HEAD_DIM - ROT_DIM), F32)
    z8 = jnp.zeros((T, half), F32)
    c = jnp.tile(jnp.concatenate([cos, cos, ones], axis=1), (1, LANES // HEAD_DIM))
    s1 = jnp.tile(jnp.concatenate([nsin, z8, zeros], axis=1), (1, LANES // HEAD_DIM))
    s2 = jnp.tile(jnp.concatenate([z8, sin, zeros], axis=1), (1, LANES // HEAD_DIM))
    return c, s1, s2


def _tail_kernel(h_ref, a1_ref, a2_ref, w1_ref, w2_ref, nwm_ref, wu_ref, wd_ref, p_ref, nwp_ref, wg_ref, wp_ref,
                 fw_ref, o_ref, xn_ref, acc_ref, *, final):
    f = pl.program_id(1)

    @pl.when(f == 0)
    def _():
        hm = h_ref[...] + _dot(a1_ref[...], w1_ref[...]) + _dot(a2_ref[...], w2_ref[...])
        acc_ref[...] = hm
        xn_ref[...] = _rms(hm, nwm_ref[...]).astype(BF16)

    u = jnp.maximum(_dot(xn_ref[...], wu_ref[...]), 0.0)
    acc_ref[...] += _dot((u * u).astype(BF16), wd_ref[...])

    @pl.when(f == pl.num_programs(1) - 1)
    def _():
        h2 = acc_ref[...]
        gate = _sigmoid(_dot(_rms(h2, nwp_ref[...]).astype(BF16), wg_ref[...]))
        y = h2 + gate * _dot(p_ref[...].astype(BF16), wp_ref[...])
        if final:
            y = _rms(y, fw_ref[...])
        o_ref[...] = y


def _layer_tail(h, a1, a2, w1, w2, a2_time_major, nwm, wu, wd, p, nwp, wg, wp, fw, final, tm=512, tf=1024):
    T, D = h.shape
    FF = wu.shape[1]
    P = p.shape[1]
    tm = min(tm, T)
    K1, K2 = w1.shape[0], w2.shape[0]
    if a2_time_major is None:
        a2_spec = pl.BlockSpec((tm, K2), lambda i, f: (i, 0))
    else:
        nst = a2_time_major[1] // tm
        a2_spec = pl.BlockSpec((tm, K2), lambda i, f: (i % nst, i // nst))
    row = lambda w: pl.BlockSpec((tm, w), lambda i, f: (i, 0))
    full = lambda r, c: pl.BlockSpec((r, c), lambda i, f: (0, 0))
    return pl.pallas_call(
        functools.partial(_tail_kernel, final=final), grid=(T // tm, FF // tf),
        in_specs=[row(D), row(K1), a2_spec, full(K1, D), full(K2, D), full(1, D),
                  pl.BlockSpec((D, tf), lambda i, f: (0, f)), pl.BlockSpec((tf, D), lambda i, f: (f, 0)),
                  row(P), full(1, D), full(D, D), full(P, D), full(1, D)],
        out_specs=row(D),
        out_shape=jax.ShapeDtypeStruct((T, D), F32),
        scratch_shapes=[pltpu.VMEM((tm, D), BF16), pltpu.VMEM((tm, D), F32)],
        compiler_params=_cparams("parallel", "arbitrary"), name="layer_tail",
    )(h, a1, a2, w1, w2, nwm.reshape(1, D), wu, wd, p, nwp.reshape(1, D), wg, wp, fw.reshape(1, D))


def _fox_gate_kernel(x_ref, b_ref, o_ref, carry_ref, *, ts):
    @pl.when(pl.program_id(1) == 0)
    def _():
        carry_ref[...] = jnp.zeros_like(carry_ref)

    r = lax.broadcasted_iota(jnp.int32, (LANES, LANES), 0)
    c = lax.broadcasted_iota(jnp.int32, (LANES, LANES), 1)
    lower = (r >= c).astype(F32)
    lane = lax.broadcasted_iota(jnp.int32, (LANES, LANES), 1)

    def body(n, carry):
        rows = pl.ds(pl.multiple_of(n * LANES, LANES), LANES)
        x = x_ref[rows, :] + b_ref[...]
        logf = jnp.minimum(x, 0.0) - jnp.log1p(jnp.exp(-jnp.abs(x)))
        cum = _dot(lower, logf, precision=HIGHEST) + carry
        f = cum * LOG2E
        hi = f.astype(BF16).astype(F32)
        mid = (f - hi).astype(BF16).astype(F32)
        lo = f - hi - mid
        for h in range(FOX_HEADS):
            v = jnp.where(lane == HEAD_DIM, hi[:, h:h + 1],
                          jnp.where(lane == HEAD_DIM + 1, mid[:, h:h + 1],
                                    jnp.where(lane == HEAD_DIM + 2, lo[:, h:h + 1], 0.0)))
            o_ref[rows, h * LANES:(h + 1) * LANES] = v.astype(o_ref.dtype)
        return cum[LANES - 1:LANES, :]

    carry_ref[0:1, :] = lax.fori_loop(0, ts // LANES, body, carry_ref[0:1, :])


def _fox_gate(sm, bias, B, S, ts=1024):
    T = B * S
    ts = min(ts, S)
    ns = S // ts
    b = jnp.zeros((1, LANES), F32).at[0, :FOX_HEADS].set(bias.astype(F32))
    return pl.pallas_call(
        functools.partial(_fox_gate_kernel, ts=ts), grid=(B, ns),
        in_specs=[pl.BlockSpec((ts, LANES), lambda bb, n: (bb * ns + n, 0)),
                  pl.BlockSpec((1, LANES), lambda bb, n: (0, 0))],
        out_specs=pl.BlockSpec((ts, FOX_HEADS * LANES), lambda bb, n: (bb * ns + n, 0)),
        out_shape=jax.ShapeDtypeStruct((T, FOX_HEADS * LANES), BF16),
        scratch_shapes=[pltpu.VMEM((SUBLANES, LANES), F32)],
        compiler_params=_cparams("parallel", "arbitrary"), name="fox_gate",
    )(sm, b)


def _fox_attn_kernel(q_ref, k_ref, v_ref, f_ref, o_ref, *, tq, tk, dk, hp):
    i = pl.program_id(2)
    qs = [q_ref[:, hh * LANES:(hh + 1) * LANES] for hh in range(hp)]
    nsub = tq // tk

    def tile(start, size, r0, carry, masked):
        rows = pl.ds(start, size)
        ss = [_dot_nt(qs[hh][r0:], k_ref[rows, hh * LANES:(hh + 1) * LANES] + f_ref[rows, hh * LANES:(hh + 1) * LANES])
              for hh in range(hp)]
        out = []
        for hh in range(hp):
            m, acc = carry[hh]
            s = ss[hh]
            if masked:
                r = lax.broadcasted_iota(jnp.int32, (tq - r0, size), 0)
                c = lax.broadcasted_iota(jnp.int32, (tq - r0, size), 1)
                s = jnp.where(c <= r, s, NEG)
            m_new = jnp.maximum(m[r0:], jnp.max(s, axis=1, keepdims=True))
            pr = jnp.exp2(s - m_new).astype(BF16)
            acc_new = jnp.exp2(m[r0:] - m_new) * acc[r0:] + _dot(pr, v_ref[rows, hh * LANES:(hh + 1) * LANES])
            if r0:
                m_new = jnp.concatenate([m[:r0], m_new], axis=0)
                acc_new = jnp.concatenate([acc[:r0], acc_new], axis=0)
            out.append((m_new, acc_new))
        return tuple(out)

    init = tuple((jnp.full((tq, 1), NEG, F32), jnp.zeros((tq, LANES), F32)) for _ in range(hp))
    carry = lax.fori_loop(0, i * nsub, lambda j, c: tile(pl.multiple_of(j * tk, tk), tk, 0, c, False), init)
    for d in range(tq // dk):
        carry = tile(pl.multiple_of(i * tq + d * dk, dk), dk, d * dk, carry, True)
    o_ref[...] = jnp.concatenate([acc[:, :HEAD_DIM] / acc[:, HEAD_DIM:HEAD_DIM + 1] for (_, acc) in carry],
                                 axis=1).astype(o_ref.dtype)


def _fox_attn(q, k, v, f3, B, S, tq=1024, tk=1024, dk=512, hp=2):
    T = B * S
    tq = min(tq, S)
    tk = min(tk, tq)
    dk = min(dk, tq)
    nq = S // tq
    bw = hp * LANES
    ng = FOX_HEADS // hp
    res = pl.BlockSpec((S, bw), lambda b, p, i: (b, p))
    return pl.pallas_call(
        functools.partial(_fox_attn_kernel, tq=tq, tk=tk, dk=dk, hp=hp), grid=(B, ng, nq),
        in_specs=[pl.BlockSpec((tq, bw), lambda b, p, i: (b * nq + i, p)), res, res, res],
        out_specs=pl.BlockSpec((tq, hp * HEAD_DIM), lambda b, p, i: (b * nq + i, p)),
        out_shape=jax.ShapeDtypeStruct((T, FOX_HEADS * HEAD_DIM), BF16),
        compiler_params=_cparams("parallel", "parallel", "arbitrary"), name="fox_attn",
    )(q, k, v, f3)


def _cumsum_rows(x):
    n = x.shape[0]
    row = lax.broadcasted_iota(jnp.int32, x.shape, 0)
    d = 1
    while d < n:
        x = x + jnp.where(row >= d, pltpu.roll(x, d, 0), 0.0)
        d *= 2
    return x


def _gdn_kernel(x_ref, z_ref, sm_ref, cw_ref, par_ref, nw_ref, o_ref, state_ref, xp_ref, *, cps):
    C = GDN_CHUNK
    D = GDN_DIM
    W = GDN_HEADS * D
    R = cps * C
    n = pl.program_id(1)

    @pl.when(n == 0)
    def _():
        state_ref[...] = jnp.zeros_like(state_ref)
        xp_ref[0:SUBLANES, :] = jnp.zeros((SUBLANES, 3 * W), F32)

    x = x_ref[...]
    xp_ref[SUBLANES:SUBLANES + R, :] = x
    cw = cw_ref[...]
    y = x * cw[CONV_W - 1:CONV_W]
    for j in range(CONV_W - 1):
        off = SUBLANES - (CONV_W - 1) + j
        y = y + xp_ref[off:off + R, :] * cw[j:j + 1]
    xp_ref[0:SUBLANES, :] = x[R - SUBLANES:R]
    y = _silu(y)

    sm = sm_ref[...]
    beta_all = _sigmoid(sm)
    g_all = -jnp.exp(par_ref[0:1, :]) * _softplus(sm + par_ref[1:2, :])
    ri = lax.broadcasted_iota(jnp.int32, (C, C), 0)
    ci = lax.broadcasted_iota(jnp.int32, (C, C), 1)
    incl = ri >= ci
    strict = ri > ci

    chains = [(c, h) for c in range(cps) for h in range(GDN_HEADS)]
    gcs = [_cumsum_rows(g_all[c * C:(c + 1) * C]) for c in range(cps)]
    gcts = [gc.T for gc in gcs]
    pre = {}
    for (c, h) in chains:
        r0 = c * C
        q = y[r0:r0 + C, h * D:(h + 1) * D]
        k = y[r0:r0 + C, W + h * D:W + (h + 1) * D]
        v = y[r0:r0 + C, 2 * W + h * D:2 * W + (h + 1) * D]
        qn = q * lax.rsqrt(jnp.sum(q * q, axis=-1, keepdims=True) + 1e-6) * (D ** -0.5)
        kn = k * lax.rsqrt(jnp.sum(k * k, axis=-1, keepdims=True) + 1e-6)
        beta = beta_all[r0:r0 + C, 8 + h:9 + h]
        gcol = gcs[c][:, 12 + h:13 + h]
        grow = gcts[c][12 + h:13 + h, :]
        decay = jnp.where(incl, jnp.exp(jnp.where(incl, gcol - grow, 0.0)), 0.0)
        kb = kn * beta
        g_last = gcol[C - 1:C, :]
        pre[c, h] = dict(qn16=qn.astype(BF16), kn16=kn.astype(BF16), kb16=kb.astype(BF16), decay=decay,
                         rhs=jnp.concatenate([v * beta, kb * jnp.exp(gcol)], axis=1),
                         qg16=(qn * jnp.exp(gcol)).astype(BF16),
                         kg16=(kn * jnp.exp(g_last - gcol)).astype(BF16), e_last=jnp.exp(g_last))
    kk = {ch: _dot_nt(pre[ch]["kb16"], pre[ch]["kn16"]) for ch in chains}
    qk = {ch: _dot_nt(pre[ch]["qn16"], pre[ch]["kn16"]) for ch in chains}
    P = {ch: -jnp.where(strict, kk[ch] * pre[ch]["decay"], 0.0) for ch in chains}
    X = dict(P)
    for _ in range(5):
        P = {ch: _dot(P[ch].astype(BF16), P[ch].astype(BF16)) for ch in chains}
        XP = {ch: _dot(X[ch].astype(BF16), P[ch].astype(BF16)) for ch in chains}
        X = {ch: X[ch] + P[ch] + XP[ch] for ch in chains}
    sol = {ch: pre[ch]["rhs"] + _dot(X[ch].astype(BF16), pre[ch]["rhs"].astype(BF16)) for ch in chains}
    attn16 = {ch: jnp.where(incl, qk[ch] * pre[ch]["decay"], 0.0).astype(BF16) for ch in chains}

    heads = range(GDN_HEADS)
    states = [state_ref[h] for h in heads]
    for c in range(cps):
        r0 = c * C
        st16 = [states[h].astype(BF16) for h in heads]
        ws = [_dot(sol[c, h][:, D:].astype(BF16), st16[h]) for h in heads]
        qs = [_dot(pre[c, h]["qg16"], st16[h]) for h in heads]
        v_new = [(sol[c, h][:, :D] - ws[h]).astype(BF16) for h in heads]
        av = [_dot(attn16[c, h], v_new[h]) for h in heads]
        kv = [_dot_tn(pre[c, h]["kg16"], v_new[h]) for h in heads]
        outs = []
        for h in heads:
            states[h] = states[h] * pre[c, h]["e_last"] + kv[h]
            o = _rms(qs[h] + av[h], nw_ref[...]) * _silu(z_ref[r0:r0 + C, h * D:(h + 1) * D])
            outs.append(o)
        o_ref[r0:r0 + C, :] = jnp.concatenate(outs, axis=1).astype(o_ref.dtype)
    for h in heads:
        state_ref[h] = states[h]


def _gdn(gqkv, gz, small, conv_w, a_log, dt_bias, norm_w, B, S, cps=2):
    T = B * S
    C = GDN_CHUNK
    R = cps * C
    N = S // R
    W = GDN_HEADS * GDN_DIM
    par = jnp.zeros((SUBLANES, LANES), F32)
    par = par.at[0, 12:16].set(a_log.astype(F32)).at[1, 12:16].set(dt_bias.astype(F32))
    return pl.pallas_call(
        functools.partial(_gdn_kernel, cps=cps), grid=(B, N),
        in_specs=[pl.BlockSpec((R, 3 * W), lambda b, n: (b * N + n, 0)),
                  pl.BlockSpec((R, W), lambda b, n: (b * N + n, 0)),
                  pl.BlockSpec((R, LANES), lambda b, n: (b * N + n, 0)),
                  pl.BlockSpec((CONV_W, 3 * W), lambda b, n: (0, 0)),
                  pl.BlockSpec((SUBLANES, LANES), lambda b, n: (0, 0)),
                  pl.BlockSpec((1, GDN_DIM), lambda b, n: (0, 0))],
        out_specs=pl.BlockSpec((R, W), lambda b, n: (b * N + n, 0)),
        out_shape=jax.ShapeDtypeStruct((T, W), BF16),
        scratch_shapes=[pltpu.VMEM((GDN_HEADS, GDN_DIM, GDN_DIM), F32),
                        pltpu.VMEM((SUBLANES + R, 3 * W), F32)],
        compiler_params=_cparams("parallel", "arbitrary"), name="gdn",
    )(gqkv, gz, small, conv_w.astype(F32), par, norm_w.reshape(1, GDN_DIM).astype(F32))


def _compress_kernel(x_ref, pe_ref, w1_ref, w2_ref, o_ref):
    half = w1_ref.shape[1] // 2
    x = x_ref[0, 0].astype(BF16)
    w1 = w1_ref[0]
    a = _dot(x, w1[:half])
    b = _dot(x, w1[half:])
    c = _dot(pe_ref[0].astype(BF16), w1)
    nh = x.shape[0]
    b_next = pltpu.roll(b, nh - 1, 0)
    hid = _gelu_tanh(a + b_next + c[0:1])
    o_ref[0, 0] = _dot(hid.astype(BF16), w2_ref[0]).astype(o_ref.dtype)


def _compress(xkv, pe, w1, w2):
    B, four, nh, fw = xkv.shape
    return pl.pallas_call(
        _compress_kernel, grid=(B, four),
        in_specs=[pl.BlockSpec((1, 1, nh, fw), lambda b, j: (b, j, 0, 0)),
                  pl.BlockSpec((1, SUBLANES, 2 * fw), lambda b, j: (j // 2, 0, 0)),
                  pl.BlockSpec((1, 2 * fw, CMP_HIDDEN), lambda b, j: (j // 2, 0, 0)),
                  pl.BlockSpec((1, CMP_HIDDEN, HEAD_DIM), lambda b, j: (j // 2, 0, 0))],
        out_specs=pl.BlockSpec((1, 1, nh, HEAD_DIM), lambda b, j: (b, j, 0, 0)),
        out_shape=jax.ShapeDtypeStruct((B, four, nh, HEAD_DIM), BF16),
        compiler_params=_cparams("parallel", "parallel"), name="nsa_compress",
    )(xkv, pe, w1, w2)


def _stack_heads(qb):
    return jnp.concatenate([qb[:, h * HEAD_DIM:(h + 1) * HEAD_DIM] for h in range(NSA_HPG)], axis=0)


def _unstack_heads(o, tq):
    return jnp.concatenate([o[h * tq:(h + 1) * tq] for h in range(NSA_HPG)], axis=1)


def _nsa_cmp_kernel(q_ref, kc_ref, vc_ref, ov_ref, oc_ref, bias_ref, *, tq, n_sel, top_k):
    i = pl.program_id(2)
    q4 = _stack_heads(q_ref[...])
    ncp = kc_ref.shape[2]
    s = _dot_nt(q4, kc_ref[0, 0])
    row = lax.broadcasted_iota(jnp.int32, (NSA_HPG * tq, 1), 0)
    t4 = i * tq + (row & (tq - 1))
    cmp_end = lax.broadcasted_iota(jnp.int32, (1, ncp), 1) * CMP_STRIDE + (CMP_BLOCK - 1)
    mask = cmp_end <= t4
    s = jnp.where(mask, s, NEG)
    m = jnp.max(s, axis=1, keepdims=True)
    p = jnp.where(mask, jnp.exp2(s - m), 0.0)
    l = jnp.sum(p, axis=1, keepdims=True)
    p = p * jnp.where(l > 0.0, 1.0 / l, 0.0)
    oc = _dot(p.astype(BF16), vc_ref[0, 0])
    oc_ref[...] = _unstack_heads(oc, tq)
    psum = p[0:tq] + p[tq:2 * tq] + p[2 * tq:3 * tq] + p[3 * tq:4 * tq]
    imp = _dot(psum, ov_ref[...], precision=HIGHEST)
    blk = lax.broadcasted_iota(jnp.int32, (1, LANES), 1)
    t = i * tq + lax.broadcasted_iota(jnp.int32, (tq, 1), 0)
    cur = t >> (SEL_BLOCK.bit_length() - 1)
    forced = (blk == 0) | (blk == cur) | (blk == cur - 1)
    future = blk * SEL_BLOCK > t
    imp = jnp.where(future, NEG, jnp.where(forced, -NEG, imp))
    imp = jnp.where(blk < n_sel, imp, REMOVED)
    imp_t = imp.T
    blk_t = lax.broadcasted_iota(jnp.int32, (LANES, tq), 0).astype(F32)
    sel_t = jnp.zeros((LANES, tq), F32)
    for _ in range(top_k):
        mx = jnp.max(imp_t, axis=0, keepdims=True)
        first = jnp.min(jnp.where(imp_t == mx, blk_t, float(LANES)), axis=0, keepdims=True)
        pick = blk_t == first
        sel_t = jnp.where(pick, 1.0, sel_t)
        imp_t = jnp.where(pick, REMOVED, imp_t)
    sel = sel_t.T
    bias_ref[0, 0] = jnp.where((sel > 0.0) & jnp.logical_not(future), 0.0, NEG).astype(bias_ref.dtype)


def _nsa_cmp(q, kc, vc, overlap, B, S, tq=256):
    T = B * S
    tq = min(tq, S)
    nq = S // tq
    ncp = kc.shape[2]
    n_sel = S // SEL_BLOCK
    GW = NSA_HPG * HEAD_DIM
    return pl.pallas_call(
        functools.partial(_nsa_cmp_kernel, tq=tq, n_sel=n_sel, top_k=min(SEL_TOPK, n_sel)),
        grid=(B, NSA_GROUPS, nq),
        in_specs=[pl.BlockSpec((tq, GW), lambda b, g, i: (b * nq + i, g)),
                  pl.BlockSpec((1, 1, ncp, HEAD_DIM), lambda b, g, i: (b, g, 0, 0)),
                  pl.BlockSpec((1, 1, ncp, HEAD_DIM), lambda b, g, i: (b, NSA_GROUPS + g, 0, 0)),
                  pl.BlockSpec((ncp, LANES), lambda b, g, i: (0, 0))],
        out_specs=[pl.BlockSpec((tq, GW), lambda b, g, i: (b * nq + i, g)),
                   pl.BlockSpec((1, 1, tq, LANES), lambda b, g, i: (b, g, i, 0))],
        out_shape=[jax.ShapeDtypeStruct((T, NSA_HEADS * HEAD_DIM), F32),
                   jax.ShapeDtypeStruct((B, NSA_GROUPS, S, LANES), BF16)],
        compiler_params=_cparams("parallel", "parallel", "parallel"), name="nsa_cmp",
    )(q, kc, vc, overlap)


def _nsa_main_kernel(q_ref, bias_ref, ks_ref, oh_ref, vs_ref, kw_ref, vw_ref, oc_ref, g_ref, o_ref, *, qn, tk, wspan):
    R = NSA_HPG * qn
    tb = 2 * tk
    i = pl.program_id(2)
    q4 = _stack_heads(q_ref[...])
    q4p = jnp.concatenate([q4, jnp.zeros_like(q4)], axis=1)
    b4 = jnp.concatenate([bias_ref[0, 0]] * NSA_HPG, axis=0)
    qa = jnp.concatenate([q4p, b4], axis=1)
    t4 = i * qn + (lax.broadcasted_iota(jnp.int32, (R, 1), 0) & (qn - 1))

    def step(start, size, carry, masked):
        m, acc = carry
        rows = pl.ds(start, size)
        ka = jnp.concatenate([ks_ref[rows, :], oh_ref[rows, :]], axis=1)
        s = _dot_nt(qa, ka)
        if masked:
            kpos = start + lax.broadcasted_iota(jnp.int32, (1, size), 1)
            s = jnp.where(kpos <= t4, s, NEG)
        m_new = jnp.maximum(m, jnp.max(s, axis=1, keepdims=True))
        pr = jnp.exp2(s - m_new).astype(BF16)
        acc = jnp.exp2(m - m_new) * acc + _dot(pr, vs_ref[rows, :])
        return m_new, acc

    first = i * qn
    nbig = first // tb
    diag = (first // tk) * tk
    nsmall = (diag - nbig * tb) // tk
    init = (jnp.full((R, 1), NEG, F32), jnp.zeros((R, LANES), F32))
    carry = lax.fori_loop(0, nbig, lambda j, c: step(pl.multiple_of(j * tb, tb), tb, c, False), init)
    carry = lax.fori_loop(0, nsmall, lambda j, c: step(pl.multiple_of(nbig * tb, tk), tk, c, False), carry)
    m, acc = step(pl.multiple_of(diag, tk), tk, carry, True)
    o_s = acc[:, :HEAD_DIM] / acc[:, HEAD_DIM:HEAD_DIM + 1]

    wrows = pl.ds(pl.multiple_of(jnp.maximum(first + qn - wspan, 0), qn), wspan)
    s = _dot_nt(q4p, kw_ref[wrows, :])
    kpos = jnp.maximum(first + qn - wspan, 0) + lax.broadcasted_iota(jnp.int32, (1, wspan), 1)
    wmask = lax.bitcast_convert_type(t4 - kpos, jnp.uint32) < jnp.uint32(WINDOW)
    s = jnp.where(wmask, s, NEG)
    m = jnp.max(s, axis=1, keepdims=True)
    acc = _dot(jnp.exp2(s - m).astype(BF16), vw_ref[wrows, :])
    o_w = acc[:, :HEAD_DIM] / acc[:, HEAD_DIM:HEAD_DIM + 1]

    gates = _sigmoid(g_ref[0, 0])
    oc = oc_ref[...]
    outs = []
    for h in range(NSA_HPG):
        o_h = (gates[:, 3 * h:3 * h + 1] * oc[:, h * HEAD_DIM:(h + 1) * HEAD_DIM]
               + gates[:, 3 * h + 1:3 * h + 2] * o_s[h * qn:(h + 1) * qn]
               + gates[:, 3 * h + 2:3 * h + 3] * o_w[h * qn:(h + 1) * qn])
        outs.append(o_h)
    o_ref[...] = jnp.concatenate(outs, axis=1).astype(o_ref.dtype)


def _nsa_main(q, bias, ksp, onehot, vsa, kwp, vwa, oc, gates, B, S, qn=256, tk=512):
    T = B * S
    nq = S // qn
    tk = min(tk, S // 2)
    wspan = min(WINDOW + qn, S)
    GW = NSA_HPG * HEAD_DIM
    res = pl.BlockSpec((S, LANES), lambda b, g, i: (b, g))
    return pl.pallas_call(
        functools.partial(_nsa_main_kernel, qn=qn, tk=tk, wspan=wspan), grid=(B, NSA_GROUPS, nq),
        in_specs=[pl.BlockSpec((qn, GW), lambda b, g, i: (b * nq + i, g)),
                  pl.BlockSpec((1, 1, qn, LANES), lambda b, g, i: (b, g, i, 0)),
                  res, pl.BlockSpec((S, LANES), lambda b, g, i: (0, 0)), res, res, res,
                  pl.BlockSpec((qn, GW), lambda b, g, i: (b * nq + i, g)),
                  pl.BlockSpec((1, 1, qn, LANES), lambda b, g, i: (b, g, i, 0))],
        out_specs=pl.BlockSpec((qn, GW), lambda b, g, i: (b * nq + i, g)),
        out_shape=jax.ShapeDtypeStruct((T, NSA_HEADS * HEAD_DIM), BF16),
        compiler_params=_cparams("parallel", "parallel", "arbitrary"), name="nsa_main",
    )(q, bias, ksp, onehot, vsa, kwp, vwa, oc, gates)


def _lru_kernel(gx_ref, cw_ref, cb_ref, wab_ref, bab_ref, lam_ref, o_ref, h_ref, tail_ref, a_ref, b_ref, *, ts):
    n = pl.program_id(0)
    Bb = gx_ref.shape[1]
    W = gx_ref.shape[2] // 2

    @pl.when(n == 0)
    def _():
        h_ref[...] = jnp.zeros_like(h_ref)
        tail_ref[...] = jnp.zeros_like(tail_ref)

    xin = gx_ref[:, :, W:]
    xp = jnp.concatenate([tail_ref[...], xin], axis=0)
    cw = cw_ref[...]
    x = cb_ref[...].reshape(1, 1, W)
    for j in range(CONV_W):
        x = x + xp[j:j + ts] * cw[j:j + 1].reshape(1, 1, W)
    tail_ref[...] = xin[ts - (CONV_W - 1):ts]
    x2 = x.reshape(ts * Bb, W)
    pre = _dot(x2.astype(BF16), wab_ref[...]) + bab_ref[...]
    r = _sigmoid(pre[:, :W])
    ig = _sigmoid(pre[:, W:])
    log_a = (-RG_C * _softplus(-lam_ref[...])) * r
    a = jnp.exp(log_a)
    th = jnp.tanh(log_a)
    bb = jnp.sqrt(-2.0 * th / (1.0 - th)) * (ig * x2)
    a_ref[...] = a.reshape(ts, Bb, W)
    b_ref[...] = bb.reshape(ts, Bb, W)

    def scan(t, h):
        h = a_ref[t] * h + b_ref[t]
        b_ref[t] = h
        return h

    h_ref[...] = lax.fori_loop(0, ts, scan, h_ref[...], unroll=8)
    o_ref[...] = (b_ref[...] * _gelu_tanh(gx_ref[:, :, :W])).astype(o_ref.dtype)


def _lru(gx_t, conv_w, conv_b, wab, bab, lam, ts=128):
    S, Bb, W2 = gx_t.shape
    W = W2 // 2
    ts = min(ts, S)
    full = lambda shape: pl.BlockSpec(shape, lambda n: (0,) * len(shape))
    return pl.pallas_call(
        functools.partial(_lru_kernel, ts=ts), grid=(S // ts,),
        in_specs=[pl.BlockSpec((ts, Bb, W2), lambda n: (n, 0, 0)),
                  full((CONV_W, W)), full((1, W)), full((W, 2 * W)), full((1, 2 * W)), full((1, W))],
        out_specs=pl.BlockSpec((ts, Bb, W), lambda n: (n, 0, 0)),
        out_shape=jax.ShapeDtypeStruct((S, Bb, W), BF16),
        scratch_shapes=[pltpu.VMEM((Bb, W), F32), pltpu.VMEM((CONV_W - 1, Bb, W), F32),
                        pltpu.VMEM((ts, Bb, W), F32), pltpu.VMEM((ts, Bb, W), F32)],
        compiler_params=_cparams("arbitrary"), name="rg_lru",
    )(gx_t, conv_w, conv_b, wab, bab, lam)


def _even_mixer(h, B, S, norm_w, w_in, fox_bf, conv_w, a_log, dt_bias, gdn_norm_w, w_out):
    T, D = h.shape
    FW = FOX_HEADS * HEAD_DIM
    GW = GDN_HEADS * GDN_DIM
    o_ff = 3 * FW
    o_g = o_ff + FOX_HEADS
    o_gb = o_g + 4 * GW
    small = jnp.concatenate([w_in[:, o_ff:o_g], w_in[:, o_gb:o_gb + 2 * GDN_HEADS],
                             jnp.zeros((D, LANES - FOX_HEADS - 2 * GDN_HEADS), w_in.dtype)], axis=1)
    w = jnp.concatenate([w_in[:, :o_ff], w_in[:, o_g:o_gb], small], axis=1).astype(BF16)
    segs = [(0, 0, 0, FW, False, HEAD_DIM ** -0.5 * LOG2E, tuple((j, -1.0) for j in range(FOX_F_PIECES))),
            (1, 0, FW, FW, False, 1.0, ()),
            (2, 0, 2 * FW, FW, False, 1.0, ((0, 1.0),)),
            (3, 0, 3 * FW, 3 * GW, False, 1.0, None),
            (4, 0, 3 * FW + 3 * GW, GW, False, 1.0, None),
            (5, 0, 3 * FW + 4 * GW, LANES, False, 1.0, None)]
    fq, fk, fv, gqkv, gz, sm = _norm_proj(h, norm_w, w, segs,
                                          [(2 * FW, BF16), (2 * FW, BF16), (2 * FW, BF16),
                                           (3 * GW, F32), (GW, F32), (LANES, F32)])
    f3 = _fox_gate(sm, fox_bf, B, S)
    fox = _fox_attn(fq, fk, fv, f3, B, S)
    gdn = _gdn(gqkv, gz, sm, conv_w, a_log, dt_bias, gdn_norm_w, B, S)
    wo = w_out.astype(BF16)
    return fox, gdn, wo[:FW], wo[FW:], None


def _odd_mixer(h, B, S, rope_tabs, norm_w, w_in, k_pe, k_w1, k_w2, v_pe, v_w1, v_w2,
               conv_w, conv_b, wa, ba, wx, bx, lam, w_out):
    T, D = h.shape
    QW = NSA_HEADS * HEAD_DIM
    KW = NSA_GROUPS * HEAD_DIM
    G = NSA_GROUPS
    o_ng = QW + 6 * KW
    ngw = NSA_HEADS * 3
    w = jnp.concatenate([w_in[:, :o_ng], w_in[:, o_ng:o_ng + ngw], jnp.zeros((D, LANES - ngw), w_in.dtype),
                         w_in[:, o_ng + ngw:]], axis=1).astype(BF16)
    ones = ((0, 1.0),)
    segs = [(0, 0, 0, QW, True, HEAD_DIM ** -0.5 * LOG2E, None),
            (1, 0, QW, KW, True, 1.0, None),
            (1, KW, QW + KW, KW, False, 1.0, None),
            (2, 0, QW + 2 * KW, KW, True, 1.0, ()),
            (3, 0, QW + 3 * KW, KW, False, 1.0, ones),
            (4, 0, QW + 4 * KW, KW, True, 1.0, ()),
            (5, 0, QW + 5 * KW, KW, False, 1.0, ones),
            (6, 0, o_ng, LANES, False, 1.0, None),
            (7, 0, o_ng + LANES, 2 * LRU_W, False, 1.0, None)]
    q, kvc, ksp, vsa, kwp, vwa, ng, rgx_t = _norm_proj(
        h, norm_w, w, segs,
        [(QW, BF16), (2 * KW, F32), (2 * KW, BF16), (2 * KW, BF16), (2 * KW, BF16), (2 * KW, BF16),
         (LANES, F32), (2 * LRU_W, F32, (B, S))],
        rope_tabs=rope_tabs)
    nh = S // CMP_STRIDE
    xkv = kvc.reshape(B, S, 2 * G, HEAD_DIM).transpose(0, 2, 1, 3).reshape(B, 2 * G, nh, CMP_STRIDE * HEAD_DIM)
    pe = jnp.stack([k_pe.reshape(-1), v_pe.reshape(-1)]).astype(F32)
    pe = jnp.broadcast_to(pe[:, None, :], (2, SUBLANES, pe.shape[-1]))
    w1 = jnp.stack([k_w1, v_w1]).astype(BF16)
    w2 = jnp.stack([k_w2, v_w2]).astype(BF16)
    cmp = _compress(xkv, pe, w1, w2)
    n_cmp = (S - CMP_BLOCK) // CMP_STRIDE + 1
    ncp = -(-nh // LANES) * LANES
    if ncp != nh:
        cmp = jnp.pad(cmp, ((0, 0), (0, 0), (0, ncp - nh), (0, 0)))
    n_sel = S // SEL_BLOCK
    cs = np.arange(ncp) * CMP_STRIDE
    ss = np.arange(LANES) * SEL_BLOCK
    ov = ((cs[:, None] <= ss[None, :] + SEL_BLOCK - 1) & (cs[:, None] + CMP_BLOCK - 1 >= ss[None, :])
          & (np.arange(ncp)[:, None] < n_cmp) & (np.arange(LANES)[None, :] < n_sel))
    overlap = jnp.asarray(ov.astype(np.float32))
    oc, bias = _nsa_cmp(q, cmp, cmp, overlap, B, S)
    blk_of_pos = np.arange(S) // SEL_BLOCK
    onehot = jnp.asarray((blk_of_pos[:, None] == np.arange(LANES)[None, :]).astype(np.float32), dtype=BF16)
    gates = ng[:, :ngw].reshape(B, S, G, NSA_HPG * 3).transpose(0, 2, 1, 3)
    gates = jnp.pad(gates, ((0, 0), (0, 0), (0, 0), (0, LANES - NSA_HPG * 3)))
    nsa = _nsa_main(q, bias, ksp, onehot, vsa, kwp, vwa, oc, gates, B, S)
    nblk, bw, _ = wa.shape
    eye = jnp.eye(nblk, dtype=wa.dtype)
    dense = lambda wb: (eye[:, None, :, None] * wb[:, :, None, :]).reshape(nblk * bw, nblk * bw)
    wab = jnp.concatenate([dense(wa), dense(wx)], axis=1).astype(BF16)
    bab = jnp.concatenate([ba, bx]).reshape(1, 2 * LRU_W).astype(F32)
    lru_t = _lru(rgx_t.reshape(S, B, 2 * LRU_W), conv_w.astype(F32), conv_b.reshape(1, LRU_W).astype(F32), wab, bab,
                 lam.reshape(1, LRU_W).astype(F32))
    wo = w_out.astype(BF16)
    return nsa, lru_t.reshape(S, B * LRU_W), wo[:QW], wo[QW:], (B, S)


def kernel(x, p, positions, even_norm_mix, even_w_in, even_fox_bf, even_gdn_conv_w, even_gdn_a_log, even_gdn_dt_bias, even_gdn_norm_w, even_w_out, odd_norm_mix, odd_w_in, odd_cmp_k_pe, odd_cmp_k_w1, odd_cmp_k_w2, odd_cmp_v_pe, odd_cmp_v_w1, odd_cmp_v_w2, odd_rg_conv_w, odd_rg_conv_b, odd_rg_wa, odd_rg_ba, odd_rg_wx, odd_rg_bx, odd_rg_lambda, odd_w_out, mlp_norm, mlp_w_up, mlp_w_down, ple_norm, ple_w_gate, ple_w_proj, final_norm):
    B, S, D = x.shape
    T = B * S
    depth = p.shape[0]
    h = x.reshape(T, D)
    rope_tabs = _rope_tables(positions) if depth > 1 else None
    for i in range(depth):
        j = i // 2
        if i % 2 == 0:
            mix = _even_mixer(h, B, S, even_norm_mix[j], even_w_in[j], even_fox_bf[j], even_gdn_conv_w[j],
                              even_gdn_a_log[j], even_gdn_dt_bias[j], even_gdn_norm_w[j], even_w_out[j])
        else:
            mix = _odd_mixer(h, B, S, rope_tabs, odd_norm_mix[j], odd_w_in[j], odd_cmp_k_pe[j], odd_cmp_k_w1[j],
                             odd_cmp_k_w2[j], odd_cmp_v_pe[j], odd_cmp_v_w1[j], odd_cmp_v_w2[j], odd_rg_conv_w[j],
                             odd_rg_conv_b[j], odd_rg_wa[j], odd_rg_ba[j], odd_rg_wx[j], odd_rg_bx[j],
                             odd_rg_lambda[j], odd_w_out[j])
        h = _layer_tail(h, *mix, mlp_norm[i], mlp_w_up[i].astype(BF16), mlp_w_down[i].astype(BF16),
                        p[i].reshape(T, -1), ple_norm[i], ple_w_gate[i].astype(BF16), ple_w_proj[i].astype(BF16),
                        final_norm, final=(i == depth - 1))
    return h.reshape(B, S, D)
```

```python
import functools

import numpy as np
import jax
import jax.numpy as jnp
from jax import lax
from jax.experimental import pallas as pl
from jax.experimental.pallas import tpu as pltpu

F32 = jnp.float32
BF16 = jnp.bfloat16
HIGHEST = lax.Precision.HIGHEST

NORM_EPS = 1e-6
LOG2E = float(np.log2(np.e))
NEG = -1e30
REMOVED = -3e38
LANES = 128
SUBLANES = 8
VMEM_LIMIT = 56 * 1024 * 1024

HEAD_DIM = 64
ROT_DIM = 16
ROPE_THETA = 500000.0
FOX_HEADS = 8
FOX_F_PIECES = 3
GDN_HEADS = 4
GDN_DIM = 128
GDN_CHUNK = 64
NSA_HEADS = 8
NSA_GROUPS = 2
NSA_HPG = NSA_HEADS // NSA_GROUPS
CMP_BLOCK = 32
CMP_STRIDE = 16
CMP_HIDDEN = 128
SEL_BLOCK = 64
SEL_TOPK = 16
WINDOW = 512
LRU_W = 512
RG_C = 8.0
CONV_W = 4


def _cparams(*sem):
    return pltpu.CompilerParams(dimension_semantics=sem, vmem_limit_bytes=VMEM_LIMIT)


def _dot(a, b, **kw):
    return jnp.dot(a, b, preferred_element_type=F32, **kw)


def _dot_nt(a, b, **kw):
    return lax.dot_general(a, b, (((1,), (1,)), ((), ())), preferred_element_type=F32, **kw)


def _dot_tn(a, b, **kw):
    return lax.dot_general(a, b, (((0,), (0,)), ((), ())), preferred_element_type=F32, **kw)


def _softplus(x):
    return jnp.maximum(x, 0.0) + jnp.log1p(jnp.exp(-jnp.abs(x)))


def _sigmoid(x):
    return 1.0 / (1.0 + jnp.exp(-x))


def _silu(x):
    return x * _sigmoid(x)


def _gelu_tanh(x):
    return 0.5 * x * (1.0 + jnp.tanh(np.float32(np.sqrt(2.0 / np.pi)) * (x + 0.044715 * (x * x * x))))


def _rms(x, w):
    ms = jnp.mean(x * x, axis=-1, keepdims=True)
    return x * lax.rsqrt(ms + NORM_EPS) * w


def _apply_rope(y, c, s1, s2):
    outs = []
    for g in range(y.shape[1] // LANES):
        yg = y[:, g * LANES:(g + 1) * LANES]
        outs.append(yg * c + pltpu.roll(yg, LANES - ROT_DIM // 2, 1) * s1 + pltpu.roll(yg, ROT_DIM // 2, 1) * s2)
    return outs[0] if len(outs) == 1 else jnp.concatenate(outs, axis=1)


def _pad_heads(y, consts):
    tm = y.shape[0]
    lane = lax.broadcasted_iota(jnp.int32, (tm, LANES), 1)
    cst = jnp.zeros((tm, LANES), F32)
    for ln, val in consts:
        cst = jnp.where(lane == HEAD_DIM + ln, val, cst)
    low = lane < HEAD_DIM
    outs = []
    for g in range(y.shape[1] // LANES):
        yg = y[:, g * LANES:(g + 1) * LANES]
        outs.append(jnp.where(low, yg, cst))
        outs.append(jnp.where(low, pltpu.roll(yg, HEAD_DIM, 1), cst))
    return jnp.concatenate(outs, axis=1)


def _norm_proj_kernel(*refs, segs, n_out, rope):
    x_ref, nw_ref, w_ref = refs[:3]
    n_in = 3
    if rope:
        c_ref, s1_ref, s2_ref = refs[n_in:n_in + 3]
        n_in += 3
    out_refs = refs[n_in:n_in + n_out]
    xn = _rms(x_ref[...], nw_ref[...]).astype(BF16)
    for (oi, oc, wc, width, do_rope, scale, pad) in segs:
        y = _dot(xn, w_ref[:, wc:wc + width])
        if do_rope:
            y = _apply_rope(y, c_ref[...], s1_ref[...], s2_ref[...])
        if scale != 1.0:
            y = y * scale
        if pad is not None:
            y = _pad_heads(y, pad)
        out_refs[oi][:, oc:oc + y.shape[1]] = y.astype(out_refs[oi].dtype)


def _norm_proj(x, nw, w, segs, out_defs, rope_tabs=None, tm=512):
    T, D = x.shape
    N = w.shape[1]
    tm = min(tm, T)
    rope = rope_tabs is not None
    in_specs = [pl.BlockSpec((tm, D), lambda i: (i, 0)),
                pl.BlockSpec((1, D), lambda i: (0, 0)),
                pl.BlockSpec((D, N), lambda i: (0, 0))]
    args = [x, nw.reshape(1, D), w]
    if rope:
        in_specs += [pl.BlockSpec((tm, LANES), lambda i: (i, 0))] * 3
        args += list(rope_tabs)
    out_shape, out_specs = [], []
    for od in out_defs:
        wd, dt = od[0], od[1]
        if len(od) == 3:
            Bb, Ss = od[2]
            nst = Ss // tm
            out_shape.append(jax.ShapeDtypeStruct((Ss, Bb * wd), dt))
            out_specs.append(pl.BlockSpec((tm, wd), lambda i, nst=nst: (i % nst, i // nst)))
        else:
            out_shape.append(jax.ShapeDtypeStruct((T, wd), dt))
            out_specs.append(pl.BlockSpec((tm, wd), lambda i: (i, 0)))
    return pl.pallas_call(
        functools.partial(_norm_proj_kernel, segs=tuple(segs), n_out=len(out_defs), rope=rope),
        grid=(T // tm,), in_specs=in_specs, out_specs=out_specs, out_shape=out_shape,
        compiler_params=_cparams("parallel"), name="norm_proj",
    )(*args)


def _rope_table_kernel(pos_ref, f_ref, cos_ref, sin_ref, nsin_ref):
    ang = pos_ref[0] * f_ref[...]
    c = jnp.cos(ang)
    s = jnp.sin(ang)
    cos_ref[0] = c
    sin_ref[0] = s
    nsin_ref[0] = -s


def _rope_tables(positions):
    B, S = positions.shape
    half = ROT_DIM // 2
    inv_freq = ROPE_THETA ** (-jnp.arange(half, dtype=F32) * (2.0 / ROT_DIM))
    pos = positions.astype(F32).reshape(B, 1, S)
    sh = jax.ShapeDtypeStruct((B, half, S), F32)
    spec = pl.BlockSpec((1, half, S), lambda b: (b, 0, 0))
    cos, sin, nsin = pl.pallas_call(
        _rope_table_kernel, grid=(B,),
        in_specs=[pl.BlockSpec((1, 1, S), lambda b: (b, 0, 0)), pl.BlockSpec((half, 1), lambda b: (0, 0))],
        out_specs=[spec, spec, spec], out_shape=[sh, sh, sh],
        compiler_params=_cparams("parallel"), name="rope_tables",
    )(pos, inv_freq.reshape(half, 1))
    tr = lambda t: t.transpose(0, 2, 1).reshape(B * S, half)
    cos, sin, nsin = tr(cos), tr(sin), tr(nsin)
    T = B * S
    ones = jnp.ones((T, HEAD_DIM - ROT_DIM), F32)
    zeros = jnp.zeros((T, HEAD_DIM - ROT_DIM), F32)
    z8 = jnp.zeros((T, half), F32)
    c = jnp.tile(jnp.concatenate([cos, cos, ones], axis=1), (1, LANES // HEAD_DIM))
    s1 = jnp.tile(jnp.concatenate([nsin, z8, zeros], axis=1), (1, LANES // HEAD_DIM))
    s2 = jnp.tile(jnp.concatenate([z8, sin, zeros], axis=1), (1, LANES // HEAD_DIM))
    return c, s1, s2


def _tail_kernel(h_ref, a1_ref, a2_ref, w1_ref, w2_ref, nwm_ref, wu_ref, wd_ref, p_ref, nwp_ref, wg_ref, wp_ref,
                 fw_ref, o_ref, xn_ref, acc_ref, *, final):
    f = pl.program_id(1)

    @pl.when(f == 0)
    def _():
        hm = h_ref[...] + _dot(a1_ref[...], w1_ref[...]) + _dot(a2_ref[...], w2_ref[...])
        acc_ref[...] = hm
        xn_ref[...] = _rms(hm, nwm_ref[...]).astype(BF16)

    u = jnp.maximum(_dot(xn_ref[...], wu_ref[...]), 0.0)
    acc_ref[...] += _dot((u * u).astype(BF16), wd_ref[...])

    @pl.when(f == pl.num_programs(1) - 1)
    def _():
        h2 = acc_ref[...]
        gate = _sigmoid(_dot(_rms(h2, nwp_ref[...]).astype(BF16), wg_ref[...]))
        y = h2 + gate * _dot(p_ref[...].astype(BF16), wp_ref[...])
        if final:
            y = _rms(y, fw_ref[...])
        o_ref[...] = y


def _layer_tail(h, a1, a2, w1, w2, a2_time_major, nwm, wu, wd, p, nwp, wg, wp, fw, final, tm=512, tf=1024):
    T, D = h.shape
    FF = wu.shape[1]
    P = p.shape[1]
    tm = min(tm, T)
    K1, K2 = w1.shape[0], w2.shape[0]
    if a2_time_major is None:
        a2_spec = pl.BlockSpec((tm, K2), lambda i, f: (i, 0))
    else:
        nst = a2_time_major[1] // tm
        a2_spec = pl.BlockSpec((tm, K2), lambda i, f: (i % nst, i // nst))
    row = lambda w: pl.BlockSpec((tm, w), lambda i, f: (i, 0))
    full = lambda r, c: pl.BlockSpec((r, c), lambda i, f: (0, 0))
    return pl.pallas_call(
        functools.partial(_tail_kernel, final=final), grid=(T // tm, FF // tf),
        in_specs=[row(D), row(K1), a2_spec, full(K1, D), full(K2, D), full(1, D),
                  pl.BlockSpec((D, tf), lambda i, f: (0, f)), pl.BlockSpec((tf, D), lambda i, f: (f, 0)),
                  row(P), full(1, D), full(D, D), full(P, D), full(1, D)],
        out_specs=row(D),
        out_shape=jax.ShapeDtypeStruct((T, D), F32),
        scratch_shapes=[pltpu.VMEM((tm, D), BF16), pltpu.VMEM((tm, D), F32)],
        compiler_params=_cparams("parallel", "arbitrary"), name="layer_tail",
    )(h, a1, a2, w1, w2, nwm.reshape(1, D), wu, wd, p, nwp.reshape(1, D), wg, wp, fw.reshape(1, D))


def _fox_gate_kernel(x_ref, b_ref, o_ref, carry_ref, *, ts):
    @pl.when(pl.program_id(1) == 0)
    def _():
        carry_ref[...] = jnp.zeros_like(carry_ref)

    r = lax.broadcasted_iota(jnp.int32, (LANES, LANES), 0)
    c = lax.broadcasted_iota(jnp.int32, (LANES, LANES), 1)
    lower = (r >= c).astype(F32)
    lane = lax.broadcasted_iota(jnp.int32, (LANES, LANES), 1)

    nblk = ts // LANES
    within = []
    for n in range(nblk):
        x = x_ref[n * LANES:(n + 1) * LANES, :] + b_ref[...]
        logf = jnp.minimum(x, 0.0) - jnp.log1p(jnp.exp(-jnp.abs(x)))
        within.append(_dot(lower, logf, precision=HIGHEST))
    carry = carry_ref[0:1, :]
    for n in range(nblk):
        cum = within[n] + carry
        carry = cum[LANES - 1:LANES, :]
        f = cum * LOG2E
        hi = f.astype(BF16).astype(F32)
        mid = (f - hi).astype(BF16).astype(F32)
        lo = f - hi - mid
        for h in range(FOX_HEADS):
            v = jnp.where(lane == HEAD_DIM, hi[:, h:h + 1],
                          jnp.where(lane == HEAD_DIM + 1, mid[:, h:h + 1],
                                    jnp.where(lane == HEAD_DIM + 2, lo[:, h:h + 1], 0.0)))
            o_ref[n * LANES:(n + 1) * LANES, h * LANES:(h + 1) * LANES] = v.astype(o_ref.dtype)
    carry_ref[0:1, :] = carry


def _fox_gate(sm, bias, B, S, ts=1024):
    T = B * S
    ts = min(ts, S)
    ns = S // ts
    b = jnp.zeros((1, LANES), F32).at[0, :FOX_HEADS].set(bias.astype(F32))
    return pl.pallas_call(
        functools.partial(_fox_gate_kernel, ts=ts), grid=(B, ns),
        in_specs=[pl.BlockSpec((ts, LANES), lambda bb, n: (bb * ns + n, 0)),
                  pl.BlockSpec((1, LANES), lambda bb, n: (0, 0))],
        out_specs=pl.BlockSpec((ts, FOX_HEADS * LANES), lambda bb, n: (bb * ns + n, 0)),
        out_shape=jax.ShapeDtypeStruct((T, FOX_HEADS * LANES), BF16),
        scratch_shapes=[pltpu.VMEM((SUBLANES, LANES), F32)],
        compiler_params=_cparams("parallel", "arbitrary"), name="fox_gate",
    )(sm, b)


def _fox_attn_kernel(q_ref, k_ref, v_ref, f_ref, o_ref, *, tq, tk, dk, hp):
    i = pl.program_id(2)
    qs = [q_ref[:, hh * LANES:(hh + 1) * LANES] for hh in range(hp)]
    nsub = tq // tk

    def tile(start, size, r0, carry, masked):
        rows = pl.ds(start, size)
        ss = [_dot_nt(qs[hh][r0:], k_ref[rows, hh * LANES:(hh + 1) * LANES] + f_ref[rows, hh * LANES:(hh + 1) * LANES])
              for hh in range(hp)]
        out = []
        for hh in range(hp):
            m, acc = carry[hh]
            s = ss[hh]
            if masked:
                r = lax.broadcasted_iota(jnp.int32, (tq - r0, size), 0)
                c = lax.broadcasted_iota(jnp.int32, (tq - r0, size), 1)
                s = jnp.where(c <= r, s, NEG)
            m_new = jnp.maximum(m[r0:], jnp.max(s, axis=1, keepdims=True))
            pr = jnp.exp2(s - m_new).astype(BF16)
            acc_new = jnp.exp2(m[r0:] - m_new) * acc[r0:] + _dot(pr, v_ref[rows, hh * LANES:(hh + 1) * LANES])
            if r0:
                m_new = jnp.concatenate([m[:r0], m_new], axis=0)
                acc_new = jnp.concatenate([acc[:r0], acc_new], axis=0)
            out.append((m_new, acc_new))
        return tuple(out)

    init = tuple((jnp.full((tq, 1), NEG, F32), jnp.zeros((tq, LANES), F32)) for _ in range(hp))
    carry = lax.fori_loop(0, i * nsub, lambda j, c: tile(pl.multiple_of(j * tk, tk), tk, 0, c, False), init)
    for d in range(tq // dk):
        carry = tile(pl.multiple_of(i * tq + d * dk, dk), dk, d * dk, carry, True)
    o_ref[...] = jnp.concatenate([acc[:, :HEAD_DIM] / acc[:, HEAD_DIM:HEAD_DIM + 1] for (_, acc) in carry],
                                 axis=1).astype(o_ref.dtype)


def _fox_attn(q, k, v, f3, B, S, tq=1024, tk=1024, dk=512, hp=2):
    T = B * S
    tq = min(tq, S)
    tk = min(tk, tq)
    dk = min(dk, tq)
    nq = S // tq
    bw = hp * LANES
    ng = FOX_HEADS // hp
    res = pl.BlockSpec((S, bw), lambda b, p, i: (b, p))
    return pl.pallas_call(
        functools.partial(_fox_attn_kernel, tq=tq, tk=tk, dk=dk, hp=hp), grid=(B, ng, nq),
        in_specs=[pl.BlockSpec((tq, bw), lambda b, p, i: (b * nq + i, p)), res, res, res],
        out_specs=pl.BlockSpec((tq, hp * HEAD_DIM), lambda b, p, i: (b * nq + i, p)),
        out_shape=jax.ShapeDtypeStruct((T, FOX_HEADS * HEAD_DIM), BF16),
        compiler_params=_cparams("parallel", "parallel", "arbitrary"), name="fox_attn",
    )(q, k, v, f3)


def _cumsum_rows(x):
    n = x.shape[0]
    row = lax.broadcasted_iota(jnp.int32, x.shape, 0)
    d = 1
    while d < n:
        x = x + jnp.where(row >= d, pltpu.roll(x, d, 0), 0.0)
        d *= 2
    return x


def _gdn_kernel(x_ref, z_ref, sm_ref, cw_ref, par_ref, nw_ref, o_ref, state_ref, xp_ref, *, cps):
    C = GDN_CHUNK
    D = GDN_DIM
    W = GDN_HEADS * D
    R = cps * C
    n = pl.program_id(1)

    @pl.when(n == 0)
    def _():
        state_ref[...] = jnp.zeros_like(state_ref)
        xp_ref[0:SUBLANES, :] = jnp.zeros((SUBLANES, 3 * W), F32)

    x = x_ref[...]
    xp_ref[SUBLANES:SUBLANES + R, :] = x
    cw = cw_ref[...]
    y = x * cw[CONV_W - 1:CONV_W]
    for j in range(CONV_W - 1):
        off = SUBLANES - (CONV_W - 1) + j
        y = y + xp_ref[off:off + R, :] * cw[j:j + 1]
    xp_ref[0:SUBLANES, :] = x[R - SUBLANES:R]
    y = _silu(y)

    sm = sm_ref[...]
    beta_all = _sigmoid(sm)
    g_all = -jnp.exp(par_ref[0:1, :]) * _softplus(sm + par_ref[1:2, :])
    ri = lax.broadcasted_iota(jnp.int32, (C, C), 0)
    ci = lax.broadcasted_iota(jnp.int32, (C, C), 1)
    incl = ri >= ci
    strict = ri > ci

    chains = [(c, h) for c in range(cps) for h in range(GDN_HEADS)]
    gcs = [_cumsum_rows(g_all[c * C:(c + 1) * C]) for c in range(cps)]
    gcts = [gc.T for gc in gcs]
    pre = {}
    for (c, h) in chains:
        r0 = c * C
        q = y[r0:r0 + C, h * D:(h + 1) * D]
        k = y[r0:r0 + C, W + h * D:W + (h + 1) * D]
        v = y[r0:r0 + C, 2 * W + h * D:2 * W + (h + 1) * D]
        qn = q * lax.rsqrt(jnp.sum(q * q, axis=-1, keepdims=True) + 1e-6) * (D ** -0.5)
        kn = k * lax.rsqrt(jnp.sum(k * k, axis=-1, keepdims=True) + 1e-6)
        beta = beta_all[r0:r0 + C, 8 + h:9 + h]
        gcol = gcs[c][:, 12 + h:13 + h]
        grow = gcts[c][12 + h:13 + h, :]
        decay = jnp.where(incl, jnp.exp(jnp.where(incl, gcol - grow, 0.0)), 0.0)
        kb = kn * beta
        g_last = gcol[C - 1:C, :]
        pre[c, h] = dict(qn16=qn.astype(BF16), kn16=kn.astype(BF16), kb16=kb.astype(BF16), decay=decay,
                         rhs=jnp.concatenate([v * beta, kb * jnp.exp(gcol)], axis=1),
                         qg16=(qn * jnp.exp(gcol)).astype(BF16),
                         kg16=(kn * jnp.exp(g_last - gcol)).astype(BF16), e_last=jnp.exp(g_last))
    kk = {ch: _dot_nt(pre[ch]["kb16"], pre[ch]["kn16"]) for ch in chains}
    qk = {ch: _dot_nt(pre[ch]["qn16"], pre[ch]["kn16"]) for ch in chains}
    P = {ch: -jnp.where(strict, kk[ch] * pre[ch]["decay"], 0.0) for ch in chains}
    X = dict(P)
    for _ in range(5):
        P = {ch: _dot(P[ch].astype(BF16), P[ch].astype(BF16)) for ch in chains}
        XP = {ch: _dot(X[ch].astype(BF16), P[ch].astype(BF16)) for ch in chains}
        X = {ch: X[ch] + P[ch] + XP[ch] for ch in chains}
    sol = {ch: pre[ch]["rhs"] + _dot(X[ch].astype(BF16), pre[ch]["rhs"].astype(BF16)) for ch in chains}
    attn16 = {ch: jnp.where(incl, qk[ch] * pre[ch]["decay"], 0.0).astype(BF16) for ch in chains}

    heads = range(GDN_HEADS)
    states = [state_ref[h] for h in heads]
    for c in range(cps):
        r0 = c * C
        st16 = [states[h].astype(BF16) for h in heads]
        ws = [_dot(sol[c, h][:, D:].astype(BF16), st16[h]) for h in heads]
        qs = [_dot(pre[c, h]["qg16"], st16[h]) for h in heads]
        v_new = [(sol[c, h][:, :D] - ws[h]).astype(BF16) for h in heads]
        av = [_dot(attn16[c, h], v_new[h]) for h in heads]
        kv = [_dot_tn(pre[c, h]["kg16"], v_new[h]) for h in heads]
        outs = []
        for h in heads:
            states[h] = states[h] * pre[c, h]["e_last"] + kv[h]
            o = _rms(qs[h] + av[h], nw_ref[...]) * _silu(z_ref[r0:r0 + C, h * D:(h + 1) * D])
            outs.append(o)
        o_ref[r0:r0 + C, :] = jnp.concatenate(outs, axis=1).astype(o_ref.dtype)
    for h in heads:
        state_ref[h] = states[h]


def _gdn(gqkv, gz, small, conv_w, a_log, dt_bias, norm_w, B, S, cps=2):
    T = B * S
    C = GDN_CHUNK
    R = cps * C
    N = S // R
    W = GDN_HEADS * GDN_DIM
    par = jnp.zeros((SUBLANES, LANES), F32)
    par = par.at[0, 12:16].set(a_log.astype(F32)).at[1, 12:16].set(dt_bias.astype(F32))
    return pl.pallas_call(
        functools.partial(_gdn_kernel, cps=cps), grid=(B, N),
        in_specs=[pl.BlockSpec((R, 3 * W), lambda b, n: (b * N + n, 0)),
                  pl.BlockSpec((R, W), lambda b, n: (b * N + n, 0)),
                  pl.BlockSpec((R, LANES), lambda b, n: (b * N + n, 0)),
                  pl.BlockSpec((CONV_W, 3 * W), lambda b, n: (0, 0)),
                  pl.BlockSpec((SUBLANES, LANES), lambda b, n: (0, 0)),
                  pl.BlockSpec((1, GDN_DIM), lambda b, n: (0, 0))],
        out_specs=pl.BlockSpec((R, W), lambda b, n: (b * N + n, 0)),
        out_shape=jax.ShapeDtypeStruct((T, W), BF16),
        scratch_shapes=[pltpu.VMEM((GDN_HEADS, GDN_DIM, GDN_DIM), F32),
                        pltpu.VMEM((SUBLANES + R, 3 * W), F32)],
        compiler_params=_cparams("parallel", "arbitrary"), name="gdn",
    )(gqkv, gz, small, conv_w.astype(F32), par, norm_w.reshape(1, GDN_DIM).astype(F32))


def _compress_kernel(x_ref, pe_ref, w1_ref, wbd_ref, w2_ref, o_ref, *, nh):
    ab = jnp.zeros((nh, NSA_GROUPS * 2 * CMP_HIDDEN), F32)
    for l in range(CMP_STRIDE):
        ab = ab + _dot(x_ref[pl.ds(l, nh, stride=CMP_STRIDE), :].astype(BF16), wbd_ref[0, l])
    lane = lax.broadcasted_iota(jnp.int32, (nh, HEAD_DIM), 1)
    one_col = jnp.where(lane == 0, 1.0, 0.0)
    c = _dot(pe_ref[0].astype(BF16), w1_ref[0])
    for g in range(NSA_GROUPS):
        a = ab[:, 2 * g * CMP_HIDDEN:(2 * g + 1) * CMP_HIDDEN]
        b = ab[:, (2 * g + 1) * CMP_HIDDEN:(2 * g + 2) * CMP_HIDDEN]
        hid = _gelu_tanh(a + pltpu.roll(b, nh - 1, 0) + c[0:1])
        out = _dot(hid.astype(BF16), w2_ref[0])
        o_ref[0, g] = jnp.concatenate([out, one_col], axis=1).astype(o_ref.dtype)


def _compress(kc, vc, pe, w1, w2, B, S):
    nh = S // CMP_STRIDE
    G = NSA_GROUPS
    half = CMP_STRIDE * HEAD_DIM
    w1r = w1.reshape(2, 2, CMP_STRIDE, HEAD_DIM, CMP_HIDDEN)
    per_tok = jnp.concatenate([w1r[:, 0], w1r[:, 1]], axis=-1)
    eye = jnp.eye(G, dtype=w1.dtype)
    wbd = per_tok[:, :, None, :, None, :] * eye[None, None, :, None, :, None]
    wbd = wbd.reshape(2, CMP_STRIDE, G * HEAD_DIM, G * 2 * CMP_HIDDEN)
    x_spec = pl.BlockSpec((S, G * HEAD_DIM), lambda b, j: (b, 0))
    out = [pl.pallas_call(
        functools.partial(_compress_kernel, nh=nh), grid=(B, 1),
        in_specs=[x_spec, pl.BlockSpec((1, SUBLANES, 2 * half), lambda b, j, kv=kv: (kv, 0, 0)),
                  pl.BlockSpec((1, 2 * half, CMP_HIDDEN), lambda b, j, kv=kv: (kv, 0, 0)),
                  pl.BlockSpec((1, CMP_STRIDE, G * HEAD_DIM, G * 2 * CMP_HIDDEN), lambda b, j, kv=kv: (kv, 0, 0, 0)),
                  pl.BlockSpec((1, CMP_HIDDEN, HEAD_DIM), lambda b, j, kv=kv: (kv, 0, 0))],
        out_specs=pl.BlockSpec((1, G, nh, LANES), lambda b, j: (b, 0, 0, 0)),
        out_shape=jax.ShapeDtypeStruct((B, G, nh, LANES), BF16),
        compiler_params=_cparams("parallel", "arbitrary"), name="nsa_compress",
    )(x, pe, w1, wbd, w2) for kv, x in enumerate((kc, vc))]
    return out


def _stack_heads(qb):
    return jnp.concatenate([qb[:, h * HEAD_DIM:(h + 1) * HEAD_DIM] for h in range(NSA_HPG)], axis=0)


def _unstack_heads(o, tq):
    return jnp.concatenate([o[h * tq:(h + 1) * tq] for h in range(NSA_HPG)], axis=1)


def _nsa_cmp_kernel(q_ref, kc_ref, vc_ref, ov_ref, oc_ref, bias_ref, *, tq, n_sel, top_k):
    i = pl.program_id(1)
    G = NSA_GROUPS
    GW = NSA_HPG * HEAD_DIM
    ncp = kc_ref.shape[2]
    row = lax.broadcasted_iota(jnp.int32, (NSA_HPG * tq, 1), 0)
    t4 = i * tq + (row & (tq - 1))
    cmp_end = lax.broadcasted_iota(jnp.int32, (1, ncp), 1) * CMP_STRIDE + (CMP_BLOCK - 1)
    valid = cmp_end <= t4
    ss = [_dot_nt(_stack_heads(q_ref[:, g * GW:(g + 1) * GW]), kc_ref[0, g, :, :HEAD_DIM]) for g in range(G)]
    ps, accs = [], []
    for g in range(G):
        s = jnp.where(valid, ss[g], NEG)
        m = jnp.max(s, axis=1, keepdims=True)
        p = jnp.exp2(s - m)
        ps.append(p)
        accs.append(_dot(p.astype(BF16), vc_ref[0, g]))
    imps = []
    for g in range(G):
        inv = jnp.where(t4 >= CMP_BLOCK - 1, 1.0 / accs[g][:, HEAD_DIM:HEAD_DIM + 1], 0.0)
        oc_ref[:, g * GW:(g + 1) * GW] = _unstack_heads(accs[g][:, :HEAD_DIM] * inv, tq)
        p = ps[g] * inv
        psum = p[0:tq] + p[tq:2 * tq] + p[2 * tq:3 * tq] + p[3 * tq:4 * tq]
        imps.append(_dot(psum, ov_ref[...], precision=HIGHEST))
    blk = lax.broadcasted_iota(jnp.int32, (1, LANES), 1)
    t = i * tq + lax.broadcasted_iota(jnp.int32, (tq, 1), 0)
    cur = t >> (SEL_BLOCK.bit_length() - 1)
    forced = (blk == 0) | (blk == cur) | (blk == cur - 1)
    future = blk * SEL_BLOCK > t
    blk_t = lax.broadcasted_iota(jnp.int32, (LANES, tq), 0).astype(F32)
    for g in range(G):
        imp = jnp.where(future, NEG, jnp.where(forced, -NEG, imps[g]))
        imp = jnp.where(blk < n_sel, imp, REMOVED)
        imp_t = imp.T
        for _ in range(top_k):
            mx = jnp.max(imp_t, axis=0, keepdims=True)
            first = jnp.min(jnp.where(imp_t == mx, blk_t, float(LANES)), axis=0, keepdims=True)
            imp_t = jnp.where(blk_t == first, REMOVED, imp_t)
        sel = jnp.where(imp_t == REMOVED, 1.0, 0.0).T
        bias_ref[0, g] = jnp.where((sel > 0.0) & jnp.logical_not(future), 0.0, NEG).astype(bias_ref.dtype)


def _nsa_cmp(q, kc, vc, overlap, B, S, tq=256):
    T = B * S
    tq = min(tq, S)
    nq = S // tq
    ncp = kc.shape[2]
    n_sel = S // SEL_BLOCK
    QW = NSA_HEADS * HEAD_DIM
    G = NSA_GROUPS
    return pl.pallas_call(
        functools.partial(_nsa_cmp_kernel, tq=tq, n_sel=n_sel, top_k=min(SEL_TOPK, n_sel)),
        grid=(B, nq),
        in_specs=[pl.BlockSpec((tq, QW), lambda b, i: (b * nq + i, 0)),
                  pl.BlockSpec((1, G, ncp, LANES), lambda b, i: (b, 0, 0, 0)),
                  pl.BlockSpec((1, G, ncp, LANES), lambda b, i: (b, 0, 0, 0)),
                  pl.BlockSpec((ncp, LANES), lambda b, i: (0, 0))],
        out_specs=[pl.BlockSpec((tq, QW), lambda b, i: (b * nq + i, 0)),
                   pl.BlockSpec((1, G, tq, LANES), lambda b, i: (b, 0, i, 0))],
        out_shape=[jax.ShapeDtypeStruct((T, QW), F32),
                   jax.ShapeDtypeStruct((B, G, S, LANES), BF16)],
        compiler_params=_cparams("parallel", "parallel"), name="nsa_cmp",
    )(q, kc, vc, overlap)


def _nsa_main_kernel(q_ref, bias_ref, ks_ref, oh_ref, vs_ref, kw_ref, vw_ref, oc_ref, g_ref, o_ref, *, qn, tk, wspan):
    R = NSA_HPG * qn
    tb = 2 * tk
    i = pl.program_id(2)
    q4 = _stack_heads(q_ref[...])
    q4p = jnp.concatenate([q4, jnp.zeros_like(q4)], axis=1)
    b4 = jnp.concatenate([bias_ref[0, 0]] * NSA_HPG, axis=0)
    qa = jnp.concatenate([q4p, b4], axis=1)
    t4 = i * qn + (lax.broadcasted_iota(jnp.int32, (R, 1), 0) & (qn - 1))

    def step(start, size, carry, masked):
        m, acc = carry
        rows = pl.ds(start, size)
        ka = jnp.concatenate([ks_ref[rows, :], oh_ref[rows, :]], axis=1)
        s = _dot_nt(qa, ka)
        if masked:
            kpos = start + lax.broadcasted_iota(jnp.int32, (1, size), 1)
            s = jnp.where(kpos <= t4, s, NEG)
        m_new = jnp.maximum(m, jnp.max(s, axis=1, keepdims=True))
        pr = jnp.exp2(s - m_new).astype(BF16)
        acc = jnp.exp2(m - m_new) * acc + _dot(pr, vs_ref[rows, :])
        return m_new, acc

    first = i * qn
    nbig = first // tb
    diag = (first // tk) * tk
    nsmall = (diag - nbig * tb) // tk
    init = (jnp.full((R, 1), NEG, F32), jnp.zeros((R, LANES), F32))
    carry = lax.fori_loop(0, nbig, lambda j, c: step(pl.multiple_of(j * tb, tb), tb, c, False), init)
    carry = lax.fori_loop(0, nsmall, lambda j, c: step(pl.multiple_of(nbig * tb, tk), tk, c, False), carry)
    m, acc = step(pl.multiple_of(diag, tk), tk, carry, True)
    o_s = acc[:, :HEAD_DIM] / acc[:, HEAD_DIM:HEAD_DIM + 1]

    wrows = pl.ds(pl.multiple_of(jnp.maximum(first + qn - wspan, 0), qn), wspan)
    s = _dot_nt(q4p, kw_ref[wrows, :])
    kpos = jnp.maximum(first + qn - wspan, 0) + lax.broadcasted_iota(jnp.int32, (1, wspan), 1)
    wmask = lax.bitcast_convert_type(t4 - kpos, jnp.uint32) < jnp.uint32(WINDOW)
    s = jnp.where(wmask, s, NEG)
    m = jnp.max(s, axis=1, keepdims=True)
    acc = _dot(jnp.exp2(s - m).astype(BF16), vw_ref[wrows, :])
    o_w = acc[:, :HEAD_DIM] / acc[:, HEAD_DIM:HEAD_DIM + 1]

    gates = _sigmoid(g_ref[0, 0])
    oc = oc_ref[...]
    outs = []
    for h in range(NSA_HPG):
        o_h = (gates[:, 3 * h:3 * h + 1] * oc[:, h * HEAD_DIM:(h + 1) * HEAD_DIM]
               + gates[:, 3 * h + 1:3 * h + 2] * o_s[h * qn:(h + 1) * qn]
               + gates[:, 3 * h + 2:3 * h + 3] * o_w[h * qn:(h + 1) * qn])
        outs.append(o_h)
    o_ref[...] = jnp.concatenate(outs, axis=1).astype(o_ref.dtype)


def _nsa_main(q, bias, ksp, onehot, vsa, kwp, vwa, oc, gates, B, S, qn=256, tk=512):
    T = B * S
    nq = S // qn
    tk = min(tk, S // 2)
    wspan = min(WINDOW + qn, S)
    GW = NSA_HPG * HEAD_DIM
    res = pl.BlockSpec((S, LANES), lambda b, g, i: (b, g))
    return pl.pallas_call(
        functools.partial(_nsa_main_kernel, qn=qn, tk=tk, wspan=wspan), grid=(B, NSA_GROUPS, nq),
        in_specs=[pl.BlockSpec((qn, GW), lambda b, g, i: (b * nq + i, g)),
                  pl.BlockSpec((1, 1, qn, LANES), lambda b, g, i: (b, g, i, 0)),
                  res, pl.BlockSpec((S, LANES), lambda b, g, i: (0, 0)), res, res, res,
                  pl.BlockSpec((qn, GW), lambda b, g, i: (b * nq + i, g)),
                  pl.BlockSpec((1, 1, qn, LANES), lambda b, g, i: (b, g, i, 0))],
        out_specs=pl.BlockSpec((qn, GW), lambda b, g, i: (b * nq + i, g)),
        out_shape=jax.ShapeDtypeStruct((T, NSA_HEADS * HEAD_DIM), BF16),
        compiler_params=_cparams("parallel", "parallel", "arbitrary"), name="nsa_main",
    )(q, bias, ksp, onehot, vsa, kwp, vwa, oc, gates)


def _lru_kernel(gx_ref, cw_ref, cb_ref, wab_ref, bab_ref, lam_ref, o_ref, h_ref, tail_ref, a_ref, b_ref, *, ts):
    n = pl.program_id(0)
    Bb = gx_ref.shape[1]
    W = gx_ref.shape[2] // 2

    @pl.when(n == 0)
    def _():
        h_ref[...] = jnp.zeros_like(h_ref)
        tail_ref[...] = jnp.zeros_like(tail_ref)

    xin = gx_ref[:, :, W:]
    xp = jnp.concatenate([tail_ref[...], xin], axis=0)
    cw = cw_ref[...]
    x = cb_ref[...].reshape(1, 1, W)
    for j in range(CONV_W):
        x = x + xp[j:j + ts] * cw[j:j + 1].reshape(1, 1, W)
    tail_ref[...] = xin[ts - (CONV_W - 1):ts]
    x2 = x.reshape(ts * Bb, W)
    pre = _dot(x2.astype(BF16), wab_ref[...]) + bab_ref[...]
    r = _sigmoid(pre[:, :W])
    ig = _sigmoid(pre[:, W:])
    log_a = (-RG_C * _softplus(-lam_ref[...])) * r
    a = jnp.exp(log_a)
    th = jnp.tanh(log_a)
    bb = jnp.sqrt(-2.0 * th / (1.0 - th)) * (ig * x2)
    a_ref[...] = a.reshape(ts, Bb, W)
    b_ref[...] = bb.reshape(ts, Bb, W)

    def scan(t, h):
        h = a_ref[t] * h + b_ref[t]
        b_ref[t] = h
        return h

    h_ref[...] = lax.fori_loop(0, ts, scan, h_ref[...], unroll=8)
    o_ref[...] = (b_ref[...] * _gelu_tanh(gx_ref[:, :, :W])).astype(o_ref.dtype)


def _lru(gx_t, conv_w, conv_b, wab, bab, lam, ts=128):
    S, Bb, W2 = gx_t.shape
    W = W2 // 2
    ts = min(ts, S)
    full = lambda shape: pl.BlockSpec(shape, lambda n: (0,) * len(shape))
    return pl.pallas_call(
        functools.partial(_lru_kernel, ts=ts), grid=(S // ts,),
        in_specs=[pl.BlockSpec((ts, Bb, W2), lambda n: (n, 0, 0)),
                  full((CONV_W, W)), full((1, W)), full((W, 2 * W)), full((1, 2 * W)), full((1, W))],
        out_specs=pl.BlockSpec((ts, Bb, W), lambda n: (n, 0, 0)),
        out_shape=jax.ShapeDtypeStruct((S, Bb, W), BF16),
        scratch_shapes=[pltpu.VMEM((Bb, W), F32), pltpu.VMEM((CONV_W - 1, Bb, W), F32),
                        pltpu.VMEM((ts, Bb, W), F32), pltpu.VMEM((ts, Bb, W), F32)],
        compiler_params=_cparams("arbitrary"), name="rg_lru",
    )(gx_t, conv_w, conv_b, wab, bab, lam)


def _even_mixer(h, B, S, norm_w, w_in, fox_bf, conv_w, a_log, dt_bias, gdn_norm_w, w_out):
    T, D = h.shape
    FW = FOX_HEADS * HEAD_DIM
    GW = GDN_HEADS * GDN_DIM
    o_ff = 3 * FW
    o_g = o_ff + FOX_HEADS
    o_gb = o_g + 4 * GW
    small = jnp.concatenate([w_in[:, o_ff:o_g], w_in[:, o_gb:o_gb + 2 * GDN_HEADS],
                             jnp.zeros((D, LANES - FOX_HEADS - 2 * GDN_HEADS), w_in.dtype)], axis=1)
    w = jnp.concatenate([w_in[:, :o_ff], w_in[:, o_g:o_gb], small], axis=1).astype(BF16)
    segs = [(0, 0, 0, FW, False, HEAD_DIM ** -0.5 * LOG2E, tuple((j, -1.0) for j in range(FOX_F_PIECES))),
            (1, 0, FW, FW, False, 1.0, ()),
            (2, 0, 2 * FW, FW, False, 1.0, ((0, 1.0),)),
            (3, 0, 3 * FW, 3 * GW, False, 1.0, None),
            (4, 0, 3 * FW + 3 * GW, GW, False, 1.0, None),
            (5, 0, 3 * FW + 4 * GW, LANES, False, 1.0, None)]
    fq, fk, fv, gqkv, gz, sm = _norm_proj(h, norm_w, w, segs,
                                          [(2 * FW, BF16), (2 * FW, BF16), (2 * FW, BF16),
                                           (3 * GW, F32), (GW, F32), (LANES, F32)])
    f3 = _fox_gate(sm, fox_bf, B, S)
    fox = _fox_attn(fq, fk, fv, f3, B, S)
    gdn = _gdn(gqkv, gz, sm, conv_w, a_log, dt_bias, gdn_norm_w, B, S)
    wo = w_out.astype(BF16)
    return fox, gdn, wo[:FW], wo[FW:], None


def _odd_mixer(h, B, S, rope_tabs, norm_w, w_in, k_pe, k_w1, k_w2, v_pe, v_w1, v_w2,
               conv_w, conv_b, wa, ba, wx, bx, lam, w_out):
    T, D = h.shape
    QW = NSA_HEADS * HEAD_DIM
    KW = NSA_GROUPS * HEAD_DIM
    G = NSA_GROUPS
    o_ng = QW + 6 * KW
    ngw = NSA_HEADS * 3
    w = jnp.concatenate([w_in[:, :o_ng], w_in[:, o_ng:o_ng + ngw], jnp.zeros((D, LANES - ngw), w_in.dtype),
                         w_in[:, o_ng + ngw:]], axis=1).astype(BF16)
    ones = ((0, 1.0),)
    segs = [(0, 0, 0, QW, True, HEAD_DIM ** -0.5 * LOG2E, None),
            (1, 0, QW, KW, True, 1.0, None),
            (2, 0, QW + KW, KW, False, 1.0, None),
            (3, 0, QW + 2 * KW, KW, True, 1.0, ()),
            (4, 0, QW + 3 * KW, KW, False, 1.0, ones),
            (5, 0, QW + 4 * KW, KW, True, 1.0, ()),
            (6, 0, QW + 5 * KW, KW, False, 1.0, ones),
            (7, 0, o_ng, LANES, False, 1.0, None),
            (8, 0, o_ng + LANES, 2 * LRU_W, False, 1.0, None)]
    q, kc, vc, ksp, vsa, kwp, vwa, ng, rgx_t = _norm_proj(
        h, norm_w, w, segs,
        [(QW, BF16), (KW, F32), (KW, F32), (2 * KW, BF16), (2 * KW, BF16), (2 * KW, BF16), (2 * KW, BF16),
         (LANES, F32), (2 * LRU_W, F32, (B, S))],
        rope_tabs=rope_tabs)
    nh = S // CMP_STRIDE
    pe = jnp.stack([k_pe.reshape(-1), v_pe.reshape(-1)]).astype(F32)
    pe = jnp.broadcast_to(pe[:, None, :], (2, SUBLANES, pe.shape[-1]))
    w1 = jnp.stack([k_w1, v_w1]).astype(BF16)
    w2 = jnp.stack([k_w2, v_w2]).astype(BF16)
    kcmp, vcmp = _compress(kc, vc, pe, w1, w2, B, S)
    n_cmp = (S - CMP_BLOCK) // CMP_STRIDE + 1
    ncp = -(-nh // LANES) * LANES
    if ncp != nh:
        kcmp, vcmp = (jnp.pad(t, ((0, 0), (0, 0), (0, ncp - nh), (0, 0))) for t in (kcmp, vcmp))
    n_sel = S // SEL_BLOCK
    cs = np.arange(ncp) * CMP_STRIDE
    ss = np.arange(LANES) * SEL_BLOCK
    ov = ((cs[:, None] <= ss[None, :] + SEL_BLOCK - 1) & (cs[:, None] + CMP_BLOCK - 1 >= ss[None, :])
          & (np.arange(ncp)[:, None] < n_cmp) & (np.arange(LANES)[None, :] < n_sel))
    overlap = jnp.asarray(ov.astype(np.float32))
    oc, bias = _nsa_cmp(q, kcmp, vcmp, overlap, B, S)
    blk_of_pos = np.arange(S) // SEL_BLOCK
    onehot = jnp.asarray((blk_of_pos[:, None] == np.arange(LANES)[None, :]).astype(np.float32), dtype=BF16)
    gates = ng[:, :ngw].reshape(B, S, G, NSA_HPG * 3).transpose(0, 2, 1, 3)
    gates = jnp.pad(gates, ((0, 0), (0, 0), (0, 0), (0, LANES - NSA_HPG * 3)))
    nsa = _nsa_main(q, bias, ksp, onehot, vsa, kwp, vwa, oc, gates, B, S)
    nblk, bw, _ = wa.shape
    eye = jnp.eye(nblk, dtype=wa.dtype)
    dense = lambda wb: (eye[:, None, :, None] * wb[:, :, None, :]).reshape(nblk * bw, nblk * bw)
    wab = jnp.concatenate([dense(wa), dense(wx)], axis=1).astype(BF16)
    bab = jnp.concatenate([ba, bx]).reshape(1, 2 * LRU_W).astype(F32)
    lru_t = _lru(rgx_t.reshape(S, B, 2 * LRU_W), conv_w.astype(F32), conv_b.reshape(1, LRU_W).astype(F32), wab, bab,
                 lam.reshape(1, LRU_W).astype(F32))
    wo = w_out.astype(BF16)
    return nsa, lru_t.reshape(S, B * LRU_W), wo[:QW], wo[QW:], (B, S)


def kernel(x, p, positions, even_norm_mix, even_w_in, even_fox_bf, even_gdn_conv_w, even_gdn_a_log, even_gdn_dt_bias, even_gdn_norm_w, even_w_out, odd_norm_mix, odd_w_in, odd_cmp_k_pe, odd_cmp_k_w1, odd_cmp_k_w2, odd_cmp_v_pe, odd_cmp_v_w1, odd_cmp_v_w2, odd_rg_conv_w, odd_rg_conv_b, odd_rg_wa, odd_rg_ba, odd_rg_wx, odd_rg_bx, odd_rg_lambda, odd_w_out, mlp_norm, mlp_w_up, mlp_w_down, ple_norm, ple_w_gate, ple_w_proj, final_norm):
    B, S, D = x.shape
    T = B * S
    depth = p.shape[0]
    h = x.reshape(T, D)
    rope_tabs = _rope_tables(positions) if depth > 1 else None
    for i in range(depth):
        j = i // 2
        if i % 2 == 0:
            mix = _even_mixer(h, B, S, even_norm_mix[j], even_w_in[j], even_fox_bf[j], even_gdn_conv_w[j],
                              even_gdn_a_log[j], even_gdn_dt_bias[j], even_gdn_norm_w[j], even_w_out[j])
        else:
            mix = _odd_mixer(h, B, S, rope_tabs, odd_norm_mix[j], odd_w_in[j], odd_cmp_k_pe[j], odd_cmp_k_w1[j],
                             odd_cmp_k_w2[j], odd_cmp_v_pe[j], odd_cmp_v_w1[j], odd_cmp_v_w2[j], odd_rg_conv_w[j],
                             odd_rg_conv_b[j], odd_rg_wa[j], odd_rg_ba[j], odd_rg_wx[j], odd_rg_bx[j],
                             odd_rg_lambda[j], odd_w_out[j])
        h = _layer_tail(h, *mix, mlp_norm[i], mlp_w_up[i].astype(BF16), mlp_w_down[i].astype(BF16),
                        p[i].reshape(T, -1), ple_norm[i], ple_w_gate[i].astype(BF16), ple_w_proj[i].astype(BF16),
                        final_norm, final=(i == depth - 1))
    return h.reshape(B, S, D)
```

```python
import functools

import numpy as np
import jax
import jax.numpy as jnp
from jax import lax
from jax.experimental import pallas as pl
from jax.experimental.pallas import tpu as pltpu

F32 = jnp.float32
BF16 = jnp.bfloat16
HIGHEST = lax.Precision.HIGHEST

NORM_EPS = 1e-6
LOG2E = float(np.log2(np.e))
NEG = -1e30
REMOVED = -3e38
LANES = 128
SUBLANES = 8
VMEM_LIMIT = 56 * 1024 * 1024

HEAD_DIM = 64
ROT_DIM = 16
ROPE_THETA = 500000.0
FOX_HEADS = 8
FOX_F_PIECES = 3
GDN_HEADS = 4
GDN_DIM = 128
GDN_CHUNK = 64
NSA_HEADS = 8
NSA_GROUPS = 2
NSA_HPG = NSA_HEADS // NSA_GROUPS
CMP_BLOCK = 32
CMP_STRIDE = 16
CMP_HIDDEN = 128
SEL_BLOCK = 64
SEL_TOPK = 16
WINDOW = 512
LRU_W = 512
RG_C = 8.0
CONV_W = 4


def _cparams(*sem):
    return pltpu.CompilerParams(dimension_semantics=sem, vmem_limit_bytes=VMEM_LIMIT)


def _dot(a, b, **kw):
    return jnp.dot(a, b, preferred_element_type=F32, **kw)


def _dot_nt(a, b, **kw):
    return lax.dot_general(a, b, (((1,), (1,)), ((), ())), preferred_element_type=F32, **kw)


def _dot_tn(a, b, **kw):
    return lax.dot_general(a, b, (((0,), (0,)), ((), ())), preferred_element_type=F32, **kw)


def _softplus(x):
    return jnp.maximum(x, 0.0) + jnp.log1p(jnp.exp(-jnp.abs(x)))


def _sigmoid(x):
    return 1.0 / (1.0 + jnp.exp(-x))


def _silu(x):
    return x * _sigmoid(x)


def _gelu_tanh(x):
    return 0.5 * x * (1.0 + jnp.tanh(np.float32(np.sqrt(2.0 / np.pi)) * (x + 0.044715 * (x * x * x))))


def _rms(x, w):
    ms = jnp.mean(x * x, axis=-1, keepdims=True)
    return x * lax.rsqrt(ms + NORM_EPS) * w


def _apply_rope(y, c, s1, s2):
    outs = []
    for g in range(y.shape[1] // LANES):
        yg = y[:, g * LANES:(g + 1) * LANES]
        outs.append(yg * c + pltpu.roll(yg, LANES - ROT_DIM // 2, 1) * s1 + pltpu.roll(yg, ROT_DIM // 2, 1) * s2)
    return outs[0] if len(outs) == 1 else jnp.concatenate(outs, axis=1)


def _pad_heads(y, consts):
    tm = y.shape[0]
    lane = lax.broadcasted_iota(jnp.int32, (tm, LANES), 1)
    cst = jnp.zeros((tm, LANES), F32)
    for ln, val in consts:
        cst = jnp.where(lane == HEAD_DIM + ln, val, cst)
    low = lane < HEAD_DIM
    outs = []
    for g in range(y.shape[1] // LANES):
        yg = y[:, g * LANES:(g + 1) * LANES]
        outs.append(jnp.where(low, yg, cst))
        outs.append(jnp.where(low, pltpu.roll(yg, HEAD_DIM, 1), cst))
    return jnp.concatenate(outs, axis=1)


def _norm_proj_kernel(*refs, segs, n_out, rope):
    x_ref, nw_ref, w_ref = refs[:3]
    n_in = 3
    if rope:
        c_ref, s1_ref, s2_ref = refs[n_in:n_in + 3]
        n_in += 3
    out_refs = refs[n_in:n_in + n_out]
    xn = _rms(x_ref[...], nw_ref[...]).astype(BF16)
    for (oi, oc, wc, width, do_rope, scale, pad) in segs:
        y = _dot(xn, w_ref[:, wc:wc + width])
        if do_rope:
            y = _apply_rope(y, c_ref[...], s1_ref[...], s2_ref[...])
        if scale != 1.0:
            y = y * scale
        if pad is not None:
            y = _pad_heads(y, pad)
        out_refs[oi][:, oc:oc + y.shape[1]] = y.astype(out_refs[oi].dtype)


def _norm_proj(x, nw, w, segs, out_defs, rope_tabs=None, tm=512):
    T, D = x.shape
    N = w.shape[1]
    tm = min(tm, T)
    rope = rope_tabs is not None
    in_specs = [pl.BlockSpec((tm, D), lambda i: (i, 0)),
                pl.BlockSpec((1, D), lambda i: (0, 0)),
                pl.BlockSpec((D, N), lambda i: (0, 0))]
    args = [x, nw.reshape(1, D), w]
    if rope:
        in_specs += [pl.BlockSpec((tm, LANES), lambda i: (i, 0))] * 3
        args += list(rope_tabs)
    out_shape, out_specs = [], []
    for od in out_defs:
        wd, dt = od[0], od[1]
        if len(od) == 3:
            Bb, Ss = od[2]
            nst = Ss // tm
            out_shape.append(jax.ShapeDtypeStruct((Ss, Bb * wd), dt))
            out_specs.append(pl.BlockSpec((tm, wd), lambda i, nst=nst: (i % nst, i // nst)))
        else:
            out_shape.append(jax.ShapeDtypeStruct((T, wd), dt))
            out_specs.append(pl.BlockSpec((tm, wd), lambda i: (i, 0)))
    return pl.pallas_call(
        functools.partial(_norm_proj_kernel, segs=tuple(segs), n_out=len(out_defs), rope=rope),
        grid=(T // tm,), in_specs=in_specs, out_specs=out_specs, out_shape=out_shape,
        compiler_params=_cparams("parallel"), name="norm_proj",
    )(*args)


def _rope_table_kernel(pos_ref, f_ref, e_ref, one_ref, c_ref, s1_ref, s2_ref):
    ang = pos_ref[0] * f_ref[...]
    cs = jnp.concatenate([jnp.cos(ang), jnp.sin(ang)], axis=0)
    hi = cs.astype(BF16)
    lo = (cs - hi.astype(F32)).astype(BF16)
    spread = lambda j: _dot_tn(hi, e_ref[j]) + _dot_tn(lo, e_ref[j])
    c_ref[...] = spread(0) + one_ref[...]
    s1_ref[...] = spread(1)
    s2_ref[...] = spread(2)


def _rope_tables(positions):
    B, S = positions.shape
    half = ROT_DIM // 2
    inv_freq = ROPE_THETA ** (-jnp.arange(half, dtype=F32) * (2.0 / ROT_DIM))
    pos = positions.astype(F32).reshape(B, 1, S)
    lane = np.arange(LANES)
    in_head, freq = lane % HEAD_DIM, lane % half
    e = np.zeros((3, 2 * half, LANES), np.float32)
    e[0, freq, lane] = in_head < ROT_DIM
    e[1, half + freq, lane] = -1.0 * (in_head < half)
    e[2, half + freq, lane] = 1.0 * ((in_head >= half) & (in_head < ROT_DIM))
    ones = (in_head >= ROT_DIM).astype(np.float32).reshape(1, LANES)
    sh = jax.ShapeDtypeStruct((B * S, LANES), F32)
    spec = pl.BlockSpec((S, LANES), lambda b: (b, 0))
    return pl.pallas_call(
        _rope_table_kernel, grid=(B,),
        in_specs=[pl.BlockSpec((1, 1, S), lambda b: (b, 0, 0)), pl.BlockSpec((half, 1), lambda b: (0, 0)),
                  pl.BlockSpec((3, 2 * half, LANES), lambda b: (0, 0, 0)), pl.BlockSpec((1, LANES), lambda b: (0, 0))],
        out_specs=[spec, spec, spec], out_shape=[sh, sh, sh],
        compiler_params=_cparams("parallel"), name="rope_tables",
    )(pos, inv_freq.reshape(half, 1), jnp.asarray(e, dtype=BF16), jnp.asarray(ones))


def _tail_kernel(h_ref, a1_ref, a2_ref, w1_ref, w2_ref, nwm_ref, wu_ref, wd_ref, p_ref, nwp_ref, wg_ref, wp_ref,
                 fw_ref, o_ref, xn_ref, acc_ref, *, final):
    f = pl.program_id(1)

    @pl.when(f == 0)
    def _():
        hm = h_ref[...] + _dot(a1_ref[...], w1_ref[...]) + _dot(a2_ref[...], w2_ref[...])
        acc_ref[...] = hm
        xn_ref[...] = _rms(hm, nwm_ref[...]).astype(BF16)

    u = jnp.maximum(_dot(xn_ref[...], wu_ref[...]), 0.0)
    acc_ref[...] += _dot((u * u).astype(BF16), wd_ref[...])

    @pl.when(f == pl.num_programs(1) - 1)
    def _():
        h2 = acc_ref[...]
        gate = _sigmoid(_dot(_rms(h2, nwp_ref[...]).astype(BF16), wg_ref[...]))
        y = h2 + gate * _dot(p_ref[...].astype(BF16), wp_ref[...])
        if final:
            y = _rms(y, fw_ref[...])
        o_ref[...] = y


def _layer_tail(h, a1, a2, w1, w2, a2_time_major, nwm, wu, wd, p, nwp, wg, wp, fw, final, tm=512, tf=1024):
    T, D = h.shape
    FF = wu.shape[1]
    P = p.shape[1]
    tm = min(tm, T)
    K1, K2 = w1.shape[0], w2.shape[0]
    if a2_time_major is None:
        a2_spec = pl.BlockSpec((tm, K2), lambda i, f: (i, 0))
    else:
        nst = a2_time_major[1] // tm
        a2_spec = pl.BlockSpec((tm, K2), lambda i, f: (i % nst, i // nst))
    row = lambda w: pl.BlockSpec((tm, w), lambda i, f: (i, 0))
    full = lambda r, c: pl.BlockSpec((r, c), lambda i, f: (0, 0))
    return pl.pallas_call(
        functools.partial(_tail_kernel, final=final), grid=(T // tm, FF // tf),
        in_specs=[row(D), row(K1), a2_spec, full(K1, D), full(K2, D), full(1, D),
                  pl.BlockSpec((D, tf), lambda i, f: (0, f)), pl.BlockSpec((tf, D), lambda i, f: (f, 0)),
                  row(P), full(1, D), full(D, D), full(P, D), full(1, D)],
        out_specs=row(D),
        out_shape=jax.ShapeDtypeStruct((T, D), F32),
        scratch_shapes=[pltpu.VMEM((tm, D), BF16), pltpu.VMEM((tm, D), F32)],
        compiler_params=_cparams("parallel", "arbitrary"), name="layer_tail",
    )(h, a1, a2, w1, w2, nwm.reshape(1, D), wu, wd, p, nwp.reshape(1, D), wg, wp, fw.reshape(1, D))


def _fox_gate_kernel(x_ref, b_ref, o_ref, carry_ref, *, ts):
    @pl.when(pl.program_id(1) == 0)
    def _():
        carry_ref[...] = jnp.zeros_like(carry_ref)

    r = lax.broadcasted_iota(jnp.int32, (LANES, LANES), 0)
    c = lax.broadcasted_iota(jnp.int32, (LANES, LANES), 1)
    lower = (r >= c).astype(F32)
    lane = lax.broadcasted_iota(jnp.int32, (LANES, LANES), 1)

    nblk = ts // LANES
    within = []
    for n in range(nblk):
        x = x_ref[n * LANES:(n + 1) * LANES, :] + b_ref[...]
        logf = jnp.minimum(x, 0.0) - jnp.log1p(jnp.exp(-jnp.abs(x)))
        within.append(_dot(lower, logf, precision=HIGHEST))
    carry = carry_ref[0:1, :]
    for n in range(nblk):
        cum = within[n] + carry
        carry = cum[LANES - 1:LANES, :]
        f = cum * LOG2E
        hi = f.astype(BF16).astype(F32)
        mid = (f - hi).astype(BF16).astype(F32)
        lo = f - hi - mid
        for h in range(FOX_HEADS):
            v = jnp.where(lane == HEAD_DIM, hi[:, h:h + 1],
                          jnp.where(lane == HEAD_DIM + 1, mid[:, h:h + 1],
                                    jnp.where(lane == HEAD_DIM + 2, lo[:, h:h + 1], 0.0)))
            o_ref[n * LANES:(n + 1) * LANES, h * LANES:(h + 1) * LANES] = v.astype(o_ref.dtype)
    carry_ref[0:1, :] = carry


def _fox_gate(sm, bias, B, S, ts=1024):
    T = B * S
    ts = min(ts, S)
    ns = S // ts
    b = jnp.zeros((1, LANES), F32).at[0, :FOX_HEADS].set(bias.astype(F32))
    return pl.pallas_call(
        functools.partial(_fox_gate_kernel, ts=ts), grid=(B, ns),
        in_specs=[pl.BlockSpec((ts, LANES), lambda bb, n: (bb * ns + n, 0)),
                  pl.BlockSpec((1, LANES), lambda bb, n: (0, 0))],
        out_specs=pl.BlockSpec((ts, FOX_HEADS * LANES), lambda bb, n: (bb * ns + n, 0)),
        out_shape=jax.ShapeDtypeStruct((T, FOX_HEADS * LANES), BF16),
        scratch_shapes=[pltpu.VMEM((SUBLANES, LANES), F32)],
        compiler_params=_cparams("parallel", "arbitrary"), name="fox_gate",
    )(sm, b)


def _fox_attn_kernel(q_ref, k_ref, v_ref, f_ref, o_ref, *, tq, tk, dk, hp):
    i = pl.program_id(2)
    qs = [q_ref[:, hh * LANES:(hh + 1) * LANES] for hh in range(hp)]
    nsub = tq // tk

    def tile(start, size, r0, carry, masked):
        rows = pl.ds(start, size)
        ss = [_dot_nt(qs[hh][r0:], k_ref[rows, hh * LANES:(hh + 1) * LANES] + f_ref[rows, hh * LANES:(hh + 1) * LANES])
              for hh in range(hp)]
        out = []
        for hh in range(hp):
            m, acc = carry[hh]
            s = ss[hh]
            if masked:
                r = lax.broadcasted_iota(jnp.int32, (tq - r0, size), 0)
                c = lax.broadcasted_iota(jnp.int32, (tq - r0, size), 1)
                s = jnp.where(c <= r, s, NEG)
            m_new = jnp.maximum(m[r0:], jnp.max(s, axis=1, keepdims=True))
            pr = jnp.exp2(s - m_new).astype(BF16)
            acc_new = jnp.exp2(m[r0:] - m_new) * acc[r0:] + _dot(pr, v_ref[rows, hh * LANES:(hh + 1) * LANES])
            if r0:
                m_new = jnp.concatenate([m[:r0], m_new], axis=0)
                acc_new = jnp.concatenate([acc[:r0], acc_new], axis=0)
            out.append((m_new, acc_new))
        return tuple(out)

    init = tuple((jnp.full((tq, 1), NEG, F32), jnp.zeros((tq, LANES), F32)) for _ in range(hp))
    carry = lax.fori_loop(0, i * nsub, lambda j, c: tile(pl.multiple_of(j * tk, tk), tk, 0, c, False), init)
    for d in range(tq // dk):
        carry = tile(pl.multiple_of(i * tq + d * dk, dk), dk, d * dk, carry, True)
    o_ref[...] = jnp.concatenate([acc[:, :HEAD_DIM] / acc[:, HEAD_DIM:HEAD_DIM + 1] for (_, acc) in carry],
                                 axis=1).astype(o_ref.dtype)


def _fox_attn(q, k, v, f3, B, S, tq=1024, tk=1024, dk=512, hp=2):
    T = B * S
    tq = min(tq, S)
    tk = min(tk, tq)
    dk = min(dk, tq)
    nq = S // tq
    bw = hp * LANES
    ng = FOX_HEADS // hp
    res = pl.BlockSpec((S, bw), lambda b, p, i: (b, p))
    return pl.pallas_call(
        functools.partial(_fox_attn_kernel, tq=tq, tk=tk, dk=dk, hp=hp), grid=(B, ng, nq),
        in_specs=[pl.BlockSpec((tq, bw), lambda b, p, i: (b * nq + i, p)), res, res, res],
        out_specs=pl.BlockSpec((tq, hp * HEAD_DIM), lambda b, p, i: (b * nq + i, p)),
        out_shape=jax.ShapeDtypeStruct((T, FOX_HEADS * HEAD_DIM), BF16),
        compiler_params=_cparams("parallel", "parallel", "arbitrary"), name="fox_attn",
    )(q, k, v, f3)


def _cumsum_rows(x):
    n = x.shape[0]
    row = lax.broadcasted_iota(jnp.int32, x.shape, 0)
    d = 1
    while d < n:
        x = x + jnp.where(row >= d, pltpu.roll(x, d, 0), 0.0)
        d *= 2
    return x


def _gdn_kernel(x_ref, z_ref, sm_ref, cw_ref, par_ref, nw_ref, o_ref, state_ref, xp_ref, *, cps):
    C = GDN_CHUNK
    D = GDN_DIM
    W = GDN_HEADS * D
    R = cps * C
    n = pl.program_id(1)

    @pl.when(n == 0)
    def _():
        state_ref[...] = jnp.zeros_like(state_ref)
        xp_ref[0:SUBLANES, :] = jnp.zeros((SUBLANES, 3 * W), F32)

    x = x_ref[...]
    xp_ref[SUBLANES:SUBLANES + R, :] = x
    cw = cw_ref[...]
    y = x * cw[CONV_W - 1:CONV_W]
    for j in range(CONV_W - 1):
        off = SUBLANES - (CONV_W - 1) + j
        y = y + xp_ref[off:off + R, :] * cw[j:j + 1]
    xp_ref[0:SUBLANES, :] = x[R - SUBLANES:R]
    y = _silu(y)

    sm = sm_ref[...]
    beta_all = _sigmoid(sm)
    g_all = -jnp.exp(par_ref[0:1, :]) * _softplus(sm + par_ref[1:2, :])
    ri = lax.broadcasted_iota(jnp.int32, (C, C), 0)
    ci = lax.broadcasted_iota(jnp.int32, (C, C), 1)
    incl = ri >= ci
    strict = ri > ci

    chains = [(c, h) for c in range(cps) for h in range(GDN_HEADS)]
    gcs = [_cumsum_rows(g_all[c * C:(c + 1) * C]) for c in range(cps)]
    gcts = [gc.T for gc in gcs]
    pre = {}
    for (c, h) in chains:
        r0 = c * C
        q = y[r0:r0 + C, h * D:(h + 1) * D]
        k = y[r0:r0 + C, W + h * D:W + (h + 1) * D]
        v = y[r0:r0 + C, 2 * W + h * D:2 * W + (h + 1) * D]
        qn = q * lax.rsqrt(jnp.sum(q * q, axis=-1, keepdims=True) + 1e-6) * (D ** -0.5)
        kn = k * lax.rsqrt(jnp.sum(k * k, axis=-1, keepdims=True) + 1e-6)
        beta = beta_all[r0:r0 + C, 8 + h:9 + h]
        gcol = gcs[c][:, 12 + h:13 + h]
        grow = gcts[c][12 + h:13 + h, :]
        decay = jnp.where(incl, jnp.exp(jnp.where(incl, gcol - grow, 0.0)), 0.0)
        kb = kn * beta
        g_last = gcol[C - 1:C, :]
        pre[c, h] = dict(qn16=qn.astype(BF16), kn16=kn.astype(BF16), kb16=kb.astype(BF16), decay=decay,
                         rhs=jnp.concatenate([v * beta, kb * jnp.exp(gcol)], axis=1),
                         qg16=(qn * jnp.exp(gcol)).astype(BF16),
                         kg16=(kn * jnp.exp(g_last - gcol)).astype(BF16), e_last=jnp.exp(g_last))
    kk = {ch: _dot_nt(pre[ch]["kb16"], pre[ch]["kn16"]) for ch in chains}
    qk = {ch: _dot_nt(pre[ch]["qn16"], pre[ch]["kn16"]) for ch in chains}
    P = {ch: -jnp.where(strict, kk[ch] * pre[ch]["decay"], 0.0) for ch in chains}
    X = dict(P)
    for _ in range(5):
        P = {ch: _dot(P[ch].astype(BF16), P[ch].astype(BF16)) for ch in chains}
        XP = {ch: _dot(X[ch].astype(BF16), P[ch].astype(BF16)) for ch in chains}
        X = {ch: X[ch] + P[ch] + XP[ch] for ch in chains}
    sol = {ch: pre[ch]["rhs"] + _dot(X[ch].astype(BF16), pre[ch]["rhs"].astype(BF16)) for ch in chains}
    attn16 = {ch: jnp.where(incl, qk[ch] * pre[ch]["decay"], 0.0).astype(BF16) for ch in chains}

    heads = range(GDN_HEADS)
    states = [state_ref[h] for h in heads]
    for c in range(cps):
        r0 = c * C
        st16 = [states[h].astype(BF16) for h in heads]
        ws = [_dot(sol[c, h][:, D:].astype(BF16), st16[h]) for h in heads]
        qs = [_dot(pre[c, h]["qg16"], st16[h]) for h in heads]
        v_new = [(sol[c, h][:, :D] - ws[h]).astype(BF16) for h in heads]
        av = [_dot(attn16[c, h], v_new[h]) for h in heads]
        kv = [_dot_tn(pre[c, h]["kg16"], v_new[h]) for h in heads]
        outs = []
        for h in heads:
            states[h] = states[h] * pre[c, h]["e_last"] + kv[h]
            o = _rms(qs[h] + av[h], nw_ref[...]) * _silu(z_ref[r0:r0 + C, h * D:(h + 1) * D])
            outs.append(o)
        o_ref[r0:r0 + C, :] = jnp.concatenate(outs, axis=1).astype(o_ref.dtype)
    for h in heads:
        state_ref[h] = states[h]


def _gdn(gqkv, gz, small, conv_w, a_log, dt_bias, norm_w, B, S, cps=8):
    T = B * S
    C = GDN_CHUNK
    R = cps * C
    N = S // R
    W = GDN_HEADS * GDN_DIM
    par = jnp.zeros((SUBLANES, LANES), F32)
    par = par.at[0, 12:16].set(a_log.astype(F32)).at[1, 12:16].set(dt_bias.astype(F32))
    return pl.pallas_call(
        functools.partial(_gdn_kernel, cps=cps), grid=(B, N),
        in_specs=[pl.BlockSpec((R, 3 * W), lambda b, n: (b * N + n, 0)),
                  pl.BlockSpec((R, W), lambda b, n: (b * N + n, 0)),
                  pl.BlockSpec((R, LANES), lambda b, n: (b * N + n, 0)),
                  pl.BlockSpec((CONV_W, 3 * W), lambda b, n: (0, 0)),
                  pl.BlockSpec((SUBLANES, LANES), lambda b, n: (0, 0)),
                  pl.BlockSpec((1, GDN_DIM), lambda b, n: (0, 0))],
        out_specs=pl.BlockSpec((R, W), lambda b, n: (b * N + n, 0)),
        out_shape=jax.ShapeDtypeStruct((T, W), BF16),
        scratch_shapes=[pltpu.VMEM((GDN_HEADS, GDN_DIM, GDN_DIM), F32),
                        pltpu.VMEM((SUBLANES + R, 3 * W), F32)],
        compiler_params=_cparams("parallel", "arbitrary"), name="gdn",
    )(gqkv, gz, small, conv_w.astype(F32), par, norm_w.reshape(1, GDN_DIM).astype(F32))


def _compress_kernel(x_ref, pe_ref, w1_ref, wbd_ref, w2_ref, o_ref, *, nh):
    ab = jnp.zeros((nh, NSA_GROUPS * 2 * CMP_HIDDEN), F32)
    for l in range(CMP_STRIDE):
        ab = ab + _dot(x_ref[pl.ds(l, nh, stride=CMP_STRIDE), :].astype(BF16), wbd_ref[0, l])
    lane = lax.broadcasted_iota(jnp.int32, (nh, HEAD_DIM), 1)
    one_col = jnp.where(lane == 0, 1.0, 0.0)
    c = _dot(pe_ref[0].astype(BF16), w1_ref[0])
    for g in range(NSA_GROUPS):
        a = ab[:, 2 * g * CMP_HIDDEN:(2 * g + 1) * CMP_HIDDEN]
        b = ab[:, (2 * g + 1) * CMP_HIDDEN:(2 * g + 2) * CMP_HIDDEN]
        hid = _gelu_tanh(a + pltpu.roll(b, nh - 1, 0) + c[0:1])
        out = _dot(hid.astype(BF16), w2_ref[0])
        o_ref[0, g] = jnp.concatenate([out, one_col], axis=1).astype(o_ref.dtype)


def _compress(kc, vc, pe, w1, w2, B, S):
    nh = S // CMP_STRIDE
    G = NSA_GROUPS
    half = CMP_STRIDE * HEAD_DIM
    w1r = w1.reshape(2, 2, CMP_STRIDE, HEAD_DIM, CMP_HIDDEN)
    per_tok = jnp.concatenate([w1r[:, 0], w1r[:, 1]], axis=-1)
    eye = jnp.eye(G, dtype=w1.dtype)
    wbd = per_tok[:, :, None, :, None, :] * eye[None, None, :, None, :, None]
    wbd = wbd.reshape(2, CMP_STRIDE, G * HEAD_DIM, G * 2 * CMP_HIDDEN)
    x_spec = pl.BlockSpec((S, G * HEAD_DIM), lambda b, j: (b, 0))
    out = [pl.pallas_call(
        functools.partial(_compress_kernel, nh=nh), grid=(B, 1),
        in_specs=[x_spec, pl.BlockSpec((1, SUBLANES, 2 * half), lambda b, j, kv=kv: (kv, 0, 0)),
                  pl.BlockSpec((1, 2 * half, CMP_HIDDEN), lambda b, j, kv=kv: (kv, 0, 0)),
                  pl.BlockSpec((1, CMP_STRIDE, G * HEAD_DIM, G * 2 * CMP_HIDDEN), lambda b, j, kv=kv: (kv, 0, 0, 0)),
                  pl.BlockSpec((1, CMP_HIDDEN, HEAD_DIM), lambda b, j, kv=kv: (kv, 0, 0))],
        out_specs=pl.BlockSpec((1, G, nh, LANES), lambda b, j: (b, 0, 0, 0)),
        out_shape=jax.ShapeDtypeStruct((B, G, nh, LANES), BF16),
        compiler_params=_cparams("parallel", "arbitrary"), name="nsa_compress",
    )(x, pe, w1, wbd, w2) for kv, x in enumerate((kc, vc))]
    return out


def _stack_heads(qb):
    return jnp.concatenate([qb[:, h * HEAD_DIM:(h + 1) * HEAD_DIM] for h in range(NSA_HPG)], axis=0)


def _unstack_heads(o, tq):
    return jnp.concatenate([o[h * tq:(h + 1) * tq] for h in range(NSA_HPG)], axis=1)


def _nsa_cmp_kernel(q_ref, kc_ref, vc_ref, ov_ref, oc_ref, bias_ref, *, tq, n_sel, top_k):
    i = pl.program_id(1)
    G = NSA_GROUPS
    GW = NSA_HPG * HEAD_DIM
    ncp = kc_ref.shape[2]
    row = lax.broadcasted_iota(jnp.int32, (NSA_HPG * tq, 1), 0)
    t4 = i * tq + (row & (tq - 1))
    cmp_end = lax.broadcasted_iota(jnp.int32, (1, ncp), 1) * CMP_STRIDE + (CMP_BLOCK - 1)
    valid = cmp_end <= t4
    ss = [_dot_nt(_stack_heads(q_ref[:, g * GW:(g + 1) * GW]), kc_ref[0, g, :, :HEAD_DIM]) for g in range(G)]
    ps, accs = [], []
    for g in range(G):
        s = jnp.where(valid, ss[g], NEG)
        m = jnp.max(s, axis=1, keepdims=True)
        p = jnp.exp2(s - m)
        ps.append(p)
        accs.append(_dot(p.astype(BF16), vc_ref[0, g]))
    imps = []
    for g in range(G):
        inv = jnp.where(t4 >= CMP_BLOCK - 1, 1.0 / accs[g][:, HEAD_DIM:HEAD_DIM + 1], 0.0)
        oc_ref[:, g * GW:(g + 1) * GW] = _unstack_heads(accs[g][:, :HEAD_DIM] * inv, tq)
        p = ps[g] * inv
        psum = p[0:tq] + p[tq:2 * tq] + p[2 * tq:3 * tq] + p[3 * tq:4 * tq]
        imps.append(_dot(psum, ov_ref[...], precision=HIGHEST))
    blk = lax.broadcasted_iota(jnp.int32, (1, LANES), 1)
    t = i * tq + lax.broadcasted_iota(jnp.int32, (tq, 1), 0)
    cur = t >> (SEL_BLOCK.bit_length() - 1)
    forced = (blk == 0) | (blk == cur) | (blk == cur - 1)
    future = blk * SEL_BLOCK > t
    blk_t = lax.broadcasted_iota(jnp.int32, (LANES, tq), 0).astype(F32)
    for g in range(G):
        imp = jnp.where(future, NEG, jnp.where(forced, -NEG, imps[g]))
        imp = jnp.where(blk < n_sel, imp, REMOVED)
        imp_t = imp.T
        for _ in range(top_k):
            mx = jnp.max(imp_t, axis=0, keepdims=True)
            first = jnp.min(jnp.where(imp_t == mx, blk_t, float(LANES)), axis=0, keepdims=True)
            imp_t = jnp.where(blk_t == first, REMOVED, imp_t)
        sel = jnp.where(imp_t == REMOVED, 1.0, 0.0).T
        bias_ref[0, g] = jnp.where((sel > 0.0) & jnp.logical_not(future), 0.0, NEG).astype(bias_ref.dtype)


def _nsa_cmp(q, kc, vc, overlap, B, S, tq=256):
    T = B * S
    tq = min(tq, S)
    nq = S // tq
    ncp = kc.shape[2]
    n_sel = S // SEL_BLOCK
    QW = NSA_HEADS * HEAD_DIM
    G = NSA_GROUPS
    return pl.pallas_call(
        functools.partial(_nsa_cmp_kernel, tq=tq, n_sel=n_sel, top_k=min(SEL_TOPK, n_sel)),
        grid=(B, nq),
        in_specs=[pl.BlockSpec((tq, QW), lambda b, i: (b * nq + i, 0)),
                  pl.BlockSpec((1, G, ncp, LANES), lambda b, i: (b, 0, 0, 0)),
                  pl.BlockSpec((1, G, ncp, LANES), lambda b, i: (b, 0, 0, 0)),
                  pl.BlockSpec((ncp, LANES), lambda b, i: (0, 0))],
        out_specs=[pl.BlockSpec((tq, QW), lambda b, i: (b * nq + i, 0)),
                   pl.BlockSpec((1, G, tq, LANES), lambda b, i: (b, 0, i, 0))],
        out_shape=[jax.ShapeDtypeStruct((T, QW), F32),
                   jax.ShapeDtypeStruct((B, G, S, LANES), BF16)],
        compiler_params=_cparams("parallel", "parallel"), name="nsa_cmp",
    )(q, kc, vc, overlap)


def _nsa_main_kernel(q_ref, bias_ref, ks_ref, oh_ref, vs_ref, kw_ref, vw_ref, oc_ref, g_ref, o_ref, *, qn, tk, wspan):
    R = NSA_HPG * qn
    tb = 2 * tk
    i = pl.program_id(2)
    q4 = _stack_heads(q_ref[...])
    q4p = jnp.concatenate([q4, jnp.zeros_like(q4)], axis=1)
    b4 = jnp.concatenate([bias_ref[0, 0]] * NSA_HPG, axis=0)
    qa = jnp.concatenate([q4p, b4], axis=1)
    t4 = i * qn + (lax.broadcasted_iota(jnp.int32, (R, 1), 0) & (qn - 1))

    def scores(start, size):
        rows = pl.ds(start, size)
        ka = jnp.concatenate([ks_ref[rows, :], oh_ref[rows, :]], axis=1)
        return _dot_nt(qa, ka)

    def finish(s, start, size, carry, masked):
        m, acc = carry
        if masked:
            kpos = start + lax.broadcasted_iota(jnp.int32, (1, size), 1)
            s = jnp.where(kpos <= t4, s, NEG)
        m_new = jnp.maximum(m, jnp.max(s, axis=1, keepdims=True))
        pr = jnp.exp2(s - m_new).astype(BF16)
        acc = jnp.exp2(m - m_new) * acc + _dot(pr, vs_ref[pl.ds(start, size), :])
        return m_new, acc

    def step(start, size, carry, masked):
        return finish(scores(start, size), start, size, carry, masked)

    first = i * qn
    nbig = first // tb
    diag = (first // tk) * tk
    nsmall = (diag - nbig * tb) // tk
    init = (jnp.full((R, 1), NEG, F32), jnp.zeros((R, LANES), F32))
    carry = lax.fori_loop(0, nbig, lambda j, c: step(pl.multiple_of(j * tb, tb), tb, c, False), init)
    carry = lax.fori_loop(0, nsmall, lambda j, c: step(pl.multiple_of(nbig * tb, tk), tk, c, False), carry)
    dstart = pl.multiple_of(diag, tk)
    s_d = scores(dstart, tk)
    wrows = pl.ds(pl.multiple_of(jnp.maximum(first + qn - wspan, 0), qn), wspan)
    s_w = _dot_nt(q4p, kw_ref[wrows, :])
    _, acc_s = finish(s_d, dstart, tk, carry, True)
    kpos = jnp.maximum(first + qn - wspan, 0) + lax.broadcasted_iota(jnp.int32, (1, wspan), 1)
    wmask = lax.bitcast_convert_type(t4 - kpos, jnp.uint32) < jnp.uint32(WINDOW)
    s_w = jnp.where(wmask, s_w, NEG)
    m_w = jnp.max(s_w, axis=1, keepdims=True)
    acc_w = _dot(jnp.exp2(s_w - m_w).astype(BF16), vw_ref[wrows, :])

    gates = _sigmoid(g_ref[...])
    first_group = pl.program_id(1) == 0
    ng = NSA_HPG * 3

    def gate(h, branch):
        j = 3 * h + branch
        return jnp.where(first_group, gates[:, j:j + 1], gates[:, ng + j:ng + j + 1])

    oc = oc_ref[...]
    outs = []
    for h in range(NSA_HPG):
        hs = slice(h * qn, (h + 1) * qn)
        c_s = gate(h, 1) / acc_s[hs, HEAD_DIM:HEAD_DIM + 1]
        c_w = gate(h, 2) / acc_w[hs, HEAD_DIM:HEAD_DIM + 1]
        outs.append(gate(h, 0) * oc[:, h * HEAD_DIM:(h + 1) * HEAD_DIM]
                    + c_s * acc_s[hs, :HEAD_DIM] + c_w * acc_w[hs, :HEAD_DIM])
    o_ref[...] = jnp.concatenate(outs, axis=1).astype(o_ref.dtype)


def _nsa_main(q, bias, ksp, onehot, vsa, kwp, vwa, oc, gates, B, S, qn=256, tk=512):
    T = B * S
    nq = S // qn
    tk = min(tk, S // 2)
    wspan = min(WINDOW + qn, S)
    GW = NSA_HPG * HEAD_DIM
    res = pl.BlockSpec((S, LANES), lambda b, g, i: (b, g))
    return pl.pallas_call(
        functools.partial(_nsa_main_kernel, qn=qn, tk=tk, wspan=wspan), grid=(B, NSA_GROUPS, nq),
        in_specs=[pl.BlockSpec((qn, GW), lambda b, g, i: (b * nq + i, g)),
                  pl.BlockSpec((1, 1, qn, LANES), lambda b, g, i: (b, g, i, 0)),
                  res, pl.BlockSpec((S, LANES), lambda b, g, i: (0, 0)), res, res, res,
                  pl.BlockSpec((qn, GW), lambda b, g, i: (b * nq + i, g)),
                  pl.BlockSpec((qn, LANES), lambda b, g, i: (b * nq + i, 0))],
        out_specs=pl.BlockSpec((qn, GW), lambda b, g, i: (b * nq + i, g)),
        out_shape=jax.ShapeDtypeStruct((T, NSA_HEADS * HEAD_DIM), BF16),
        compiler_params=_cparams("parallel", "parallel", "arbitrary"), name="nsa_main",
    )(q, bias, ksp, onehot, vsa, kwp, vwa, oc, gates)


def _lru_kernel(gx_ref, cw_ref, cb_ref, wab_ref, bab_ref, lam_ref, o_ref, h_ref, tail_ref, a_ref, b_ref, *, ts):
    n = pl.program_id(0)
    Bb = gx_ref.shape[1]
    W = gx_ref.shape[2] // 2

    @pl.when(n == 0)
    def _():
        h_ref[...] = jnp.zeros_like(h_ref)
        tail_ref[...] = jnp.zeros_like(tail_ref)

    xin = gx_ref[:, :, W:]
    xp = jnp.concatenate([tail_ref[...], xin], axis=0)
    cw = cw_ref[...]
    x = cb_ref[...].reshape(1, 1, W)
    for j in range(CONV_W):
        x = x + xp[j:j + ts] * cw[j:j + 1].reshape(1, 1, W)
    tail_ref[...] = xin[ts - (CONV_W - 1):ts]
    x2 = x.reshape(ts * Bb, W)
    pre = _dot(x2.astype(BF16), wab_ref[...]) + bab_ref[...]
    r = _sigmoid(pre[:, :W])
    ig = _sigmoid(pre[:, W:])
    log_a = (-RG_C * _softplus(-lam_ref[...])) * r
    a = jnp.exp(log_a)
    th = jnp.tanh(log_a)
    bb = jnp.sqrt(-2.0 * th / (1.0 - th)) * (ig * x2)
    a_ref[...] = a.reshape(ts, Bb, W)
    b_ref[...] = bb.reshape(ts, Bb, W)

    def scan(t, h):
        h = a_ref[t] * h + b_ref[t]
        b_ref[t] = h
        return h

    h_ref[...] = lax.fori_loop(0, ts, scan, h_ref[...], unroll=8)
    o_ref[...] = (b_ref[...] * _gelu_tanh(gx_ref[:, :, :W])).astype(o_ref.dtype)


def _lru(gx_t, conv_w, conv_b, wab, bab, lam, ts=128):
    S, Bb, W2 = gx_t.shape
    W = W2 // 2
    ts = min(ts, S)
    full = lambda shape: pl.BlockSpec(shape, lambda n: (0,) * len(shape))
    return pl.pallas_call(
        functools.partial(_lru_kernel, ts=ts), grid=(S // ts,),
        in_specs=[pl.BlockSpec((ts, Bb, W2), lambda n: (n, 0, 0)),
                  full((CONV_W, W)), full((1, W)), full((W, 2 * W)), full((1, 2 * W)), full((1, W))],
        out_specs=pl.BlockSpec((ts, Bb, W), lambda n: (n, 0, 0)),
        out_shape=jax.ShapeDtypeStruct((S, Bb, W), BF16),
        scratch_shapes=[pltpu.VMEM((Bb, W), F32), pltpu.VMEM((CONV_W - 1, Bb, W), F32),
                        pltpu.VMEM((ts, Bb, W), F32), pltpu.VMEM((ts, Bb, W), F32)],
        compiler_params=_cparams("arbitrary"), name="rg_lru",
    )(gx_t, conv_w, conv_b, wab, bab, lam)


def _even_mixer(h, B, S, norm_w, w_in, fox_bf, conv_w, a_log, dt_bias, gdn_norm_w, w_out):
    T, D = h.shape
    FW = FOX_HEADS * HEAD_DIM
    GW = GDN_HEADS * GDN_DIM
    o_ff = 3 * FW
    o_g = o_ff + FOX_HEADS
    o_gb = o_g + 4 * GW
    small = jnp.concatenate([w_in[:, o_ff:o_g], w_in[:, o_gb:o_gb + 2 * GDN_HEADS],
                             jnp.zeros((D, LANES - FOX_HEADS - 2 * GDN_HEADS), w_in.dtype)], axis=1)
    w = jnp.concatenate([w_in[:, :o_ff], w_in[:, o_g:o_gb], small], axis=1).astype(BF16)
    segs = [(0, 0, 0, FW, False, HEAD_DIM ** -0.5 * LOG2E, tuple((j, -1.0) for j in range(FOX_F_PIECES))),
            (1, 0, FW, FW, False, 1.0, ()),
            (2, 0, 2 * FW, FW, False, 1.0, ((0, 1.0),)),
            (3, 0, 3 * FW, 3 * GW, False, 1.0, None),
            (4, 0, 3 * FW + 3 * GW, GW, False, 1.0, None),
            (5, 0, 3 * FW + 4 * GW, LANES, False, 1.0, None)]
    fq, fk, fv, gqkv, gz, sm = _norm_proj(h, norm_w, w, segs,
                                          [(2 * FW, BF16), (2 * FW, BF16), (2 * FW, BF16),
                                           (3 * GW, F32), (GW, F32), (LANES, F32)])
    f3 = _fox_gate(sm, fox_bf, B, S)
    fox = _fox_attn(fq, fk, fv, f3, B, S)
    gdn = _gdn(gqkv, gz, sm, conv_w, a_log, dt_bias, gdn_norm_w, B, S)
    wo = w_out.astype(BF16)
    return fox, gdn, wo[:FW], wo[FW:], None


def _odd_mixer(h, B, S, rope_tabs, norm_w, w_in, k_pe, k_w1, k_w2, v_pe, v_w1, v_w2,
               conv_w, conv_b, wa, ba, wx, bx, lam, w_out):
    T, D = h.shape
    QW = NSA_HEADS * HEAD_DIM
    KW = NSA_GROUPS * HEAD_DIM
    G = NSA_GROUPS
    o_ng = QW + 6 * KW
    ngw = NSA_HEADS * 3
    w = jnp.concatenate([w_in[:, :o_ng], w_in[:, o_ng:o_ng + ngw], jnp.zeros((D, LANES - ngw), w_in.dtype),
                         w_in[:, o_ng + ngw:]], axis=1).astype(BF16)
    ones = ((0, 1.0),)
    segs = [(0, 0, 0, QW, True, HEAD_DIM ** -0.5 * LOG2E, None),
            (1, 0, QW, KW, True, 1.0, None),
            (2, 0, QW + KW, KW, False, 1.0, None),
            (3, 0, QW + 2 * KW, KW, True, 1.0, ()),
            (4, 0, QW + 3 * KW, KW, False, 1.0, ones),
            (5, 0, QW + 4 * KW, KW, True, 1.0, ()),
            (6, 0, QW + 5 * KW, KW, False, 1.0, ones),
            (7, 0, o_ng, LANES, False, 1.0, None),
            (8, 0, o_ng + LANES, 2 * LRU_W, False, 1.0, None)]
    q, kc, vc, ksp, vsa, kwp, vwa, ng, rgx_t = _norm_proj(
        h, norm_w, w, segs,
        [(QW, BF16), (KW, F32), (KW, F32), (2 * KW, BF16), (2 * KW, BF16), (2 * KW, BF16), (2 * KW, BF16),
         (LANES, F32), (2 * LRU_W, F32, (B, S))],
        rope_tabs=rope_tabs)
    nh = S // CMP_STRIDE
    pe = jnp.stack([k_pe.reshape(-1), v_pe.reshape(-1)]).astype(F32)
    pe = jnp.broadcast_to(pe[:, None, :], (2, SUBLANES, pe.shape[-1]))
    w1 = jnp.stack([k_w1, v_w1]).astype(BF16)
    w2 = jnp.stack([k_w2, v_w2]).astype(BF16)
    kcmp, vcmp = _compress(kc, vc, pe, w1, w2, B, S)
    n_cmp = (S - CMP_BLOCK) // CMP_STRIDE + 1
    ncp = -(-nh // LANES) * LANES
    if ncp != nh:
        kcmp, vcmp = (jnp.pad(t, ((0, 0), (0, 0), (0, ncp - nh), (0, 0))) for t in (kcmp, vcmp))
    n_sel = S // SEL_BLOCK
    cs = np.arange(ncp) * CMP_STRIDE
    ss = np.arange(LANES) * SEL_BLOCK
    ov = ((cs[:, None] <= ss[None, :] + SEL_BLOCK - 1) & (cs[:, None] + CMP_BLOCK - 1 >= ss[None, :])
          & (np.arange(ncp)[:, None] < n_cmp) & (np.arange(LANES)[None, :] < n_sel))
    overlap = jnp.asarray(ov.astype(np.float32))
    oc, bias = _nsa_cmp(q, kcmp, vcmp, overlap, B, S)
    blk_of_pos = np.arange(S) // SEL_BLOCK
    onehot = jnp.asarray((blk_of_pos[:, None] == np.arange(LANES)[None, :]).astype(np.float32), dtype=BF16)
    nsa = _nsa_main(q, bias, ksp, onehot, vsa, kwp, vwa, oc, ng, B, S)
    nblk, bw, _ = wa.shape
    eye = jnp.eye(nblk, dtype=wa.dtype)
    dense = lambda wb: (eye[:, None, :, None] * wb[:, :, None, :]).reshape(nblk * bw, nblk * bw)
    wab = jnp.concatenate([dense(wa), dense(wx)], axis=1).astype(BF16)
    bab = jnp.concatenate([ba, bx]).reshape(1, 2 * LRU_W).astype(F32)
    lru_t = _lru(rgx_t.reshape(S, B, 2 * LRU_W), conv_w.astype(F32), conv_b.reshape(1, LRU_W).astype(F32), wab, bab,
                 lam.reshape(1, LRU_W).astype(F32))
    wo = w_out.astype(BF16)
    return nsa, lru_t.reshape(S, B * LRU_W), wo[:QW], wo[QW:], (B, S)


def kernel(x, p, positions, even_norm_mix, even_w_in, even_fox_bf, even_gdn_conv_w, even_gdn_a_log, even_gdn_dt_bias, even_gdn_norm_w, even_w_out, odd_norm_mix, odd_w_in, odd_cmp_k_pe, odd_cmp_k_w1, odd_cmp_k_w2, odd_cmp_v_pe, odd_cmp_v_w1, odd_cmp_v_w2, odd_rg_conv_w, odd_rg_conv_b, odd_rg_wa, odd_rg_ba, odd_rg_wx, odd_rg_bx, odd_rg_lambda, odd_w_out, mlp_norm, mlp_w_up, mlp_w_down, ple_norm, ple_w_gate, ple_w_proj, final_norm):
    B, S, D = x.shape
    T = B * S
    depth = p.shape[0]
    h = x.reshape(T, D)
    rope_tabs = _rope_tables(positions) if depth > 1 else None
    for i in range(depth):
        j = i // 2
        if i % 2 == 0:
            mix = _even_mixer(h, B, S, even_norm_mix[j], even_w_in[j], even_fox_bf[j], even_gdn_conv_w[j],
                              even_gdn_a_log[j], even_gdn_dt_bias[j], even_gdn_norm_w[j], even_w_out[j])
        else:
            mix = _odd_mixer(h, B, S, rope_tabs, odd_norm_mix[j], odd_w_in[j], odd_cmp_k_pe[j], odd_cmp_k_w1[j],
                             odd_cmp_k_w2[j], odd_cmp_v_pe[j], odd_cmp_v_w1[j], odd_cmp_v_w2[j], odd_rg_conv_w[j],
                             odd_rg_conv_b[j], odd_rg_wa[j], odd_rg_ba[j], odd_rg_wx[j], odd_rg_bx[j],
                             odd_rg_lambda[j], odd_w_out[j])
        h = _layer_tail(h, *mix, mlp_norm[i], mlp_w_up[i].astype(BF16), mlp_w_down[i].astype(BF16),
                        p[i].reshape(T, -1), ple_norm[i], ple_w_gate[i].astype(BF16), ple_w_proj[i].astype(BF16),
                        final_norm, final=(i == depth - 1))
    return h.reshape(B, S, D)
```

```python
import functools

import numpy as np
import jax
import jax.numpy as jnp
from jax import lax
from jax.experimental import pallas as pl
from jax.experimental.pallas import tpu as pltpu

F32 = jnp.float32
BF16 = jnp.bfloat16
HIGHEST = lax.Precision.HIGHEST

NORM_EPS = 1e-6
LOG2E = float(np.log2(np.e))
NEG = -1e30
REMOVED = -3e38
LANES = 128
SUBLANES = 8
VMEM_LIMIT = 56 * 1024 * 1024

HEAD_DIM = 64
ROT_DIM = 16
ROPE_THETA = 500000.0
FOX_HEADS = 8
FOX_F_PIECES = 3
GDN_HEADS = 4
GDN_DIM = 128
GDN_CHUNK = 64
NSA_HEADS = 8
NSA_GROUPS = 2
NSA_HPG = NSA_HEADS // NSA_GROUPS
CMP_BLOCK = 32
CMP_STRIDE = 16
CMP_HIDDEN = 128
SEL_BLOCK = 64
SEL_TOPK = 16
WINDOW = 512
LRU_W = 512
RG_C = 8.0
CONV_W = 4


def _cparams(*sem):
    return pltpu.CompilerParams(dimension_semantics=sem, vmem_limit_bytes=VMEM_LIMIT)


def _dot(a, b, **kw):
    return jnp.dot(a, b, preferred_element_type=F32, **kw)


def _dot_nt(a, b, **kw):
    return lax.dot_general(a, b, (((1,), (1,)), ((), ())), preferred_element_type=F32, **kw)


def _dot_tn(a, b, **kw):
    return lax.dot_general(a, b, (((0,), (0,)), ((), ())), preferred_element_type=F32, **kw)


def _softplus(x):
    return jnp.maximum(x, 0.0) + jnp.log1p(jnp.exp(-jnp.abs(x)))


def _sigmoid(x):
    return 1.0 / (1.0 + jnp.exp(-x))


def _silu(x):
    return x * _sigmoid(x)


def _gelu_tanh(x):
    return 0.5 * x * (1.0 + jnp.tanh(np.float32(np.sqrt(2.0 / np.pi)) * (x + 0.044715 * (x * x * x))))


def _rms(x, w):
    ms = jnp.mean(x * x, axis=-1, keepdims=True)
    return x * lax.rsqrt(ms + NORM_EPS) * w


def _apply_rope(y, c, s1, s2):
    outs = []
    for g in range(y.shape[1] // LANES):
        yg = y[:, g * LANES:(g + 1) * LANES]
        outs.append(yg * c + pltpu.roll(yg, LANES - ROT_DIM // 2, 1) * s1 + pltpu.roll(yg, ROT_DIM // 2, 1) * s2)
    return outs[0] if len(outs) == 1 else jnp.concatenate(outs, axis=1)


def _pad_heads(y, consts):
    tm = y.shape[0]
    lane = lax.broadcasted_iota(jnp.int32, (tm, LANES), 1)
    cst = jnp.zeros((tm, LANES), F32)
    for ln, val in consts:
        cst = jnp.where(lane == HEAD_DIM + ln, val, cst)
    low = lane < HEAD_DIM
    outs = []
    for g in range(y.shape[1] // LANES):
        yg = y[:, g * LANES:(g + 1) * LANES]
        outs.append(jnp.where(low, yg, cst))
        outs.append(jnp.where(low, pltpu.roll(yg, HEAD_DIM, 1), cst))
    return jnp.concatenate(outs, axis=1)


def _norm_proj_kernel(*refs, segs, n_out, rope):
    x_ref, nw_ref, w_ref = refs[:3]
    n_in = 3
    if rope:
        c_ref, s1_ref, s2_ref = refs[n_in:n_in + 3]
        n_in += 3
    out_refs = refs[n_in:n_in + n_out]
    xn = _rms(x_ref[...], nw_ref[...]).astype(BF16)
    for (oi, oc, wc, width, do_rope, scale, pad) in segs:
        y = _dot(xn, w_ref[:, wc:wc + width])
        if do_rope:
            y = _apply_rope(y, c_ref[...], s1_ref[...], s2_ref[...])
        if scale != 1.0:
            y = y * scale
        if pad is not None:
            y = _pad_heads(y, pad)
        out_refs[oi][:, oc:oc + y.shape[1]] = y.astype(out_refs[oi].dtype)


def _norm_proj(x, nw, w, segs, out_defs, rope_tabs=None, tm=512):
    T, D = x.shape
    N = w.shape[1]
    tm = min(tm, T)
    rope = rope_tabs is not None
    in_specs = [pl.BlockSpec((tm, D), lambda i: (i, 0)),
                pl.BlockSpec((1, D), lambda i: (0, 0)),
                pl.BlockSpec((D, N), lambda i: (0, 0))]
    args = [x, nw.reshape(1, D), w]
    if rope:
        in_specs += [pl.BlockSpec((tm, LANES), lambda i: (i, 0))] * 3
        args += list(rope_tabs)
    out_shape, out_specs = [], []
    for od in out_defs:
        wd, dt = od[0], od[1]
        if len(od) == 3:
            Bb, Ss = od[2]
            nst = Ss // tm
            out_shape.append(jax.ShapeDtypeStruct((Ss, Bb * wd), dt))
            out_specs.append(pl.BlockSpec((tm, wd), lambda i, nst=nst: (i % nst, i // nst)))
        else:
            out_shape.append(jax.ShapeDtypeStruct((T, wd), dt))
            out_specs.append(pl.BlockSpec((tm, wd), lambda i: (i, 0)))
    return pl.pallas_call(
        functools.partial(_norm_proj_kernel, segs=tuple(segs), n_out=len(out_defs), rope=rope),
        grid=(T // tm,), in_specs=in_specs, out_specs=out_specs, out_shape=out_shape,
        compiler_params=_cparams("parallel"), name="norm_proj",
    )(*args)


def _rope_table_kernel(pos_ref, f_ref, e_ref, one_ref, c_ref, s1_ref, s2_ref):
    ang = pos_ref[0] * f_ref[...]
    cs = jnp.concatenate([jnp.cos(ang), jnp.sin(ang)], axis=0)
    hi = cs.astype(BF16)
    lo = (cs - hi.astype(F32)).astype(BF16)
    spread = lambda j: _dot_tn(hi, e_ref[j]) + _dot_tn(lo, e_ref[j])
    c_ref[...] = spread(0) + one_ref[...]
    s1_ref[...] = spread(1)
    s2_ref[...] = spread(2)


def _rope_tables(positions):
    B, S = positions.shape
    half = ROT_DIM // 2
    inv_freq = ROPE_THETA ** (-jnp.arange(half, dtype=F32) * (2.0 / ROT_DIM))
    pos = positions.astype(F32).reshape(B, 1, S)
    lane = np.arange(LANES)
    in_head, freq = lane % HEAD_DIM, lane % half
    e = np.zeros((3, 2 * half, LANES), np.float32)
    e[0, freq, lane] = in_head < ROT_DIM
    e[1, half + freq, lane] = -1.0 * (in_head < half)
    e[2, half + freq, lane] = 1.0 * ((in_head >= half) & (in_head < ROT_DIM))
    ones = (in_head >= ROT_DIM).astype(np.float32).reshape(1, LANES)
    sh = jax.ShapeDtypeStruct((B * S, LANES), F32)
    spec = pl.BlockSpec((S, LANES), lambda b: (b, 0))
    return pl.pallas_call(
        _rope_table_kernel, grid=(B,),
        in_specs=[pl.BlockSpec((1, 1, S), lambda b: (b, 0, 0)), pl.BlockSpec((half, 1), lambda b: (0, 0)),
                  pl.BlockSpec((3, 2 * half, LANES), lambda b: (0, 0, 0)), pl.BlockSpec((1, LANES), lambda b: (0, 0))],
        out_specs=[spec, spec, spec], out_shape=[sh, sh, sh],
        compiler_params=_cparams("parallel"), name="rope_tables",
    )(pos, inv_freq.reshape(half, 1), jnp.asarray(e, dtype=BF16), jnp.asarray(ones))


def _tail_kernel(h_ref, a1_ref, a2_ref, w1_ref, w2_ref, nwm_ref, wu_ref, wd_ref, p_ref, nwp_ref, wg_ref, wp_ref,
                 fw_ref, o_ref, xn_ref, acc_ref, *, final):
    f = pl.program_id(1)

    @pl.when(f == 0)
    def _():
        hm = h_ref[...] + _dot(a1_ref[...], w1_ref[...]) + _dot(a2_ref[...], w2_ref[...])
        acc_ref[...] = hm
        xn_ref[...] = _rms(hm, nwm_ref[...]).astype(BF16)

    u = jnp.maximum(_dot(xn_ref[...], wu_ref[...]), 0.0)
    acc_ref[...] += _dot((u * u).astype(BF16), wd_ref[...])

    @pl.when(f == pl.num_programs(1) - 1)
    def _():
        h2 = acc_ref[...]
        gate = _sigmoid(_dot(_rms(h2, nwp_ref[...]).astype(BF16), wg_ref[...]))
        y = h2 + gate * _dot(p_ref[...].astype(BF16), wp_ref[...])
        if final:
            y = _rms(y, fw_ref[...])
        o_ref[...] = y


def _layer_tail(h, a1, a2, w1, w2, a2_time_major, nwm, wu, wd, p, nwp, wg, wp, fw, final, tm=512, tf=4096):
    T, D = h.shape
    FF = wu.shape[1]
    P = p.shape[1]
    tm = min(tm, T)
    K1, K2 = w1.shape[0], w2.shape[0]
    if a2_time_major is None:
        a2_spec = pl.BlockSpec((tm, K2), lambda i, f: (i, 0))
    else:
        nst = a2_time_major[1] // tm
        a2_spec = pl.BlockSpec((tm, K2), lambda i, f: (i % nst, i // nst))
    row = lambda w: pl.BlockSpec((tm, w), lambda i, f: (i, 0))
    full = lambda r, c: pl.BlockSpec((r, c), lambda i, f: (0, 0))
    return pl.pallas_call(
        functools.partial(_tail_kernel, final=final), grid=(T // tm, FF // tf),
        in_specs=[row(D), row(K1), a2_spec, full(K1, D), full(K2, D), full(1, D),
                  pl.BlockSpec((D, tf), lambda i, f: (0, f)), pl.BlockSpec((tf, D), lambda i, f: (f, 0)),
                  row(P), full(1, D), full(D, D), full(P, D), full(1, D)],
        out_specs=row(D),
        out_shape=jax.ShapeDtypeStruct((T, D), F32),
        scratch_shapes=[pltpu.VMEM((tm, D), BF16), pltpu.VMEM((tm, D), F32)],
        compiler_params=_cparams("parallel", "arbitrary"), name="layer_tail",
    )(h, a1, a2, w1, w2, nwm.reshape(1, D), wu, wd, p, nwp.reshape(1, D), wg, wp, fw.reshape(1, D))


def _fox_gate_kernel(x_ref, b_ref, o_ref, carry_ref, *, ts):
    @pl.when(pl.program_id(1) == 0)
    def _():
        carry_ref[...] = jnp.zeros_like(carry_ref)

    r = lax.broadcasted_iota(jnp.int32, (LANES, LANES), 0)
    c = lax.broadcasted_iota(jnp.int32, (LANES, LANES), 1)
    lower = (r >= c).astype(F32)
    lane = lax.broadcasted_iota(jnp.int32, (LANES, LANES), 1)

    nblk = ts // LANES
    within = []
    for n in range(nblk):
        x = x_ref[n * LANES:(n + 1) * LANES, :] + b_ref[...]
        logf = jnp.minimum(x, 0.0) - jnp.log1p(jnp.exp(-jnp.abs(x)))
        within.append(_dot(lower, logf, precision=HIGHEST))
    carry = carry_ref[0:1, :]
    for n in range(nblk):
        cum = within[n] + carry
        carry = cum[LANES - 1:LANES, :]
        f = cum * LOG2E
        hi = f.astype(BF16).astype(F32)
        mid = (f - hi).astype(BF16).astype(F32)
        lo = f - hi - mid
        for h in range(FOX_HEADS):
            v = jnp.where(lane == HEAD_DIM, hi[:, h:h + 1],
                          jnp.where(lane == HEAD_DIM + 1, mid[:, h:h + 1],
                                    jnp.where(lane == HEAD_DIM + 2, lo[:, h:h + 1], 0.0)))
            o_ref[n * LANES:(n + 1) * LANES, h * LANES:(h + 1) * LANES] = v.astype(o_ref.dtype)
    carry_ref[0:1, :] = carry


def _fox_gate(sm, bias, B, S, ts=1024):
    T = B * S
    ts = min(ts, S)
    ns = S // ts
    b = jnp.zeros((1, LANES), F32).at[0, :FOX_HEADS].set(bias.astype(F32))
    return pl.pallas_call(
        functools.partial(_fox_gate_kernel, ts=ts), grid=(B, ns),
        in_specs=[pl.BlockSpec((ts, LANES), lambda bb, n: (bb * ns + n, 0)),
                  pl.BlockSpec((1, LANES), lambda bb, n: (0, 0))],
        out_specs=pl.BlockSpec((ts, FOX_HEADS * LANES), lambda bb, n: (bb * ns + n, 0)),
        out_shape=jax.ShapeDtypeStruct((T, FOX_HEADS * LANES), BF16),
        scratch_shapes=[pltpu.VMEM((SUBLANES, LANES), F32)],
        compiler_params=_cparams("parallel", "arbitrary"), name="fox_gate",
    )(sm, b)


def _fox_attn_kernel(q_ref, k_ref, v_ref, f_ref, o_ref, *, tq, tk, dk, hp):
    i = pl.program_id(2)
    qs = [q_ref[:, hh * LANES:(hh + 1) * LANES] for hh in range(hp)]
    nsub = tq // tk

    def scores(start, size, r0):
        rows = pl.ds(start, size)
        return [_dot_nt(qs[hh][r0:], k_ref[rows, hh * LANES:(hh + 1) * LANES] + f_ref[rows, hh * LANES:(hh + 1) * LANES])
                for hh in range(hp)]

    def finish(ss, start, size, r0, carry, masked):
        rows = pl.ds(start, size)
        out = []
        for hh in range(hp):
            m, acc = carry[hh]
            s = ss[hh]
            if masked:
                r = lax.broadcasted_iota(jnp.int32, (tq - r0, size), 0)
                c = lax.broadcasted_iota(jnp.int32, (tq - r0, size), 1)
                s = jnp.where(c <= r, s, NEG)
            m_new = jnp.maximum(m[r0:], jnp.max(s, axis=1, keepdims=True))
            pr = jnp.exp2(s - m_new).astype(BF16)
            acc_new = jnp.exp2(m[r0:] - m_new) * acc[r0:] + _dot(pr, v_ref[rows, hh * LANES:(hh + 1) * LANES])
            if r0:
                m_new = jnp.concatenate([m[:r0], m_new], axis=0)
                acc_new = jnp.concatenate([acc[:r0], acc_new], axis=0)
            out.append((m_new, acc_new))
        return tuple(out)

    def tile(start, size, r0, carry, masked):
        return finish(scores(start, size, r0), start, size, r0, carry, masked)

    init = tuple((jnp.full((tq, 1), NEG, F32), jnp.zeros((tq, LANES), F32)) for _ in range(hp))
    carry = lax.fori_loop(0, i * nsub, lambda j, c: tile(pl.multiple_of(j * tk, tk), tk, 0, c, False), init)
    dstarts = [pl.multiple_of(i * tq + d * dk, dk) for d in range(tq // dk)]
    dss = [scores(dstarts[d], dk, d * dk) for d in range(tq // dk)]
    for d in range(tq // dk):
        carry = finish(dss[d], dstarts[d], dk, d * dk, carry, True)
    o_ref[...] = jnp.concatenate([acc[:, :HEAD_DIM] / acc[:, HEAD_DIM:HEAD_DIM + 1] for (_, acc) in carry],
                                 axis=1).astype(o_ref.dtype)


def _fox_attn(q, k, v, f3, B, S, tq=1024, tk=1024, dk=512, hp=2):
    T = B * S
    tq = min(tq, S)
    tk = min(tk, tq)
    dk = min(dk, tq)
    nq = S // tq
    bw = hp * LANES
    ng = FOX_HEADS // hp
    res = pl.BlockSpec((S, bw), lambda b, p, i: (b, p))
    return pl.pallas_call(
        functools.partial(_fox_attn_kernel, tq=tq, tk=tk, dk=dk, hp=hp), grid=(B, ng, nq),
        in_specs=[pl.BlockSpec((tq, bw), lambda b, p, i: (b * nq + i, p)), res, res, res],
        out_specs=pl.BlockSpec((tq, hp * HEAD_DIM), lambda b, p, i: (b * nq + i, p)),
        out_shape=jax.ShapeDtypeStruct((T, FOX_HEADS * HEAD_DIM), BF16),
        compiler_params=_cparams("parallel", "parallel", "arbitrary"), name="fox_attn",
    )(q, k, v, f3)


def _cumsum_rows(x):
    n = x.shape[0]
    row = lax.broadcasted_iota(jnp.int32, x.shape, 0)
    d = 1
    while d < n:
        x = x + jnp.where(row >= d, pltpu.roll(x, d, 0), 0.0)
        d *= 2
    return x


def _gdn_kernel(x_ref, z_ref, sm_ref, cw_ref, par_ref, nw_ref, o_ref, state_ref, xp_ref, *, cps):
    C = GDN_CHUNK
    D = GDN_DIM
    W = GDN_HEADS * D
    R = cps * C
    n = pl.program_id(1)

    @pl.when(n == 0)
    def _():
        state_ref[...] = jnp.zeros_like(state_ref)
        xp_ref[0:SUBLANES, :] = jnp.zeros((SUBLANES, 3 * W), F32)

    x = x_ref[...]
    xp_ref[SUBLANES:SUBLANES + R, :] = x
    cw = cw_ref[...]
    y = x * cw[CONV_W - 1:CONV_W]
    for j in range(CONV_W - 1):
        off = SUBLANES - (CONV_W - 1) + j
        y = y + xp_ref[off:off + R, :] * cw[j:j + 1]
    xp_ref[0:SUBLANES, :] = x[R - SUBLANES:R]
    y = _silu(y)

    sm = sm_ref[...]
    beta_all = _sigmoid(sm)
    g_all = -jnp.exp(par_ref[0:1, :]) * _softplus(sm + par_ref[1:2, :])
    ri = lax.broadcasted_iota(jnp.int32, (C, C), 0)
    ci = lax.broadcasted_iota(jnp.int32, (C, C), 1)
    incl = ri >= ci
    strict = ri > ci

    chains = [(c, h) for c in range(cps) for h in range(GDN_HEADS)]
    gcs = [_cumsum_rows(g_all[c * C:(c + 1) * C]) for c in range(cps)]
    gcts = [gc.T for gc in gcs]
    pre = {}
    for (c, h) in chains:
        r0 = c * C
        q = y[r0:r0 + C, h * D:(h + 1) * D]
        k = y[r0:r0 + C, W + h * D:W + (h + 1) * D]
        v = y[r0:r0 + C, 2 * W + h * D:2 * W + (h + 1) * D]
        qn = q * lax.rsqrt(jnp.sum(q * q, axis=-1, keepdims=True) + 1e-6) * (D ** -0.5)
        kn = k * lax.rsqrt(jnp.sum(k * k, axis=-1, keepdims=True) + 1e-6)
        beta = beta_all[r0:r0 + C, 8 + h:9 + h]
        gcol = gcs[c][:, 12 + h:13 + h]
        grow = gcts[c][12 + h:13 + h, :]
        decay = jnp.where(incl, jnp.exp(jnp.where(incl, gcol - grow, 0.0)), 0.0)
        kb = kn * beta
        g_last = gcol[C - 1:C, :]
        pre[c, h] = dict(qn16=qn.astype(BF16), kn16=kn.astype(BF16), kb16=kb.astype(BF16), decay=decay,
                         rhs=jnp.concatenate([v * beta, kb * jnp.exp(gcol)], axis=1),
                         qg16=(qn * jnp.exp(gcol)).astype(BF16),
                         kg16=(kn * jnp.exp(g_last - gcol)).astype(BF16), e_last=jnp.exp(g_last))
    kk = {ch: _dot_nt(pre[ch]["kb16"], pre[ch]["kn16"]) for ch in chains}
    qk = {ch: _dot_nt(pre[ch]["qn16"], pre[ch]["kn16"]) for ch in chains}
    P = {ch: -jnp.where(strict, kk[ch] * pre[ch]["decay"], 0.0) for ch in chains}
    X = dict(P)
    for _ in range(5):
        P = {ch: _dot(P[ch].astype(BF16), P[ch].astype(BF16)) for ch in chains}
        XP = {ch: _dot(X[ch].astype(BF16), P[ch].astype(BF16)) for ch in chains}
        X = {ch: X[ch] + P[ch] + XP[ch] for ch in chains}
    sol = {ch: pre[ch]["rhs"] + _dot(X[ch].astype(BF16), pre[ch]["rhs"].astype(BF16)) for ch in chains}
    attn16 = {ch: jnp.where(incl, qk[ch] * pre[ch]["decay"], 0.0).astype(BF16) for ch in chains}

    heads = range(GDN_HEADS)
    states = [state_ref[h] for h in heads]
    for c in range(cps):
        r0 = c * C
        st16 = [states[h].astype(BF16) for h in heads]
        ws = [_dot(sol[c, h][:, D:].astype(BF16), st16[h]) for h in heads]
        qs = [_dot(pre[c, h]["qg16"], st16[h]) for h in heads]
        v_new = [(sol[c, h][:, :D] - ws[h]).astype(BF16) for h in heads]
        av = [_dot(attn16[c, h], v_new[h]) for h in heads]
        kv = [_dot_tn(pre[c, h]["kg16"], v_new[h]) for h in heads]
        outs = []
        for h in heads:
            states[h] = states[h] * pre[c, h]["e_last"] + kv[h]
            o = _rms(qs[h] + av[h], nw_ref[...]) * _silu(z_ref[r0:r0 + C, h * D:(h + 1) * D])
            outs.append(o)
        o_ref[r0:r0 + C, :] = jnp.concatenate(outs, axis=1).astype(o_ref.dtype)
    for h in heads:
        state_ref[h] = states[h]


def _gdn(gqkv, gz, small, conv_w, a_log, dt_bias, norm_w, B, S, cps=8):
    T = B * S
    C = GDN_CHUNK
    R = cps * C
    N = S // R
    W = GDN_HEADS * GDN_DIM
    par = jnp.zeros((SUBLANES, LANES), F32)
    par = par.at[0, 12:16].set(a_log.astype(F32)).at[1, 12:16].set(dt_bias.astype(F32))
    return pl.pallas_call(
        functools.partial(_gdn_kernel, cps=cps), grid=(B, N),
        in_specs=[pl.BlockSpec((R, 3 * W), lambda b, n: (b * N + n, 0)),
                  pl.BlockSpec((R, W), lambda b, n: (b * N + n, 0)),
                  pl.BlockSpec((R, LANES), lambda b, n: (b * N + n, 0)),
                  pl.BlockSpec((CONV_W, 3 * W), lambda b, n: (0, 0)),
                  pl.BlockSpec((SUBLANES, LANES), lambda b, n: (0, 0)),
                  pl.BlockSpec((1, GDN_DIM), lambda b, n: (0, 0))],
        out_specs=pl.BlockSpec((R, W), lambda b, n: (b * N + n, 0)),
        out_shape=jax.ShapeDtypeStruct((T, W), BF16),
        scratch_shapes=[pltpu.VMEM((GDN_HEADS, GDN_DIM, GDN_DIM), F32),
                        pltpu.VMEM((SUBLANES + R, 3 * W), F32)],
        compiler_params=_cparams("parallel", "arbitrary"), name="gdn",
    )(gqkv, gz, small, conv_w.astype(F32), par, norm_w.reshape(1, GDN_DIM).astype(F32))


def _compress_kernel(x_ref, pe_ref, w1_ref, wbd_ref, w2_ref, o_ref, *, nh):
    ab = jnp.zeros((nh, NSA_GROUPS * 2 * CMP_HIDDEN), F32)
    for l in range(CMP_STRIDE):
        ab = ab + _dot(x_ref[pl.ds(l, nh, stride=CMP_STRIDE), :].astype(BF16), wbd_ref[0, l])
    lane = lax.broadcasted_iota(jnp.int32, (nh, HEAD_DIM), 1)
    one_col = jnp.where(lane == 0, 1.0, 0.0)
    c = _dot(pe_ref[0].astype(BF16), w1_ref[0])
    for g in range(NSA_GROUPS):
        a = ab[:, 2 * g * CMP_HIDDEN:(2 * g + 1) * CMP_HIDDEN]
        b = ab[:, (2 * g + 1) * CMP_HIDDEN:(2 * g + 2) * CMP_HIDDEN]
        hid = _gelu_tanh(a + pltpu.roll(b, nh - 1, 0) + c[0:1])
        out = _dot(hid.astype(BF16), w2_ref[0])
        o_ref[0, g] = jnp.concatenate([out, one_col], axis=1).astype(o_ref.dtype)


def _compress(kc, vc, pe, w1, w2, B, S):
    nh = S // CMP_STRIDE
    G = NSA_GROUPS
    half = CMP_STRIDE * HEAD_DIM
    w1r = w1.reshape(2, 2, CMP_STRIDE, HEAD_DIM, CMP_HIDDEN)
    per_tok = jnp.concatenate([w1r[:, 0], w1r[:, 1]], axis=-1)
    eye = jnp.eye(G, dtype=w1.dtype)
    wbd = per_tok[:, :, None, :, None, :] * eye[None, None, :, None, :, None]
    wbd = wbd.reshape(2, CMP_STRIDE, G * HEAD_DIM, G * 2 * CMP_HIDDEN)
    x_spec = pl.BlockSpec((S, G * HEAD_DIM), lambda b, j: (b, 0))
    out = [pl.pallas_call(
        functools.partial(_compress_kernel, nh=nh), grid=(B, 1),
        in_specs=[x_spec, pl.BlockSpec((1, SUBLANES, 2 * half), lambda b, j, kv=kv: (kv, 0, 0)),
                  pl.BlockSpec((1, 2 * half, CMP_HIDDEN), lambda b, j, kv=kv: (kv, 0, 0)),
                  pl.BlockSpec((1, CMP_STRIDE, G * HEAD_DIM, G * 2 * CMP_HIDDEN), lambda b, j, kv=kv: (kv, 0, 0, 0)),
                  pl.BlockSpec((1, CMP_HIDDEN, HEAD_DIM), lambda b, j, kv=kv: (kv, 0, 0))],
        out_specs=pl.BlockSpec((1, G, nh, LANES), lambda b, j: (b, 0, 0, 0)),
        out_shape=jax.ShapeDtypeStruct((B, G, nh, LANES), BF16),
        compiler_params=_cparams("parallel", "arbitrary"), name="nsa_compress",
    )(x, pe, w1, wbd, w2) for kv, x in enumerate((kc, vc))]
    return out


def _stack_heads(qb):
    return jnp.concatenate([qb[:, h * HEAD_DIM:(h + 1) * HEAD_DIM] for h in range(NSA_HPG)], axis=0)


def _unstack_heads(o, tq):
    return jnp.concatenate([o[h * tq:(h + 1) * tq] for h in range(NSA_HPG)], axis=1)


def _nsa_cmp_kernel(q_ref, kc_ref, vc_ref, ov_ref, oc_ref, bias_ref, *, tq, n_sel, top_k):
    i = pl.program_id(1)
    G = NSA_GROUPS
    GW = NSA_HPG * HEAD_DIM
    ncp = kc_ref.shape[2]
    row = lax.broadcasted_iota(jnp.int32, (NSA_HPG * tq, 1), 0)
    t4 = i * tq + (row & (tq - 1))
    cmp_end = lax.broadcasted_iota(jnp.int32, (1, ncp), 1) * CMP_STRIDE + (CMP_BLOCK - 1)
    valid = cmp_end <= t4
    ss = [_dot_nt(_stack_heads(q_ref[:, g * GW:(g + 1) * GW]), kc_ref[0, g, :, :HEAD_DIM]) for g in range(G)]
    ps, accs = [], []
    for g in range(G):
        s = jnp.where(valid, ss[g], NEG)
        m = jnp.max(s, axis=1, keepdims=True)
        p = jnp.exp2(s - m)
        ps.append(p)
        accs.append(_dot(p.astype(BF16), vc_ref[0, g]))
    imps = []
    for g in range(G):
        inv = jnp.where(t4 >= CMP_BLOCK - 1, 1.0 / accs[g][:, HEAD_DIM:HEAD_DIM + 1], 0.0)
        oc_ref[:, g * GW:(g + 1) * GW] = _unstack_heads(accs[g][:, :HEAD_DIM] * inv, tq)
        p = ps[g] * inv
        psum = p[0:tq] + p[tq:2 * tq] + p[2 * tq:3 * tq] + p[3 * tq:4 * tq]
        imps.append(_dot(psum, ov_ref[...], precision=HIGHEST))
    blk = lax.broadcasted_iota(jnp.int32, (1, LANES), 1)
    t = i * tq + lax.broadcasted_iota(jnp.int32, (tq, 1), 0)
    cur = t >> (SEL_BLOCK.bit_length() - 1)
    forced = (blk == 0) | (blk == cur) | (blk == cur - 1)
    future = blk * SEL_BLOCK > t
    blk_t = lax.broadcasted_iota(jnp.int32, (LANES, tq), 0).astype(F32)
    for g in range(G):
        imp = jnp.where(future, NEG, jnp.where(forced, -NEG, imps[g]))
        imp = jnp.where(blk < n_sel, imp, REMOVED)
        imp_t = imp.T
        for _ in range(top_k):
            mx = jnp.max(imp_t, axis=0, keepdims=True)
            first = jnp.min(jnp.where(imp_t == mx, blk_t, float(LANES)), axis=0, keepdims=True)
            imp_t = jnp.where(blk_t == first, REMOVED, imp_t)
        sel = jnp.where(imp_t == REMOVED, 1.0, 0.0).T
        bias_ref[0, g] = jnp.where((sel > 0.0) & jnp.logical_not(future), 0.0, NEG).astype(bias_ref.dtype)


def _nsa_cmp(q, kc, vc, overlap, B, S, tq=512):
    T = B * S
    tq = min(tq, S)
    nq = S // tq
    ncp = kc.shape[2]
    n_sel = S // SEL_BLOCK
    QW = NSA_HEADS * HEAD_DIM
    G = NSA_GROUPS
    return pl.pallas_call(
        functools.partial(_nsa_cmp_kernel, tq=tq, n_sel=n_sel, top_k=min(SEL_TOPK, n_sel)),
        grid=(B, nq),
        in_specs=[pl.BlockSpec((tq, QW), lambda b, i: (b * nq + i, 0)),
                  pl.BlockSpec((1, G, ncp, LANES), lambda b, i: (b, 0, 0, 0)),
                  pl.BlockSpec((1, G, ncp, LANES), lambda b, i: (b, 0, 0, 0)),
                  pl.BlockSpec((ncp, LANES), lambda b, i: (0, 0))],
        out_specs=[pl.BlockSpec((tq, QW), lambda b, i: (b * nq + i, 0)),
                   pl.BlockSpec((1, G, tq, LANES), lambda b, i: (b, 0, i, 0))],
        out_shape=[jax.ShapeDtypeStruct((T, QW), F32),
                   jax.ShapeDtypeStruct((B, G, S, LANES), BF16)],
        compiler_params=_cparams("parallel", "parallel"), name="nsa_cmp",
    )(q, kc, vc, overlap)


def _nsa_main_kernel(q_ref, bias_ref, ks_ref, oh_ref, vs_ref, kw_ref, vw_ref, oc_ref, g_ref, o_ref, *, qn, tk, wspan):
    R = NSA_HPG * qn
    tb = 2 * tk
    i = pl.program_id(2)
    q4 = _stack_heads(q_ref[...])
    q4p = jnp.concatenate([q4, jnp.zeros_like(q4)], axis=1)
    b4 = jnp.concatenate([bias_ref[0, 0]] * NSA_HPG, axis=0)
    qa = jnp.concatenate([q4p, b4], axis=1)
    t4 = i * qn + (lax.broadcasted_iota(jnp.int32, (R, 1), 0) & (qn - 1))

    def scores(start, size):
        rows = pl.ds(start, size)
        ka = jnp.concatenate([ks_ref[rows, :], oh_ref[rows, :]], axis=1)
        return _dot_nt(qa, ka)

    def finish(s, start, size, carry, masked):
        m, acc = carry
        if masked:
            kpos = start + lax.broadcasted_iota(jnp.int32, (1, size), 1)
            s = jnp.where(kpos <= t4, s, NEG)
        m_new = jnp.maximum(m, jnp.max(s, axis=1, keepdims=True))
        pr = jnp.exp2(s - m_new).astype(BF16)
        acc = jnp.exp2(m - m_new) * acc + _dot(pr, vs_ref[pl.ds(start, size), :])
        return m_new, acc

    def step(start, size, carry, masked):
        return finish(scores(start, size), start, size, carry, masked)

    first = i * qn
    nbig = first // tb
    diag = (first // tk) * tk
    nsmall = (diag - nbig * tb) // tk
    init = (jnp.full((R, 1), NEG, F32), jnp.zeros((R, LANES), F32))
    carry = lax.fori_loop(0, nbig, lambda j, c: step(pl.multiple_of(j * tb, tb), tb, c, False), init)
    carry = lax.fori_loop(0, nsmall, lambda j, c: step(pl.multiple_of(nbig * tb, tk), tk, c, False), carry)
    dstart = pl.multiple_of(diag, tk)
    s_d = scores(dstart, tk)
    wrows = pl.ds(pl.multiple_of(jnp.maximum(first + qn - wspan, 0), qn), wspan)
    s_w = _dot_nt(q4p, kw_ref[wrows, :])
    _, acc_s = finish(s_d, dstart, tk, carry, True)
    kpos = jnp.maximum(first + qn - wspan, 0) + lax.broadcasted_iota(jnp.int32, (1, wspan), 1)
    wmask = lax.bitcast_convert_type(t4 - kpos, jnp.uint32) < jnp.uint32(WINDOW)
    s_w = jnp.where(wmask, s_w, NEG)
    m_w = jnp.max(s_w, axis=1, keepdims=True)
    acc_w = _dot(jnp.exp2(s_w - m_w).astype(BF16), vw_ref[wrows, :])

    gates = _sigmoid(g_ref[...])
    first_group = pl.program_id(1) == 0
    ng = NSA_HPG * 3

    def gate(h, branch):
        j = 3 * h + branch
        return jnp.where(first_group, gates[:, j:j + 1], gates[:, ng + j:ng + j + 1])

    oc = oc_ref[...]
    outs = []
    for h in range(NSA_HPG):
        hs = slice(h * qn, (h + 1) * qn)
        c_s = gate(h, 1) / acc_s[hs, HEAD_DIM:HEAD_DIM + 1]
        c_w = gate(h, 2) / acc_w[hs, HEAD_DIM:HEAD_DIM + 1]
        outs.append(gate(h, 0) * oc[:, h * HEAD_DIM:(h + 1) * HEAD_DIM]
                    + c_s * acc_s[hs, :HEAD_DIM] + c_w * acc_w[hs, :HEAD_DIM])
    o_ref[...] = jnp.concatenate(outs, axis=1).astype(o_ref.dtype)


def _nsa_main(q, bias, ksp, onehot, vsa, kwp, vwa, oc, gates, B, S, qn=256, tk=512):
    T = B * S
    nq = S // qn
    tk = min(tk, S // 2)
    wspan = min(WINDOW + qn, S)
    GW = NSA_HPG * HEAD_DIM
    res = pl.BlockSpec((S, LANES), lambda b, g, i: (b, g))
    return pl.pallas_call(
        functools.partial(_nsa_main_kernel, qn=qn, tk=tk, wspan=wspan), grid=(B, NSA_GROUPS, nq),
        in_specs=[pl.BlockSpec((qn, GW), lambda b, g, i: (b * nq + i, g)),
                  pl.BlockSpec((1, 1, qn, LANES), lambda b, g, i: (b, g, i, 0)),
                  res, pl.BlockSpec((S, LANES), lambda b, g, i: (0, 0)), res, res, res,
                  pl.BlockSpec((qn, GW), lambda b, g, i: (b * nq + i, g)),
                  pl.BlockSpec((qn, LANES), lambda b, g, i: (b * nq + i, 0))],
        out_specs=pl.BlockSpec((qn, GW), lambda b, g, i: (b * nq + i, g)),
        out_shape=jax.ShapeDtypeStruct((T, NSA_HEADS * HEAD_DIM), BF16),
        compiler_params=_cparams("parallel", "parallel", "arbitrary"), name="nsa_main",
    )(q, bias, ksp, onehot, vsa, kwp, vwa, oc, gates)


def _lru_kernel(gx_ref, cw_ref, cb_ref, wab_ref, bab_ref, lam_ref, o_ref, h_ref, tail_ref, a_ref, b_ref, *, ts):
    n = pl.program_id(0)
    Bb = gx_ref.shape[1]
    W = gx_ref.shape[2] // 2

    @pl.when(n == 0)
    def _():
        h_ref[...] = jnp.zeros_like(h_ref)
        tail_ref[...] = jnp.zeros_like(tail_ref)

    xin = gx_ref[:, :, W:]
    xp = jnp.concatenate([tail_ref[...], xin], axis=0)
    cw = cw_ref[...]
    x = cb_ref[...].reshape(1, 1, W)
    for j in range(CONV_W):
        x = x + xp[j:j + ts] * cw[j:j + 1].reshape(1, 1, W)
    tail_ref[...] = xin[ts - (CONV_W - 1):ts]
    x2 = x.reshape(ts * Bb, W)
    pre = _dot(x2.astype(BF16), wab_ref[...]) + bab_ref[...]
    r = _sigmoid(pre[:, :W])
    ig = _sigmoid(pre[:, W:])
    log_a = (-RG_C * _softplus(-lam_ref[...])) * r
    a = jnp.exp(log_a)
    th = jnp.tanh(log_a)
    bb = jnp.sqrt(-2.0 * th / (1.0 - th)) * (ig * x2)
    a_ref[...] = a.reshape(ts, Bb, W)
    b_ref[...] = bb.reshape(ts, Bb, W)

    def scan(t, h):
        h = a_ref[t] * h + b_ref[t]
        b_ref[t] = h
        return h

    h_ref[...] = lax.fori_loop(0, ts, scan, h_ref[...], unroll=8)
    o_ref[...] = (b_ref[...] * _gelu_tanh(gx_ref[:, :, :W])).astype(o_ref.dtype)


def _lru(gx_t, conv_w, conv_b, wab, bab, lam, ts=128):
    S, Bb, W2 = gx_t.shape
    W = W2 // 2
    ts = min(ts, S)
    full = lambda shape: pl.BlockSpec(shape, lambda n: (0,) * len(shape))
    return pl.pallas_call(
        functools.partial(_lru_kernel, ts=ts), grid=(S // ts,),
        in_specs=[pl.BlockSpec((ts, Bb, W2), lambda n: (n, 0, 0)),
                  full((CONV_W, W)), full((1, W)), full((W, 2 * W)), full((1, 2 * W)), full((1, W))],
        out_specs=pl.BlockSpec((ts, Bb, W), lambda n: (n, 0, 0)),
        out_shape=jax.ShapeDtypeStruct((S, Bb, W), BF16),
        scratch_shapes=[pltpu.VMEM((Bb, W), F32), pltpu.VMEM((CONV_W - 1, Bb, W), F32),
                        pltpu.VMEM((ts, Bb, W), F32), pltpu.VMEM((ts, Bb, W), F32)],
        compiler_params=_cparams("arbitrary"), name="rg_lru",
    )(gx_t, conv_w, conv_b, wab, bab, lam)


def _even_mixer(h, B, S, norm_w, w_in, fox_bf, conv_w, a_log, dt_bias, gdn_norm_w, w_out):
    T, D = h.shape
    FW = FOX_HEADS * HEAD_DIM
    GW = GDN_HEADS * GDN_DIM
    o_ff = 3 * FW
    o_g = o_ff + FOX_HEADS
    o_gb = o_g + 4 * GW
    small = jnp.concatenate([w_in[:, o_ff:o_g], w_in[:, o_gb:o_gb + 2 * GDN_HEADS],
                             jnp.zeros((D, LANES - FOX_HEADS - 2 * GDN_HEADS), w_in.dtype)], axis=1)
    w = jnp.concatenate([w_in[:, :o_ff], w_in[:, o_g:o_gb], small], axis=1).astype(BF16)
    segs = [(0, 0, 0, FW, False, HEAD_DIM ** -0.5 * LOG2E, tuple((j, -1.0) for j in range(FOX_F_PIECES))),
            (1, 0, FW, FW, False, 1.0, ()),
            (2, 0, 2 * FW, FW, False, 1.0, ((0, 1.0),)),
            (3, 0, 3 * FW, 3 * GW, False, 1.0, None),
            (4, 0, 3 * FW + 3 * GW, GW, False, 1.0, None),
            (5, 0, 3 * FW + 4 * GW, LANES, False, 1.0, None)]
    fq, fk, fv, gqkv, gz, sm = _norm_proj(h, norm_w, w, segs,
                                          [(2 * FW, BF16), (2 * FW, BF16), (2 * FW, BF16),
                                           (3 * GW, F32), (GW, F32), (LANES, F32)])
    f3 = _fox_gate(sm, fox_bf, B, S)
    fox = _fox_attn(fq, fk, fv, f3, B, S)
    gdn = _gdn(gqkv, gz, sm, conv_w, a_log, dt_bias, gdn_norm_w, B, S)
    wo = w_out.astype(BF16)
    return fox, gdn, wo[:FW], wo[FW:], None


def _odd_mixer(h, B, S, rope_tabs, norm_w, w_in, k_pe, k_w1, k_w2, v_pe, v_w1, v_w2,
               conv_w, conv_b, wa, ba, wx, bx, lam, w_out):
    T, D = h.shape
    QW = NSA_HEADS * HEAD_DIM
    KW = NSA_GROUPS * HEAD_DIM
    G = NSA_GROUPS
    o_ng = QW + 6 * KW
    ngw = NSA_HEADS * 3
    w = jnp.concatenate([w_in[:, :o_ng], w_in[:, o_ng:o_ng + ngw], jnp.zeros((D, LANES - ngw), w_in.dtype),
                         w_in[:, o_ng + ngw:]], axis=1).astype(BF16)
    ones = ((0, 1.0),)
    segs = [(0, 0, 0, QW, True, HEAD_DIM ** -0.5 * LOG2E, None),
            (1, 0, QW, KW, True, 1.0, None),
            (2, 0, QW + KW, KW, False, 1.0, None),
            (3, 0, QW + 2 * KW, KW, True, 1.0, ()),
            (4, 0, QW + 3 * KW, KW, False, 1.0, ones),
            (5, 0, QW + 4 * KW, KW, True, 1.0, ()),
            (6, 0, QW + 5 * KW, KW, False, 1.0, ones),
            (7, 0, o_ng, LANES, False, 1.0, None),
            (8, 0, o_ng + LANES, 2 * LRU_W, False, 1.0, None)]
    q, kc, vc, ksp, vsa, kwp, vwa, ng, rgx_t = _norm_proj(
        h, norm_w, w, segs,
        [(QW, BF16), (KW, F32), (KW, F32), (2 * KW, BF16), (2 * KW, BF16), (2 * KW, BF16), (2 * KW, BF16),
         (LANES, F32), (2 * LRU_W, F32, (B, S))],
        rope_tabs=rope_tabs)
    nh = S // CMP_STRIDE
    pe = jnp.stack([k_pe.reshape(-1), v_pe.reshape(-1)]).astype(F32)
    pe = jnp.broadcast_to(pe[:, None, :], (2, SUBLANES, pe.shape[-1]))
    w1 = jnp.stack([k_w1, v_w1]).astype(BF16)
    w2 = jnp.stack([k_w2, v_w2]).astype(BF16)
    kcmp, vcmp = _compress(kc, vc, pe, w1, w2, B, S)
    n_cmp = (S - CMP_BLOCK) // CMP_STRIDE + 1
    ncp = -(-nh // LANES) * LANES
    if ncp != nh:
        kcmp, vcmp = (jnp.pad(t, ((0, 0), (0, 0), (0, ncp - nh), (0, 0))) for t in (kcmp, vcmp))
    n_sel = S // SEL_BLOCK
    cs = np.arange(ncp) * CMP_STRIDE
    ss = np.arange(LANES) * SEL_BLOCK
    ov = ((cs[:, None] <= ss[None, :] + SEL_BLOCK - 1) & (cs[:, None] + CMP_BLOCK - 1 >= ss[None, :])
          & (np.arange(ncp)[:, None] < n_cmp) & (np.arange(LANES)[None, :] < n_sel))
    overlap = jnp.asarray(ov.astype(np.float32))
    oc, bias = _nsa_cmp(q, kcmp, vcmp, overlap, B, S)
    blk_of_pos = np.arange(S) // SEL_BLOCK
    onehot = jnp.asarray((blk_of_pos[:, None] == np.arange(LANES)[None, :]).astype(np.float32), dtype=BF16)
    nsa = _nsa_main(q, bias, ksp, onehot, vsa, kwp, vwa, oc, ng, B, S)
    nblk, bw, _ = wa.shape
    eye = jnp.eye(nblk, dtype=wa.dtype)
    dense = lambda wb: (eye[:, None, :, None] * wb[:, :, None, :]).reshape(nblk * bw, nblk * bw)
    wab = jnp.concatenate([dense(wa), dense(wx)], axis=1).astype(BF16)
    bab = jnp.concatenate([ba, bx]).reshape(1, 2 * LRU_W).astype(F32)
    lru_t = _lru(rgx_t.reshape(S, B, 2 * LRU_W), conv_w.astype(F32), conv_b.reshape(1, LRU_W).astype(F32), wab, bab,
                 lam.reshape(1, LRU_W).astype(F32))
    wo = w_out.astype(BF16)
    return nsa, lru_t.reshape(S, B * LRU_W), wo[:QW], wo[QW:], (B, S)


def kernel(x, p, positions, even_norm_mix, even_w_in, even_fox_bf, even_gdn_conv_w, even_gdn_a_log, even_gdn_dt_bias, even_gdn_norm_w, even_w_out, odd_norm_mix, odd_w_in, odd_cmp_k_pe, odd_cmp_k_w1, odd_cmp_k_w2, odd_cmp_v_pe, odd_cmp_v_w1, odd_cmp_v_w2, odd_rg_conv_w, odd_rg_conv_b, odd_rg_wa, odd_rg_ba, odd_rg_wx, odd_rg_bx, odd_rg_lambda, odd_w_out, mlp_norm, mlp_w_up, mlp_w_down, ple_norm, ple_w_gate, ple_w_proj, final_norm):
    B, S, D = x.shape
    T = B * S
    depth = p.shape[0]
    h = x.reshape(T, D)
    rope_tabs = _rope_tables(positions) if depth > 1 else None
    for i in range(depth):
        j = i // 2
        if i % 2 == 0:
            mix = _even_mixer(h, B, S, even_norm_mix[j], even_w_in[j], even_fox_bf[j], even_gdn_conv_w[j],
                              even_gdn_a_log[j], even_gdn_dt_bias[j], even_gdn_norm_w[j], even_w_out[j])
        else:
            mix = _odd_mixer(h, B, S, rope_tabs, odd_norm_mix[j], odd_w_in[j], odd_cmp_k_pe[j], odd_cmp_k_w1[j],
                             odd_cmp_k_w2[j], odd_cmp_v_pe[j], odd_cmp_v_w1[j], odd_cmp_v_w2[j], odd_rg_conv_w[j],
                             odd_rg_conv_b[j], odd_rg_wa[j], odd_rg_ba[j], odd_rg_wx[j], odd_rg_bx[j],
                             odd_rg_lambda[j], odd_w_out[j])
        h = _layer_tail(h, *mix, mlp_norm[i], mlp_w_up[i].astype(BF16), mlp_w_down[i].astype(BF16),
                        p[i].reshape(T, -1), ple_norm[i], ple_w_gate[i].astype(BF16), ple_w_proj[i].astype(BF16),
                        final_norm, final=(i == depth - 1))
    return h.reshape(B, S, D)
```

```python
import functools

import numpy as np
import jax
import jax.numpy as jnp
from jax import lax
from jax.experimental import pallas as pl
from jax.experimental.pallas import tpu as pltpu

F32 = jnp.float32
BF16 = jnp.bfloat16
HIGHEST = lax.Precision.HIGHEST

NORM_EPS = 1e-6
LOG2E = float(np.log2(np.e))
NEG = -1e30
REMOVED = -3e38
LANES = 128
SUBLANES = 8
VMEM_LIMIT = 56 * 1024 * 1024

HEAD_DIM = 64
ROT_DIM = 16
ROPE_THETA = 500000.0
FOX_HEADS = 8
FOX_F_PIECES = 3
GDN_HEADS = 4
GDN_DIM = 128
GDN_CHUNK = 64
GDN_B_LANE = FOX_HEADS
GDN_A_LANE = FOX_HEADS + GDN_HEADS
NSA_HEADS = 8
NSA_GROUPS = 2
NSA_HPG = NSA_HEADS // NSA_GROUPS
CMP_BLOCK = 32
CMP_STRIDE = 16
CMP_HIDDEN = 128
SEL_BLOCK = 64
SEL_TOPK = 16
WINDOW = 512
LRU_W = 512
RG_C = 8.0
CONV_W = 4


def _cparams(*sem):
    return pltpu.CompilerParams(dimension_semantics=sem, vmem_limit_bytes=VMEM_LIMIT)


def _dot(a, b, **kw):
    return jnp.dot(a, b, preferred_element_type=F32, **kw)


def _dot_nt(a, b, **kw):
    return lax.dot_general(a, b, (((1,), (1,)), ((), ())), preferred_element_type=F32, **kw)


def _dot_tn(a, b, **kw):
    return lax.dot_general(a, b, (((0,), (0,)), ((), ())), preferred_element_type=F32, **kw)


def _softplus(x):
    return jnp.maximum(x, 0.0) + jnp.log1p(jnp.exp(-jnp.abs(x)))


def _sigmoid(x):
    return 1.0 / (1.0 + jnp.exp(-x))


def _silu(x):
    return x * _sigmoid(x)


def _gelu_tanh(x):
    return 0.5 * x * (1.0 + jnp.tanh(np.float32(np.sqrt(2.0 / np.pi)) * (x + 0.044715 * (x * x * x))))


def _rms(x, w):
    ms = jnp.mean(x * x, axis=-1, keepdims=True)
    return x * lax.rsqrt(ms + NORM_EPS) * w


def _apply_rope(y, c, s1, s2):
    outs = []
    for g in range(y.shape[1] // LANES):
        yg = y[:, g * LANES:(g + 1) * LANES]
        outs.append(yg * c + pltpu.roll(yg, LANES - ROT_DIM // 2, 1) * s1 + pltpu.roll(yg, ROT_DIM // 2, 1) * s2)
    return outs[0] if len(outs) == 1 else jnp.concatenate(outs, axis=1)


def _pad_heads(y, consts):
    tm = y.shape[0]
    lane = lax.broadcasted_iota(jnp.int32, (tm, LANES), 1)
    cst = jnp.zeros((tm, LANES), F32)
    for ln, val in consts:
        cst = jnp.where(lane == HEAD_DIM + ln, val, cst)
    low = lane < HEAD_DIM
    outs = []
    for g in range(y.shape[1] // LANES):
        yg = y[:, g * LANES:(g + 1) * LANES]
        outs.append(jnp.where(low, yg, cst))
        outs.append(jnp.where(low, pltpu.roll(yg, HEAD_DIM, 1), cst))
    return jnp.concatenate(outs, axis=1)


def _norm_proj_kernel(*refs, segs, n_out, rope):
    x_ref, nw_ref, w_ref = refs[:3]
    n_in = 3
    if rope:
        c_ref, s1_ref, s2_ref = refs[n_in:n_in + 3]
        n_in += 3
    out_refs = refs[n_in:n_in + n_out]
    xn = _rms(x_ref[...], nw_ref[...]).astype(BF16)
    for (oi, oc, wc, width, do_rope, scale, pad) in segs:
        y = _dot(xn, w_ref[:, wc:wc + width])
        if do_rope:
            y = _apply_rope(y, c_ref[...], s1_ref[...], s2_ref[...])
        if scale != 1.0:
            y = y * scale
        if pad is not None:
            y = _pad_heads(y, pad)
        out_refs[oi][:, oc:oc + y.shape[1]] = y.astype(out_refs[oi].dtype)


def _norm_proj(x, nw, w, segs, out_defs, rope_tabs=None, tm=512):
    T, D = x.shape
    N = w.shape[1]
    tm = min(tm, T)
    rope = rope_tabs is not None
    in_specs = [pl.BlockSpec((tm, D), lambda i: (i, 0)),
                pl.BlockSpec((1, D), lambda i: (0, 0)),
                pl.BlockSpec((D, N), lambda i: (0, 0))]
    args = [x, nw.reshape(1, D), w]
    if rope:
        in_specs += [pl.BlockSpec((tm, LANES), lambda i: (i, 0))] * 3
        args += list(rope_tabs)
    out_shape, out_specs = [], []
    for od in out_defs:
        wd, dt = od[0], od[1]
        if len(od) == 3:
            Bb, Ss = od[2]
            nst = Ss // tm
            out_shape.append(jax.ShapeDtypeStruct((Ss, Bb * wd), dt))
            out_specs.append(pl.BlockSpec((tm, wd), lambda i, nst=nst: (i % nst, i // nst)))
        else:
            out_shape.append(jax.ShapeDtypeStruct((T, wd), dt))
            out_specs.append(pl.BlockSpec((tm, wd), lambda i: (i, 0)))
    return pl.pallas_call(
        functools.partial(_norm_proj_kernel, segs=tuple(segs), n_out=len(out_defs), rope=rope),
        grid=(T // tm,), in_specs=in_specs, out_specs=out_specs, out_shape=out_shape,
        compiler_params=_cparams("parallel"), name="norm_proj",
    )(*args)


def _rope_table_kernel(pos_ref, f_ref, e_ref, one_ref, c_ref, s1_ref, s2_ref):
    ang = pos_ref[0] * f_ref[...]
    cs = jnp.concatenate([jnp.cos(ang), jnp.sin(ang)], axis=0)
    hi = cs.astype(BF16)
    lo = (cs - hi.astype(F32)).astype(BF16)
    spread = lambda j: _dot_tn(hi, e_ref[j]) + _dot_tn(lo, e_ref[j])
    c_ref[...] = spread(0) + one_ref[...]
    s1_ref[...] = spread(1)
    s2_ref[...] = spread(2)


def _rope_tables(positions):
    B, S = positions.shape
    half = ROT_DIM // 2
    inv_freq = ROPE_THETA ** (-jnp.arange(half, dtype=F32) * (2.0 / ROT_DIM))
    pos = positions.astype(F32).reshape(B, 1, S)
    lane = np.arange(LANES)
    in_head, freq = lane % HEAD_DIM, lane % half
    e = np.zeros((3, 2 * half, LANES), np.float32)
    e[0, freq, lane] = in_head < ROT_DIM
    e[1, half + freq, lane] = -1.0 * (in_head < half)
    e[2, half + freq, lane] = 1.0 * ((in_head >= half) & (in_head < ROT_DIM))
    ones = (in_head >= ROT_DIM).astype(np.float32).reshape(1, LANES)
    sh = jax.ShapeDtypeStruct((B * S, LANES), F32)
    spec = pl.BlockSpec((S, LANES), lambda b: (b, 0))
    return pl.pallas_call(
        _rope_table_kernel, grid=(B,),
        in_specs=[pl.BlockSpec((1, 1, S), lambda b: (b, 0, 0)), pl.BlockSpec((half, 1), lambda b: (0, 0)),
                  pl.BlockSpec((3, 2 * half, LANES), lambda b: (0, 0, 0)), pl.BlockSpec((1, LANES), lambda b: (0, 0))],
        out_specs=[spec, spec, spec], out_shape=[sh, sh, sh],
        compiler_params=_cparams("parallel"), name="rope_tables",
    )(pos, inv_freq.reshape(half, 1), jnp.asarray(e, dtype=BF16), jnp.asarray(ones))


def _tail_kernel(h_ref, a1_ref, a2_ref, w1_ref, w2_ref, nwm_ref, wu_ref, wd_ref, p_ref, nwp_ref, wg_ref, wp_ref,
                 fw_ref, o_ref, *, final):
    hm = h_ref[...] + _dot(a1_ref[...], w1_ref[...]) + _dot(a2_ref[...], w2_ref[...])
    u = jnp.maximum(_dot(_rms(hm, nwm_ref[...]).astype(BF16), wu_ref[...]), 0.0)
    h2 = hm + _dot((u * u).astype(BF16), wd_ref[...])
    gate = _sigmoid(_dot(_rms(h2, nwp_ref[...]).astype(BF16), wg_ref[...]))
    y = h2 + gate * _dot(p_ref[...].astype(BF16), wp_ref[...])
    if final:
        y = _rms(y, fw_ref[...])
    o_ref[...] = y


def _layer_tail(h, a1, a2, w1, w2, a2_time_major, nwm, wu, wd, p, nwp, wg, wp, fw, final, tm=512):
    T, D = h.shape
    FF = wu.shape[1]
    P = p.shape[1]
    tm = min(tm, T)
    K1, K2 = w1.shape[0], w2.shape[0]
    if a2_time_major is None:
        a2_spec = pl.BlockSpec((tm, K2), lambda i: (i, 0))
    else:
        nst = a2_time_major[1] // tm
        a2_spec = pl.BlockSpec((tm, K2), lambda i: (i % nst, i // nst))
    row = lambda w: pl.BlockSpec((tm, w), lambda i: (i, 0))
    full = lambda r, c: pl.BlockSpec((r, c), lambda i: (0, 0))
    return pl.pallas_call(
        functools.partial(_tail_kernel, final=final), grid=(T // tm,),
        in_specs=[row(D), row(K1), a2_spec, full(K1, D), full(K2, D), full(1, D), full(D, FF), full(FF, D),
                  row(P), full(1, D), full(D, D), full(P, D), full(1, D)],
        out_specs=row(D),
        out_shape=jax.ShapeDtypeStruct((T, D), F32),
        compiler_params=_cparams("parallel"), name="layer_tail",
    )(h, a1, a2, w1, w2, nwm.reshape(1, D), wu, wd, p, nwp.reshape(1, D), wg, wp, fw.reshape(1, D))


def _fox_gate_kernel(x_ref, b_ref, o_ref, carry_ref, *, ts):
    @pl.when(pl.program_id(1) == 0)
    def _():
        carry_ref[...] = jnp.zeros_like(carry_ref)

    r = lax.broadcasted_iota(jnp.int32, (LANES, LANES), 0)
    c = lax.broadcasted_iota(jnp.int32, (LANES, LANES), 1)
    lower = (r >= c).astype(F32)
    lane = lax.broadcasted_iota(jnp.int32, (LANES, LANES), 1)

    nblk = ts // LANES
    within = []
    for n in range(nblk):
        x = x_ref[n * LANES:(n + 1) * LANES, :] + b_ref[...]
        logf = jnp.minimum(x, 0.0) - jnp.log1p(jnp.exp(-jnp.abs(x)))
        within.append(_dot(lower, logf, precision=HIGHEST))
    carry = carry_ref[0:1, :]
    for n in range(nblk):
        cum = within[n] + carry
        carry = cum[LANES - 1:LANES, :]
        f = cum * LOG2E
        hi = f.astype(BF16).astype(F32)
        mid = (f - hi).astype(BF16).astype(F32)
        lo = f - hi - mid
        for h in range(FOX_HEADS):
            v = jnp.where(lane == HEAD_DIM, hi[:, h:h + 1],
                          jnp.where(lane == HEAD_DIM + 1, mid[:, h:h + 1],
                                    jnp.where(lane == HEAD_DIM + 2, lo[:, h:h + 1], 0.0)))
            o_ref[n * LANES:(n + 1) * LANES, h * LANES:(h + 1) * LANES] = v.astype(o_ref.dtype)
    carry_ref[0:1, :] = carry


def _fox_gate(sm, bias, B, S, ts=1024):
    T = B * S
    ts = min(ts, S)
    ns = S // ts
    b = jnp.zeros((1, LANES), F32).at[0, :FOX_HEADS].set(bias.astype(F32))
    return pl.pallas_call(
        functools.partial(_fox_gate_kernel, ts=ts), grid=(B, ns),
        in_specs=[pl.BlockSpec((ts, LANES), lambda bb, n: (bb * ns + n, 0)),
                  pl.BlockSpec((1, LANES), lambda bb, n: (0, 0))],
        out_specs=pl.BlockSpec((ts, FOX_HEADS * LANES), lambda bb, n: (bb * ns + n, 0)),
        out_shape=jax.ShapeDtypeStruct((T, FOX_HEADS * LANES), BF16),
        scratch_shapes=[pltpu.VMEM((SUBLANES, LANES), F32)],
        compiler_params=_cparams("parallel", "arbitrary"), name="fox_gate",
    )(sm, b)


def _fox_attn_kernel(q_ref, k_ref, v_ref, f_ref, o_ref, *, tq, tk, dk, hp):
    i = pl.program_id(2)
    qs = [q_ref[:, hh * LANES:(hh + 1) * LANES] for hh in range(hp)]
    nsub = tq // tk

    def scores(start, size, r0):
        rows = pl.ds(start, size)
        return [_dot_nt(qs[hh][r0:], k_ref[rows, hh * LANES:(hh + 1) * LANES] + f_ref[rows, hh * LANES:(hh + 1) * LANES])
                for hh in range(hp)]

    def finish(ss, start, size, r0, carry, masked):
        rows = pl.ds(start, size)
        out = []
        for hh in range(hp):
            m, acc = carry[hh]
            s = ss[hh]
            if masked:
                r = lax.broadcasted_iota(jnp.int32, (tq - r0, size), 0)
                c = lax.broadcasted_iota(jnp.int32, (tq - r0, size), 1)
                s = jnp.where(c <= r, s, NEG)
            m_new = jnp.maximum(m[r0:], jnp.max(s, axis=1, keepdims=True))
            pr = jnp.exp2(s - m_new).astype(BF16)
            acc_new = jnp.exp2(m[r0:] - m_new) * acc[r0:] + _dot(pr, v_ref[rows, hh * LANES:(hh + 1) * LANES])
            if r0:
                m_new = jnp.concatenate([m[:r0], m_new], axis=0)
                acc_new = jnp.concatenate([acc[:r0], acc_new], axis=0)
            out.append((m_new, acc_new))
        return tuple(out)

    def tile(start, size, r0, carry, masked):
        return finish(scores(start, size, r0), start, size, r0, carry, masked)

    init = tuple((jnp.full((tq, 1), NEG, F32), jnp.zeros((tq, LANES), F32)) for _ in range(hp))
    carry = lax.fori_loop(0, i * nsub, lambda j, c: tile(pl.multiple_of(j * tk, tk), tk, 0, c, False), init)
    dstarts = [pl.multiple_of(i * tq + d * dk, dk) for d in range(tq // dk)]
    dss = [scores(dstarts[d], dk, d * dk) for d in range(tq // dk)]
    for d in range(tq // dk):
        carry = finish(dss[d], dstarts[d], dk, d * dk, carry, True)
    o_ref[...] = jnp.concatenate([acc[:, :HEAD_DIM] / acc[:, HEAD_DIM:HEAD_DIM + 1] for (_, acc) in carry],
                                 axis=1).astype(o_ref.dtype)


def _fox_attn(q, k, v, f3, B, S, tq=1024, tk=1024, dk=512, hp=2):
    T = B * S
    tq = min(tq, S)
    tk = min(tk, tq)
    dk = min(dk, tq)
    nq = S // tq
    bw = hp * LANES
    ng = FOX_HEADS // hp
    res = pl.BlockSpec((S, bw), lambda b, p, i: (b, p))
    return pl.pallas_call(
        functools.partial(_fox_attn_kernel, tq=tq, tk=tk, dk=dk, hp=hp), grid=(B, ng, nq),
        in_specs=[pl.BlockSpec((tq, bw), lambda b, p, i: (b * nq + i, p)), res, res, res],
        out_specs=pl.BlockSpec((tq, hp * HEAD_DIM), lambda b, p, i: (b * nq + i, p)),
        out_shape=jax.ShapeDtypeStruct((T, FOX_HEADS * HEAD_DIM), BF16),
        compiler_params=_cparams("parallel", "parallel", "arbitrary"), name="fox_attn",
    )(q, k, v, f3)


def _cumsum_rows(x):
    n = x.shape[0]
    row = lax.broadcasted_iota(jnp.int32, x.shape, 0)
    d = 1
    while d < n:
        x = x + jnp.where(row >= d, pltpu.roll(x, d, 0), 0.0)
        d *= 2
    return x


def _gdn_kernel(x_ref, z_ref, sm_ref, cw_ref, par_ref, nw_ref, o_ref, state_ref, xp_ref, *, cps):
    C = GDN_CHUNK
    D = GDN_DIM
    W = GDN_HEADS * D
    R = cps * C
    n = pl.program_id(1)

    @pl.when(n == 0)
    def _():
        state_ref[...] = jnp.zeros_like(state_ref)
        xp_ref[0:SUBLANES, :] = jnp.zeros((SUBLANES, 3 * W), F32)

    x = x_ref[...]
    xp_ref[SUBLANES:SUBLANES + R, :] = x
    cw = cw_ref[...]
    y = x * cw[CONV_W - 1:CONV_W]
    for j in range(CONV_W - 1):
        off = SUBLANES - (CONV_W - 1) + j
        y = y + xp_ref[off:off + R, :] * cw[j:j + 1]
    xp_ref[0:SUBLANES, :] = x[R - SUBLANES:R]
    y = _silu(y)

    sm = sm_ref[...]
    beta_all = _sigmoid(sm)
    g_all = -jnp.exp(par_ref[0:1, :]) * _softplus(sm + par_ref[1:2, :])
    ri = lax.broadcasted_iota(jnp.int32, (C, C), 0)
    ci = lax.broadcasted_iota(jnp.int32, (C, C), 1)
    incl = ri >= ci
    strict = ri > ci

    chains = [(c, h) for c in range(cps) for h in range(GDN_HEADS)]
    gcs = [_cumsum_rows(g_all[c * C:(c + 1) * C]) for c in range(cps)]
    gcts = [gc.T for gc in gcs]
    pre = {}
    for (c, h) in chains:
        r0 = c * C
        q = y[r0:r0 + C, h * D:(h + 1) * D]
        k = y[r0:r0 + C, W + h * D:W + (h + 1) * D]
        v = y[r0:r0 + C, 2 * W + h * D:2 * W + (h + 1) * D]
        qn = q * lax.rsqrt(jnp.sum(q * q, axis=-1, keepdims=True) + 1e-6) * (D ** -0.5)
        kn = k * lax.rsqrt(jnp.sum(k * k, axis=-1, keepdims=True) + 1e-6)
        beta = beta_all[r0:r0 + C, GDN_B_LANE + h:GDN_B_LANE + h + 1]
        gcol = gcs[c][:, GDN_A_LANE + h:GDN_A_LANE + h + 1]
        grow = gcts[c][GDN_A_LANE + h:GDN_A_LANE + h + 1, :]
        decay = jnp.where(incl, jnp.exp(jnp.where(incl, gcol - grow, 0.0)), 0.0)
        kb = kn * beta
        g_last = gcol[C - 1:C, :]
        pre[c, h] = dict(qn16=qn.astype(BF16), kn16=kn.astype(BF16), kb16=kb.astype(BF16), decay=decay,
                         rhs=jnp.concatenate([v * beta, kb * jnp.exp(gcol)], axis=1),
                         qg16=(qn * jnp.exp(gcol)).astype(BF16),
                         kg16=(kn * jnp.exp(g_last - gcol)).astype(BF16), e_last=jnp.exp(g_last))
    kk = {ch: _dot_nt(pre[ch]["kb16"], pre[ch]["kn16"]) for ch in chains}
    qk = {ch: _dot_nt(pre[ch]["qn16"], pre[ch]["kn16"]) for ch in chains}
    P = {ch: -jnp.where(strict, kk[ch] * pre[ch]["decay"], 0.0) for ch in chains}
    X = dict(P)
    for _ in range(5):
        P = {ch: _dot(P[ch].astype(BF16), P[ch].astype(BF16)) for ch in chains}
        XP = {ch: _dot(X[ch].astype(BF16), P[ch].astype(BF16)) for ch in chains}
        X = {ch: X[ch] + P[ch] + XP[ch] for ch in chains}
    sol = {ch: pre[ch]["rhs"] + _dot(X[ch].astype(BF16), pre[ch]["rhs"].astype(BF16)) for ch in chains}
    attn16 = {ch: jnp.where(incl, qk[ch] * pre[ch]["decay"], 0.0).astype(BF16) for ch in chains}

    heads = range(GDN_HEADS)
    states = [state_ref[h] for h in heads]
    for c in range(cps):
        r0 = c * C
        st16 = [states[h].astype(BF16) for h in heads]
        ws = [_dot(sol[c, h][:, D:].astype(BF16), st16[h]) for h in heads]
        qs = [_dot(pre[c, h]["qg16"], st16[h]) for h in heads]
        v_new = [(sol[c, h][:, :D] - ws[h]).astype(BF16) for h in heads]
        av = [_dot(attn16[c, h], v_new[h]) for h in heads]
        kv = [_dot_tn(pre[c, h]["kg16"], v_new[h]) for h in heads]
        outs = []
        for h in heads:
            states[h] = states[h] * pre[c, h]["e_last"] + kv[h]
            o = _rms(qs[h] + av[h], nw_ref[...]) * _silu(z_ref[r0:r0 + C, h * D:(h + 1) * D])
            outs.append(o)
        o_ref[r0:r0 + C, :] = jnp.concatenate(outs, axis=1).astype(o_ref.dtype)
    for h in heads:
        state_ref[h] = states[h]


def _gdn(gqkv, gz, small, conv_w, a_log, dt_bias, norm_w, B, S, cps=8):
    T = B * S
    C = GDN_CHUNK
    R = cps * C
    N = S // R
    W = GDN_HEADS * GDN_DIM
    par = jnp.zeros((SUBLANES, LANES), F32)
    lanes = slice(GDN_A_LANE, GDN_A_LANE + GDN_HEADS)
    par = par.at[0, lanes].set(a_log.astype(F32)).at[1, lanes].set(dt_bias.astype(F32))
    return pl.pallas_call(
        functools.partial(_gdn_kernel, cps=cps), grid=(B, N),
        in_specs=[pl.BlockSpec((R, 3 * W), lambda b, n: (b * N + n, 0)),
                  pl.BlockSpec((R, W), lambda b, n: (b * N + n, 0)),
                  pl.BlockSpec((R, LANES), lambda b, n: (b * N + n, 0)),
                  pl.BlockSpec((CONV_W, 3 * W), lambda b, n: (0, 0)),
                  pl.BlockSpec((SUBLANES, LANES), lambda b, n: (0, 0)),
                  pl.BlockSpec((1, GDN_DIM), lambda b, n: (0, 0))],
        out_specs=pl.BlockSpec((R, W), lambda b, n: (b * N + n, 0)),
        out_shape=jax.ShapeDtypeStruct((T, W), BF16),
        scratch_shapes=[pltpu.VMEM((GDN_HEADS, GDN_DIM, GDN_DIM), F32),
                        pltpu.VMEM((SUBLANES + R, 3 * W), F32)],
        compiler_params=_cparams("parallel", "arbitrary"), name="gdn",
    )(gqkv, gz, small, conv_w.astype(F32), par, norm_w.reshape(1, GDN_DIM).astype(F32))


def _compress_kernel(x_ref, pe_ref, w1_ref, wbd_ref, w2_ref, o_ref, *, nh):
    ab = jnp.zeros((nh, NSA_GROUPS * 2 * CMP_HIDDEN), F32)
    for l in range(CMP_STRIDE):
        ab = ab + _dot(x_ref[pl.ds(l, nh, stride=CMP_STRIDE), :].astype(BF16), wbd_ref[0, l])
    lane = lax.broadcasted_iota(jnp.int32, (nh, HEAD_DIM), 1)
    one_col = jnp.where(lane == 0, 1.0, 0.0)
    c = _dot(pe_ref[0].astype(BF16), w1_ref[0])
    for g in range(NSA_GROUPS):
        a = ab[:, 2 * g * CMP_HIDDEN:(2 * g + 1) * CMP_HIDDEN]
        b = ab[:, (2 * g + 1) * CMP_HIDDEN:(2 * g + 2) * CMP_HIDDEN]
        hid = _gelu_tanh(a + pltpu.roll(b, nh - 1, 0) + c[0:1])
        out = _dot(hid.astype(BF16), w2_ref[0])
        o_ref[0, g] = jnp.concatenate([out, one_col], axis=1).astype(o_ref.dtype)


def _compress(kc, vc, pe, w1, w2, B, S):
    nh = S // CMP_STRIDE
    G = NSA_GROUPS
    half = CMP_STRIDE * HEAD_DIM
    w1r = w1.reshape(2, 2, CMP_STRIDE, HEAD_DIM, CMP_HIDDEN)
    per_tok = jnp.concatenate([w1r[:, 0], w1r[:, 1]], axis=-1)
    eye = jnp.eye(G, dtype=w1.dtype)
    wbd = per_tok[:, :, None, :, None, :] * eye[None, None, :, None, :, None]
    wbd = wbd.reshape(2, CMP_STRIDE, G * HEAD_DIM, G * 2 * CMP_HIDDEN)
    x_spec = pl.BlockSpec((S, G * HEAD_DIM), lambda b, j: (b, 0))
    out = [pl.pallas_call(
        functools.partial(_compress_kernel, nh=nh), grid=(B, 1),
        in_specs=[x_spec, pl.BlockSpec((1, SUBLANES, 2 * half), lambda b, j, kv=kv: (kv, 0, 0)),
                  pl.BlockSpec((1, 2 * half, CMP_HIDDEN), lambda b, j, kv=kv: (kv, 0, 0)),
                  pl.BlockSpec((1, CMP_STRIDE, G * HEAD_DIM, G * 2 * CMP_HIDDEN), lambda b, j, kv=kv: (kv, 0, 0, 0)),
                  pl.BlockSpec((1, CMP_HIDDEN, HEAD_DIM), lambda b, j, kv=kv: (kv, 0, 0))],
        out_specs=pl.BlockSpec((1, G, nh, LANES), lambda b, j: (b, 0, 0, 0)),
        out_shape=jax.ShapeDtypeStruct((B, G, nh, LANES), BF16),
        compiler_params=_cparams("parallel", "arbitrary"), name="nsa_compress",
    )(x, pe, w1, wbd, w2) for kv, x in enumerate((kc, vc))]
    return out


def _stack_heads(qb):
    return jnp.concatenate([qb[:, h * HEAD_DIM:(h + 1) * HEAD_DIM] for h in range(NSA_HPG)], axis=0)


def _unstack_heads(o, tq):
    return jnp.concatenate([o[h * tq:(h + 1) * tq] for h in range(NSA_HPG)], axis=1)


def _nsa_cmp_kernel(q_ref, kc_ref, vc_ref, ov_ref, oc_ref, bias_ref, *, tq, n_sel, top_k):
    i = pl.program_id(1)
    G = NSA_GROUPS
    GW = NSA_HPG * HEAD_DIM
    ncp = kc_ref.shape[2]
    row = lax.broadcasted_iota(jnp.int32, (NSA_HPG * tq, 1), 0)
    t4 = i * tq + (row & (tq - 1))
    cmp_end = lax.broadcasted_iota(jnp.int32, (1, ncp), 1) * CMP_STRIDE + (CMP_BLOCK - 1)
    valid = cmp_end <= t4
    ss = [_dot_nt(_stack_heads(q_ref[:, g * GW:(g + 1) * GW]), kc_ref[0, g, :, :HEAD_DIM]) for g in range(G)]
    ps, accs = [], []
    for g in range(G):
        s = jnp.where(valid, ss[g], NEG)
        m = jnp.max(s, axis=1, keepdims=True)
        p = jnp.exp2(s - m)
        ps.append(p)
        accs.append(_dot(p.astype(BF16), vc_ref[0, g]))
    imps = []
    for g in range(G):
        inv = jnp.where(t4 >= CMP_BLOCK - 1, 1.0 / accs[g][:, HEAD_DIM:HEAD_DIM + 1], 0.0)
        oc_ref[:, g * GW:(g + 1) * GW] = _unstack_heads(accs[g][:, :HEAD_DIM] * inv, tq)
        p = ps[g] * inv
        psum = p[0:tq] + p[tq:2 * tq] + p[2 * tq:3 * tq] + p[3 * tq:4 * tq]
        imps.append(_dot(psum, ov_ref[...], precision=HIGHEST))
    blk = lax.broadcasted_iota(jnp.int32, (1, LANES), 1)
    t = i * tq + lax.broadcasted_iota(jnp.int32, (tq, 1), 0)
    cur = t >> (SEL_BLOCK.bit_length() - 1)
    forced = (blk == 0) | (blk == cur) | (blk == cur - 1)
    future = blk * SEL_BLOCK > t
    blk_t = lax.broadcasted_iota(jnp.int32, (LANES, tq), 0).astype(F32)
    for g in range(G):
        imp = jnp.where(future, NEG, jnp.where(forced, REMOVED, imps[g]))
        imp = jnp.where(blk < n_sel, imp, REMOVED)
        imp_t = imp.T
        for _ in range(max(top_k - 3, 0)):
            mx = jnp.max(imp_t, axis=0, keepdims=True)
            first = jnp.min(jnp.where(imp_t == mx, blk_t, float(LANES)), axis=0, keepdims=True)
            imp_t = jnp.where(blk_t == first, REMOVED, imp_t)
        sel = jnp.where(imp_t == REMOVED, 1.0, 0.0).T
        bias_ref[0, g] = jnp.where((sel > 0.0) & jnp.logical_not(future), 0.0, NEG).astype(bias_ref.dtype)


def _nsa_cmp(q, kc, vc, overlap, B, S, tq=512):
    T = B * S
    tq = min(tq, S)
    nq = S // tq
    ncp = kc.shape[2]
    n_sel = S // SEL_BLOCK
    QW = NSA_HEADS * HEAD_DIM
    G = NSA_GROUPS
    return pl.pallas_call(
        functools.partial(_nsa_cmp_kernel, tq=tq, n_sel=n_sel, top_k=min(SEL_TOPK, n_sel)),
        grid=(B, nq),
        in_specs=[pl.BlockSpec((tq, QW), lambda b, i: (b * nq + i, 0)),
                  pl.BlockSpec((1, G, ncp, LANES), lambda b, i: (b, 0, 0, 0)),
                  pl.BlockSpec((1, G, ncp, LANES), lambda b, i: (b, 0, 0, 0)),
                  pl.BlockSpec((ncp, LANES), lambda b, i: (0, 0))],
        out_specs=[pl.BlockSpec((tq, QW), lambda b, i: (b * nq + i, 0)),
                   pl.BlockSpec((1, G, tq, LANES), lambda b, i: (b, 0, i, 0))],
        out_shape=[jax.ShapeDtypeStruct((T, QW), F32),
                   jax.ShapeDtypeStruct((B, G, S, LANES), BF16)],
        compiler_params=_cparams("parallel", "parallel"), name="nsa_cmp",
    )(q, kc, vc, overlap)


def _nsa_main_kernel(q_ref, bias_ref, ks_ref, oh_ref, vs_ref, kw_ref, vw_ref, oc_ref, g_ref, o_ref, *, qn, tk, wspan):
    R = NSA_HPG * qn
    tb = 2 * tk
    i = pl.program_id(2)
    q4 = _stack_heads(q_ref[...])
    q4p = jnp.concatenate([q4, jnp.zeros_like(q4)], axis=1)
    b4 = jnp.concatenate([bias_ref[0, 0]] * NSA_HPG, axis=0)
    qa = jnp.concatenate([q4p, b4], axis=1)
    t4 = i * qn + (lax.broadcasted_iota(jnp.int32, (R, 1), 0) & (qn - 1))

    def scores(start, size):
        rows = pl.ds(start, size)
        ka = jnp.concatenate([ks_ref[rows, :], oh_ref[rows, :]], axis=1)
        return _dot_nt(qa, ka)

    def finish(s, start, size, carry, masked):
        m, acc = carry
        if masked:
            kpos = start + lax.broadcasted_iota(jnp.int32, (1, size), 1)
            s = jnp.where(kpos <= t4, s, NEG)
        m_new = jnp.maximum(m, jnp.max(s, axis=1, keepdims=True))
        pr = jnp.exp2(s - m_new).astype(BF16)
        acc = jnp.exp2(m - m_new) * acc + _dot(pr, vs_ref[pl.ds(start, size), :])
        return m_new, acc

    def step(start, size, carry, masked):
        return finish(scores(start, size), start, size, carry, masked)

    first = i * qn
    nbig = first // tb
    diag = (first // tk) * tk
    nsmall = (diag - nbig * tb) // tk
    init = (jnp.full((R, 1), NEG, F32), jnp.zeros((R, LANES), F32))
    carry = lax.fori_loop(0, nbig, lambda j, c: step(pl.multiple_of(j * tb, tb), tb, c, False), init)
    carry = lax.fori_loop(0, nsmall, lambda j, c: step(pl.multiple_of(nbig * tb, tk), tk, c, False), carry)
    dstart = pl.multiple_of(diag, tk)
    s_d = scores(dstart, tk)
    wrows = pl.ds(pl.multiple_of(jnp.maximum(first + qn - wspan, 0), qn), wspan)
    s_w = _dot_nt(q4p, kw_ref[wrows, :])
    _, acc_s = finish(s_d, dstart, tk, carry, True)
    kpos = jnp.maximum(first + qn - wspan, 0) + lax.broadcasted_iota(jnp.int32, (1, wspan), 1)
    wmask = lax.bitcast_convert_type(t4 - kpos, jnp.uint32) < jnp.uint32(WINDOW)
    s_w = jnp.where(wmask, s_w, NEG)
    m_w = jnp.max(s_w, axis=1, keepdims=True)
    acc_w = _dot(jnp.exp2(s_w - m_w).astype(BF16), vw_ref[wrows, :])

    gates = _sigmoid(g_ref[...])
    first_group = pl.program_id(1) == 0
    ng = NSA_HPG * 3

    def gate(h, branch):
        j = 3 * h + branch
        return jnp.where(first_group, gates[:, j:j + 1], gates[:, ng + j:ng + j + 1])

    oc = oc_ref[...]
    outs = []
    for h in range(NSA_HPG):
        hs = slice(h * qn, (h + 1) * qn)
        c_s = gate(h, 1) / acc_s[hs, HEAD_DIM:HEAD_DIM + 1]
        c_w = gate(h, 2) / acc_w[hs, HEAD_DIM:HEAD_DIM + 1]
        outs.append(gate(h, 0) * oc[:, h * HEAD_DIM:(h + 1) * HEAD_DIM]
                    + c_s * acc_s[hs, :HEAD_DIM] + c_w * acc_w[hs, :HEAD_DIM])
    o_ref[...] = jnp.concatenate(outs, axis=1).astype(o_ref.dtype)


def _nsa_main(q, bias, ksp, onehot, vsa, kwp, vwa, oc, gates, B, S, qn=256, tk=512):
    T = B * S
    nq = S // qn
    tk = min(tk, S // 2)
    wspan = min(WINDOW + qn, S)
    GW = NSA_HPG * HEAD_DIM
    res = pl.BlockSpec((S, LANES), lambda b, g, i: (b, g))
    return pl.pallas_call(
        functools.partial(_nsa_main_kernel, qn=qn, tk=tk, wspan=wspan), grid=(B, NSA_GROUPS, nq),
        in_specs=[pl.BlockSpec((qn, GW), lambda b, g, i: (b * nq + i, g)),
                  pl.BlockSpec((1, 1, qn, LANES), lambda b, g, i: (b, g, i, 0)),
                  res, pl.BlockSpec((S, LANES), lambda b, g, i: (0, 0)), res, res, res,
                  pl.BlockSpec((qn, GW), lambda b, g, i: (b * nq + i, g)),
                  pl.BlockSpec((qn, LANES), lambda b, g, i: (b * nq + i, 0))],
        out_specs=pl.BlockSpec((qn, GW), lambda b, g, i: (b * nq + i, g)),
        out_shape=jax.ShapeDtypeStruct((T, NSA_HEADS * HEAD_DIM), BF16),
        compiler_params=_cparams("parallel", "parallel", "arbitrary"), name="nsa_main",
    )(q, bias, ksp, onehot, vsa, kwp, vwa, oc, gates)


def _lru_kernel(gx_ref, cw_ref, cb_ref, wab_ref, bab_ref, lam_ref, o_ref, h_ref, tail_ref, a_ref, b_ref, *, ts):
    n = pl.program_id(0)
    Bb = gx_ref.shape[1]
    W = gx_ref.shape[2] // 2

    @pl.when(n == 0)
    def _():
        h_ref[...] = jnp.zeros_like(h_ref)
        tail_ref[...] = jnp.zeros_like(tail_ref)

    xin = gx_ref[:, :, W:]
    xp = jnp.concatenate([tail_ref[...], xin], axis=0)
    cw = cw_ref[...]
    x = cb_ref[...].reshape(1, 1, W)
    for j in range(CONV_W):
        x = x + xp[j:j + ts] * cw[j:j + 1].reshape(1, 1, W)
    tail_ref[...] = xin[ts - (CONV_W - 1):ts]
    x2 = x.reshape(ts * Bb, W)
    pre = _dot(x2.astype(BF16), wab_ref[...]) + bab_ref[...]
    r = _sigmoid(pre[:, :W])
    ig = _sigmoid(pre[:, W:])
    log_a = (-RG_C * _softplus(-lam_ref[...])) * r
    a = jnp.exp(log_a)
    th = jnp.tanh(log_a)
    bb = jnp.sqrt(-2.0 * th / (1.0 - th)) * (ig * x2)
    a_ref[...] = a.reshape(ts, Bb, W)
    b_ref[...] = bb.reshape(ts, Bb, W)

    def scan(t, h):
        h = a_ref[t] * h + b_ref[t]
        b_ref[t] = h
        return h

    h_ref[...] = lax.fori_loop(0, ts, scan, h_ref[...], unroll=8)
    o_ref[...] = (b_ref[...] * _gelu_tanh(gx_ref[:, :, :W])).astype(o_ref.dtype)


def _lru(gx_t, conv_w, conv_b, wab, bab, lam, ts=128):
    S, Bb, W2 = gx_t.shape
    W = W2 // 2
    ts = min(ts, S)
    full = lambda shape: pl.BlockSpec(shape, lambda n: (0,) * len(shape))
    return pl.pallas_call(
        functools.partial(_lru_kernel, ts=ts), grid=(S // ts,),
        in_specs=[pl.BlockSpec((ts, Bb, W2), lambda n: (n, 0, 0)),
                  full((CONV_W, W)), full((1, W)), full((W, 2 * W)), full((1, 2 * W)), full((1, W))],
        out_specs=pl.BlockSpec((ts, Bb, W), lambda n: (n, 0, 0)),
        out_shape=jax.ShapeDtypeStruct((S, Bb, W), BF16),
        scratch_shapes=[pltpu.VMEM((Bb, W), F32), pltpu.VMEM((CONV_W - 1, Bb, W), F32),
                        pltpu.VMEM((ts, Bb, W), F32), pltpu.VMEM((ts, Bb, W), F32)],
        compiler_params=_cparams("arbitrary"), name="rg_lru",
    )(gx_t, conv_w, conv_b, wab, bab, lam)


def _even_mixer(h, B, S, norm_w, w_in, fox_bf, conv_w, a_log, dt_bias, gdn_norm_w, w_out):
    T, D = h.shape
    FW = FOX_HEADS * HEAD_DIM
    GW = GDN_HEADS * GDN_DIM
    o_ff = 3 * FW
    o_g = o_ff + FOX_HEADS
    o_gb = o_g + 4 * GW
    small = jnp.concatenate([w_in[:, o_ff:o_g], w_in[:, o_gb:o_gb + 2 * GDN_HEADS],
                             jnp.zeros((D, LANES - FOX_HEADS - 2 * GDN_HEADS), w_in.dtype)], axis=1)
    w = jnp.concatenate([w_in[:, :o_ff], w_in[:, o_g:o_gb], small], axis=1).astype(BF16)
    segs = [(0, 0, 0, FW, False, HEAD_DIM ** -0.5 * LOG2E, tuple((j, -1.0) for j in range(FOX_F_PIECES))),
            (1, 0, FW, FW, False, 1.0, ()),
            (2, 0, 2 * FW, FW, False, 1.0, ((0, 1.0),)),
            (3, 0, 3 * FW, 3 * GW, False, 1.0, None),
            (4, 0, 3 * FW + 3 * GW, GW, False, 1.0, None),
            (5, 0, 3 * FW + 4 * GW, LANES, False, 1.0, None)]
    fq, fk, fv, gqkv, gz, sm = _norm_proj(h, norm_w, w, segs,
                                          [(2 * FW, BF16), (2 * FW, BF16), (2 * FW, BF16),
                                           (3 * GW, F32), (GW, F32), (LANES, F32)])
    f3 = _fox_gate(sm, fox_bf, B, S)
    fox = _fox_attn(fq, fk, fv, f3, B, S)
    gdn = _gdn(gqkv, gz, sm, conv_w, a_log, dt_bias, gdn_norm_w, B, S)
    wo = w_out.astype(BF16)
    return fox, gdn, wo[:FW], wo[FW:], None


def _odd_mixer(h, B, S, rope_tabs, norm_w, w_in, k_pe, k_w1, k_w2, v_pe, v_w1, v_w2,
               conv_w, conv_b, wa, ba, wx, bx, lam, w_out):
    T, D = h.shape
    QW = NSA_HEADS * HEAD_DIM
    KW = NSA_GROUPS * HEAD_DIM
    G = NSA_GROUPS
    o_ng = QW + 6 * KW
    ngw = NSA_HEADS * 3
    w = jnp.concatenate([w_in[:, :o_ng], w_in[:, o_ng:o_ng + ngw], jnp.zeros((D, LANES - ngw), w_in.dtype),
                         w_in[:, o_ng + ngw:]], axis=1).astype(BF16)
    ones = ((0, 1.0),)
    segs = [(0, 0, 0, QW, True, HEAD_DIM ** -0.5 * LOG2E, None),
            (1, 0, QW, KW, True, 1.0, None),
            (2, 0, QW + KW, KW, False, 1.0, None),
            (3, 0, QW + 2 * KW, KW, True, 1.0, ()),
            (4, 0, QW + 3 * KW, KW, False, 1.0, ones),
            (5, 0, QW + 4 * KW, KW, True, 1.0, ()),
            (6, 0, QW + 5 * KW, KW, False, 1.0, ones),
            (7, 0, o_ng, LANES, False, 1.0, None),
            (8, 0, o_ng + LANES, 2 * LRU_W, False, 1.0, None)]
    q, kc, vc, ksp, vsa, kwp, vwa, ng, rgx_t = _norm_proj(
        h, norm_w, w, segs,
        [(QW, BF16), (KW, F32), (KW, F32), (2 * KW, BF16), (2 * KW, BF16), (2 * KW, BF16), (2 * KW, BF16),
         (LANES, F32), (2 * LRU_W, F32, (B, S))],
        rope_tabs=rope_tabs)
    nh = S // CMP_STRIDE
    pe = jnp.stack([k_pe.reshape(-1), v_pe.reshape(-1)]).astype(F32)
    pe = jnp.broadcast_to(pe[:, None, :], (2, SUBLANES, pe.shape[-1]))
    w1 = jnp.stack([k_w1, v_w1]).astype(BF16)
    w2 = jnp.stack([k_w2, v_w2]).astype(BF16)
    kcmp, vcmp = _compress(kc, vc, pe, w1, w2, B, S)
    n_cmp = (S - CMP_BLOCK) // CMP_STRIDE + 1
    ncp = -(-nh // LANES) * LANES
    if ncp != nh:
        kcmp, vcmp = (jnp.pad(t, ((0, 0), (0, 0), (0, ncp - nh), (0, 0))) for t in (kcmp, vcmp))
    n_sel = S // SEL_BLOCK
    cs = np.arange(ncp) * CMP_STRIDE
    ss = np.arange(LANES) * SEL_BLOCK
    ov = ((cs[:, None] <= ss[None, :] + SEL_BLOCK - 1) & (cs[:, None] + CMP_BLOCK - 1 >= ss[None, :])
          & (np.arange(ncp)[:, None] < n_cmp) & (np.arange(LANES)[None, :] < n_sel))
    overlap = jnp.asarray(ov.astype(np.float32))
    oc, bias = _nsa_cmp(q, kcmp, vcmp, overlap, B, S)
    blk_of_pos = np.arange(S) // SEL_BLOCK
    onehot = jnp.asarray((blk_of_pos[:, None] == np.arange(LANES)[None, :]).astype(np.float32), dtype=BF16)
    nsa = _nsa_main(q, bias, ksp, onehot, vsa, kwp, vwa, oc, ng, B, S)
    nblk, bw, _ = wa.shape
    eye = jnp.eye(nblk, dtype=wa.dtype)
    dense = lambda wb: (eye[:, None, :, None] * wb[:, :, None, :]).reshape(nblk * bw, nblk * bw)
    wab = jnp.concatenate([dense(wa), dense(wx)], axis=1).astype(BF16)
    bab = jnp.concatenate([ba, bx]).reshape(1, 2 * LRU_W).astype(F32)
    lru_t = _lru(rgx_t.reshape(S, B, 2 * LRU_W), conv_w.astype(F32), conv_b.reshape(1, LRU_W).astype(F32), wab, bab,
                 lam.reshape(1, LRU_W).astype(F32))
    wo = w_out.astype(BF16)
    return nsa, lru_t.reshape(S, B * LRU_W), wo[:QW], wo[QW:], (B, S)


def kernel(x, p, positions, even_norm_mix, even_w_in, even_fox_bf, even_gdn_conv_w, even_gdn_a_log, even_gdn_dt_bias, even_gdn_norm_w, even_w_out, odd_norm_mix, odd_w_in, odd_cmp_k_pe, odd_cmp_k_w1, odd_cmp_k_w2, odd_cmp_v_pe, odd_cmp_v_w1, odd_cmp_v_w2, odd_rg_conv_w, odd_rg_conv_b, odd_rg_wa, odd_rg_ba, odd_rg_wx, odd_rg_bx, odd_rg_lambda, odd_w_out, mlp_norm, mlp_w_up, mlp_w_down, ple_norm, ple_w_gate, ple_w_proj, final_norm):
    B, S, D = x.shape
    T = B * S
    depth = p.shape[0]
    h = x.reshape(T, D)
    rope_tabs = _rope_tables(positions) if depth > 1 else None
    for i in range(depth):
        j = i // 2
        if i % 2 == 0:
            mix = _even_mixer(h, B, S, even_norm_mix[j], even_w_in[j], even_fox_bf[j], even_gdn_conv_w[j],
                              even_gdn_a_log[j], even_gdn_dt_bias[j], even_gdn_norm_w[j], even_w_out[j])
        else:
            mix = _odd_mixer(h, B, S, rope_tabs, odd_norm_mix[j], odd_w_in[j], odd_cmp_k_pe[j], odd_cmp_k_w1[j],
                             odd_cmp_k_w2[j], odd_cmp_v_pe[j], odd_cmp_v_w1[j], odd_cmp_v_w2[j], odd_rg_conv_w[j],
                             odd_rg_conv_b[j], odd_rg_wa[j], odd_rg_ba[j], odd_rg_wx[j], odd_rg_bx[j],
                             odd_rg_lambda[j], odd_w_out[j])
        h = _layer_tail(h, *mix, mlp_norm[i], mlp_w_up[i].astype(BF16), mlp_w_down[i].astype(BF16),
                        p[i].reshape(T, -1), ple_norm[i], ple_w_gate[i].astype(BF16), ple_w_proj[i].astype(BF16),
                        final_norm, final=(i == depth - 1))
    return h.reshape(B, S, D)
```

```python
import functools

import numpy as np
import jax
import jax.numpy as jnp
from jax import lax
from jax.experimental import pallas as pl
from jax.experimental.pallas import tpu as pltpu

F32 = jnp.float32
BF16 = jnp.bfloat16
HIGHEST = lax.Precision.HIGHEST

NORM_EPS = 1e-6
LOG2E = float(np.log2(np.e))
NEG = -1e30
REMOVED = -3e38
LANES = 128
SUBLANES = 8
VMEM_LIMIT = 56 * 1024 * 1024

HEAD_DIM = 64
ROT_DIM = 16
ROPE_THETA = 500000.0
FOX_HEADS = 8
FOX_F_PIECES = 3
GDN_HEADS = 4
GDN_DIM = 128
GDN_CHUNK = 64
GDN_B_LANE = FOX_HEADS
GDN_A_LANE = FOX_HEADS + GDN_HEADS
NSA_HEADS = 8
NSA_GROUPS = 2
NSA_HPG = NSA_HEADS // NSA_GROUPS
CMP_BLOCK = 32
CMP_STRIDE = 16
CMP_HIDDEN = 128
SEL_BLOCK = 64
SEL_TOPK = 16
WINDOW = 512
LRU_W = 512
RG_C = 8.0
CONV_W = 4


def _cparams(*sem):
    return pltpu.CompilerParams(dimension_semantics=sem, vmem_limit_bytes=VMEM_LIMIT)


def _dot(a, b, **kw):
    return jnp.dot(a, b, preferred_element_type=F32, **kw)


def _dot_nt(a, b, **kw):
    return lax.dot_general(a, b, (((1,), (1,)), ((), ())), preferred_element_type=F32, **kw)


def _dot_tn(a, b, **kw):
    return lax.dot_general(a, b, (((0,), (0,)), ((), ())), preferred_element_type=F32, **kw)


def _softplus(x):
    return jnp.maximum(x, 0.0) + jnp.log1p(jnp.exp(-jnp.abs(x)))


def _sigmoid(x):
    return 1.0 / (1.0 + jnp.exp(-x))


def _silu(x):
    return x * _sigmoid(x)


def _gelu_tanh(x):
    return 0.5 * x * (1.0 + jnp.tanh(np.float32(np.sqrt(2.0 / np.pi)) * (x + 0.044715 * (x * x * x))))


def _rms(x, w):
    ms = jnp.mean(x * x, axis=-1, keepdims=True)
    return x * lax.rsqrt(ms + NORM_EPS) * w


def _apply_rope(y, c, s1, s2):
    outs = []
    for g in range(y.shape[1] // LANES):
        yg = y[:, g * LANES:(g + 1) * LANES]
        outs.append(yg * c + pltpu.roll(yg, LANES - ROT_DIM // 2, 1) * s1 + pltpu.roll(yg, ROT_DIM // 2, 1) * s2)
    return outs[0] if len(outs) == 1 else jnp.concatenate(outs, axis=1)


def _pad_heads(y, consts):
    tm = y.shape[0]
    lane = lax.broadcasted_iota(jnp.int32, (tm, LANES), 1)
    cst = jnp.zeros((tm, LANES), F32)
    for ln, val in consts:
        cst = jnp.where(lane == HEAD_DIM + ln, val, cst)
    low = lane < HEAD_DIM
    outs = []
    for g in range(y.shape[1] // LANES):
        yg = y[:, g * LANES:(g + 1) * LANES]
        outs.append(jnp.where(low, yg, cst))
        outs.append(jnp.where(low, pltpu.roll(yg, HEAD_DIM, 1), cst))
    return jnp.concatenate(outs, axis=1)


def _norm_proj_kernel(*refs, segs, n_out, rope):
    x_ref, nw_ref, w_ref = refs[:3]
    n_in = 3
    if rope:
        c_ref, s1_ref, s2_ref = refs[n_in:n_in + 3]
        n_in += 3
    out_refs = refs[n_in:n_in + n_out]
    xn = _rms(x_ref[...], nw_ref[...]).astype(BF16)
    for (oi, oc, wc, width, do_rope, scale, pad) in segs:
        y = _dot(xn, w_ref[:, wc:wc + width])
        if do_rope:
            y = _apply_rope(y, c_ref[...], s1_ref[...], s2_ref[...])
        if scale != 1.0:
            y = y * scale
        if pad is not None:
            y = _pad_heads(y, pad)
        out_refs[oi][:, oc:oc + y.shape[1]] = y.astype(out_refs[oi].dtype)


def _norm_proj(x, nw, w, segs, out_defs, rope_tabs=None, tm=512):
    T, D = x.shape
    N = w.shape[1]
    tm = min(tm, T)
    rope = rope_tabs is not None
    in_specs = [pl.BlockSpec((tm, D), lambda i: (i, 0)),
                pl.BlockSpec((1, D), lambda i: (0, 0)),
                pl.BlockSpec((D, N), lambda i: (0, 0))]
    args = [x, nw.reshape(1, D), w]
    if rope:
        in_specs += [pl.BlockSpec((tm, LANES), lambda i: (i, 0))] * 3
        args += list(rope_tabs)
    out_shape = [jax.ShapeDtypeStruct((T, wd), dt) for wd, dt in out_defs]
    out_specs = [pl.BlockSpec((tm, wd), lambda i: (i, 0)) for wd, _ in out_defs]
    return pl.pallas_call(
        functools.partial(_norm_proj_kernel, segs=tuple(segs), n_out=len(out_defs), rope=rope),
        grid=(T // tm,), in_specs=in_specs, out_specs=out_specs, out_shape=out_shape,
        compiler_params=_cparams("parallel"), name="norm_proj",
    )(*args)


def _rope_table_kernel(pos_ref, f_ref, e_ref, one_ref, c_ref, s1_ref, s2_ref):
    ang = pos_ref[0] * f_ref[...]
    cs = jnp.concatenate([jnp.cos(ang), jnp.sin(ang)], axis=0)
    hi = cs.astype(BF16)
    lo = (cs - hi.astype(F32)).astype(BF16)
    spread = lambda j: _dot_tn(hi, e_ref[j]) + _dot_tn(lo, e_ref[j])
    c_ref[...] = spread(0) + one_ref[...]
    s1_ref[...] = spread(1)
    s2_ref[...] = spread(2)


def _rope_tables(positions):
    B, S = positions.shape
    half = ROT_DIM // 2
    inv_freq = ROPE_THETA ** (-jnp.arange(half, dtype=F32) * (2.0 / ROT_DIM))
    pos = positions.astype(F32).reshape(B, 1, S)
    lane = np.arange(LANES)
    in_head, freq = lane % HEAD_DIM, lane % half
    e = np.zeros((3, 2 * half, LANES), np.float32)
    e[0, freq, lane] = in_head < ROT_DIM
    e[1, half + freq, lane] = -1.0 * (in_head < half)
    e[2, half + freq, lane] = 1.0 * ((in_head >= half) & (in_head < ROT_DIM))
    ones = (in_head >= ROT_DIM).astype(np.float32).reshape(1, LANES)
    sh = jax.ShapeDtypeStruct((B * S, LANES), F32)
    spec = pl.BlockSpec((S, LANES), lambda b: (b, 0))
    return pl.pallas_call(
        _rope_table_kernel, grid=(B,),
        in_specs=[pl.BlockSpec((1, 1, S), lambda b: (b, 0, 0)), pl.BlockSpec((half, 1), lambda b: (0, 0)),
                  pl.BlockSpec((3, 2 * half, LANES), lambda b: (0, 0, 0)), pl.BlockSpec((1, LANES), lambda b: (0, 0))],
        out_specs=[spec, spec, spec], out_shape=[sh, sh, sh],
        compiler_params=_cparams("parallel"), name="rope_tables",
    )(pos, inv_freq.reshape(half, 1), jnp.asarray(e, dtype=BF16), jnp.asarray(ones))


def _tail_kernel(h_ref, a1_ref, a2_ref, w1_ref, w2_ref, nwm_ref, wu_ref, wd_ref, p_ref, nwp_ref, wg_ref, wp_ref,
                 fw_ref, o_ref, *, final):
    hm = h_ref[...] + _dot(a1_ref[...], w1_ref[...]) + _dot(a2_ref[...], w2_ref[...])
    u = jnp.maximum(_dot(_rms(hm, nwm_ref[...]).astype(BF16), wu_ref[...]), 0.0)
    h2 = hm + _dot((u * u).astype(BF16), wd_ref[...])
    gate = _sigmoid(_dot(_rms(h2, nwp_ref[...]).astype(BF16), wg_ref[...]))
    y = h2 + gate * _dot(p_ref[...].astype(BF16), wp_ref[...])
    if final:
        y = _rms(y, fw_ref[...])
    o_ref[...] = y


def _layer_tail(h, a1, a2, w1, w2, nwm, wu, wd, p, nwp, wg, wp, fw, final, tm=512):
    T, D = h.shape
    FF = wu.shape[1]
    P = p.shape[1]
    tm = min(tm, T)
    K1, K2 = w1.shape[0], w2.shape[0]
    row = lambda w: pl.BlockSpec((tm, w), lambda i: (i, 0))
    full = lambda r, c: pl.BlockSpec((r, c), lambda i: (0, 0))
    return pl.pallas_call(
        functools.partial(_tail_kernel, final=final), grid=(T // tm,),
        in_specs=[row(D), row(K1), row(K2), full(K1, D), full(K2, D), full(1, D), full(D, FF), full(FF, D),
                  row(P), full(1, D), full(D, D), full(P, D), full(1, D)],
        out_specs=row(D),
        out_shape=jax.ShapeDtypeStruct((T, D), F32),
        compiler_params=_cparams("parallel"), name="layer_tail",
    )(h, a1, a2, w1, w2, nwm.reshape(1, D), wu, wd, p, nwp.reshape(1, D), wg, wp, fw.reshape(1, D))


def _fox_gate_kernel(x_ref, b_ref, o_ref, carry_ref, *, ts):
    @pl.when(pl.program_id(1) == 0)
    def _():
        carry_ref[...] = jnp.zeros_like(carry_ref)

    r = lax.broadcasted_iota(jnp.int32, (LANES, LANES), 0)
    c = lax.broadcasted_iota(jnp.int32, (LANES, LANES), 1)
    lower = (r >= c).astype(F32)
    lane = lax.broadcasted_iota(jnp.int32, (LANES, LANES), 1)

    nblk = ts // LANES
    within = []
    for n in range(nblk):
        x = x_ref[n * LANES:(n + 1) * LANES, :] + b_ref[...]
        logf = jnp.minimum(x, 0.0) - jnp.log1p(jnp.exp(-jnp.abs(x)))
        within.append(_dot(lower, logf, precision=HIGHEST))
    carry = carry_ref[0:1, :]
    for n in range(nblk):
        cum = within[n] + carry
        carry = cum[LANES - 1:LANES, :]
        f = cum * LOG2E
        hi = f.astype(BF16).astype(F32)
        mid = (f - hi).astype(BF16).astype(F32)
        lo = f - hi - mid
        for h in range(FOX_HEADS):
            v = jnp.where(lane == HEAD_DIM, hi[:, h:h + 1],
                          jnp.where(lane == HEAD_DIM + 1, mid[:, h:h + 1],
                                    jnp.where(lane == HEAD_DIM + 2, lo[:, h:h + 1], 0.0)))
            o_ref[n * LANES:(n + 1) * LANES, h * LANES:(h + 1) * LANES] = v.astype(o_ref.dtype)
    carry_ref[0:1, :] = carry


def _fox_gate(sm, bias, B, S, ts=1024):
    T = B * S
    ts = min(ts, S)
    ns = S // ts
    b = jnp.zeros((1, LANES), F32).at[0, :FOX_HEADS].set(bias.astype(F32))
    return pl.pallas_call(
        functools.partial(_fox_gate_kernel, ts=ts), grid=(B, ns),
        in_specs=[pl.BlockSpec((ts, LANES), lambda bb, n: (bb * ns + n, 0)),
                  pl.BlockSpec((1, LANES), lambda bb, n: (0, 0))],
        out_specs=pl.BlockSpec((ts, FOX_HEADS * LANES), lambda bb, n: (bb * ns + n, 0)),
        out_shape=jax.ShapeDtypeStruct((T, FOX_HEADS * LANES), BF16),
        scratch_shapes=[pltpu.VMEM((SUBLANES, LANES), F32)],
        compiler_params=_cparams("parallel", "arbitrary"), name="fox_gate",
    )(sm, b)


def _fox_attn_kernel(q_ref, k_ref, v_ref, f_ref, o_ref, *, tq, tk, dk, hp):
    i = pl.program_id(2)
    qs = [q_ref[:, hh * LANES:(hh + 1) * LANES] for hh in range(hp)]
    nsub = tq // tk

    def scores(start, size, r0):
        rows = pl.ds(start, size)
        return [_dot_nt(qs[hh][r0:], k_ref[rows, hh * LANES:(hh + 1) * LANES] + f_ref[rows, hh * LANES:(hh + 1) * LANES])
                for hh in range(hp)]

    def finish(ss, start, size, r0, carry, masked):
        rows = pl.ds(start, size)
        out = []
        for hh in range(hp):
            m, acc = carry[hh]
            s = ss[hh]
            if masked:
                r = lax.broadcasted_iota(jnp.int32, (tq - r0, size), 0)
                c = lax.broadcasted_iota(jnp.int32, (tq - r0, size), 1)
                s = jnp.where(c <= r, s, NEG)
            m_new = jnp.maximum(m[r0:], jnp.max(s, axis=1, keepdims=True))
            pr = jnp.exp2(s - m_new).astype(BF16)
            acc_new = jnp.exp2(m[r0:] - m_new) * acc[r0:] + _dot(pr, v_ref[rows, hh * LANES:(hh + 1) * LANES])
            if r0:
                m_new = jnp.concatenate([m[:r0], m_new], axis=0)
                acc_new = jnp.concatenate([acc[:r0], acc_new], axis=0)
            out.append((m_new, acc_new))
        return tuple(out)

    def tile(start, size, r0, carry, masked):
        return finish(scores(start, size, r0), start, size, r0, carry, masked)

    init = tuple((jnp.full((tq, 1), NEG, F32), jnp.zeros((tq, LANES), F32)) for _ in range(hp))
    carry = lax.fori_loop(0, i * nsub, lambda j, c: tile(pl.multiple_of(j * tk, tk), tk, 0, c, False), init)
    dstarts = [pl.multiple_of(i * tq + d * dk, dk) for d in range(tq // dk)]
    dss = [scores(dstarts[d], dk, d * dk) for d in range(tq // dk)]
    for d in range(tq // dk):
        carry = finish(dss[d], dstarts[d], dk, d * dk, carry, True)
    o_ref[...] = jnp.concatenate([acc[:, :HEAD_DIM] / acc[:, HEAD_DIM:HEAD_DIM + 1] for (_, acc) in carry],
                                 axis=1).astype(o_ref.dtype)


def _fox_attn(q, k, v, f3, B, S, tq=1024, tk=1024, dk=512, hp=2):
    T = B * S
    tq = min(tq, S)
    tk = min(tk, tq)
    dk = min(dk, tq)
    nq = S // tq
    bw = hp * LANES
    ng = FOX_HEADS // hp
    res = pl.BlockSpec((S, bw), lambda b, p, i: (b, p))
    return pl.pallas_call(
        functools.partial(_fox_attn_kernel, tq=tq, tk=tk, dk=dk, hp=hp), grid=(B, ng, nq),
        in_specs=[pl.BlockSpec((tq, bw), lambda b, p, i: (b * nq + i, p)), res, res, res],
        out_specs=pl.BlockSpec((tq, hp * HEAD_DIM), lambda b, p, i: (b * nq + i, p)),
        out_shape=jax.ShapeDtypeStruct((T, FOX_HEADS * HEAD_DIM), BF16),
        compiler_params=_cparams("parallel", "parallel", "arbitrary"), name="fox_attn",
    )(q, k, v, f3)


def _cumsum_rows(x):
    n = x.shape[0]
    row = lax.broadcasted_iota(jnp.int32, x.shape, 0)
    d = 1
    while d < n:
        x = x + jnp.where(row >= d, pltpu.roll(x, d, 0), 0.0)
        d *= 2
    return x


def _gdn_kernel(x_ref, z_ref, sm_ref, cw_ref, par_ref, nw_ref, o_ref, state_ref, xp_ref, *, cps):
    C = GDN_CHUNK
    D = GDN_DIM
    W = GDN_HEADS * D
    R = cps * C
    n = pl.program_id(1)

    @pl.when(n == 0)
    def _():
        state_ref[...] = jnp.zeros_like(state_ref)
        xp_ref[0:SUBLANES, :] = jnp.zeros((SUBLANES, 3 * W), F32)

    x = x_ref[...]
    xp_ref[SUBLANES:SUBLANES + R, :] = x
    cw = cw_ref[...]
    y = x * cw[CONV_W - 1:CONV_W]
    for j in range(CONV_W - 1):
        off = SUBLANES - (CONV_W - 1) + j
        y = y + xp_ref[off:off + R, :] * cw[j:j + 1]
    xp_ref[0:SUBLANES, :] = x[R - SUBLANES:R]
    y = _silu(y)

    sm = sm_ref[...]
    beta_all = _sigmoid(sm)
    g_all = -jnp.exp(par_ref[0:1, :]) * _softplus(sm + par_ref[1:2, :])
    ri = lax.broadcasted_iota(jnp.int32, (C, C), 0)
    ci = lax.broadcasted_iota(jnp.int32, (C, C), 1)
    incl = ri >= ci
    strict = ri > ci

    chains = [(c, h) for c in range(cps) for h in range(GDN_HEADS)]
    gcs = [_cumsum_rows(g_all[c * C:(c + 1) * C]) for c in range(cps)]
    gcts = [gc.T for gc in gcs]
    pre = {}
    for (c, h) in chains:
        r0 = c * C
        q = y[r0:r0 + C, h * D:(h + 1) * D]
        k = y[r0:r0 + C, W + h * D:W + (h + 1) * D]
        v = y[r0:r0 + C, 2 * W + h * D:2 * W + (h + 1) * D]
        qn = q * lax.rsqrt(jnp.sum(q * q, axis=-1, keepdims=True) + 1e-6) * (D ** -0.5)
        kn = k * lax.rsqrt(jnp.sum(k * k, axis=-1, keepdims=True) + 1e-6)
        beta = beta_all[r0:r0 + C, GDN_B_LANE + h:GDN_B_LANE + h + 1]
        gcol = gcs[c][:, GDN_A_LANE + h:GDN_A_LANE + h + 1]
        grow = gcts[c][GDN_A_LANE + h:GDN_A_LANE + h + 1, :]
        decay = jnp.where(incl, jnp.exp(jnp.where(incl, gcol - grow, 0.0)), 0.0)
        kb = kn * beta
        g_last = gcol[C - 1:C, :]
        pre[c, h] = dict(qn16=qn.astype(BF16), kn16=kn.astype(BF16), kb16=kb.astype(BF16), decay=decay,
                         rhs=jnp.concatenate([v * beta, kb * jnp.exp(gcol)], axis=1),
                         qg16=(qn * jnp.exp(gcol)).astype(BF16),
                         kg16=(kn * jnp.exp(g_last - gcol)).astype(BF16), e_last=jnp.exp(g_last))
    kk = {ch: _dot_nt(pre[ch]["kb16"], pre[ch]["kn16"]) for ch in chains}
    qk = {ch: _dot_nt(pre[ch]["qn16"], pre[ch]["kn16"]) for ch in chains}
    P = {ch: -jnp.where(strict, kk[ch] * pre[ch]["decay"], 0.0) for ch in chains}
    X = dict(P)
    for _ in range(5):
        P = {ch: _dot(P[ch].astype(BF16), P[ch].astype(BF16)) for ch in chains}
        XP = {ch: _dot(X[ch].astype(BF16), P[ch].astype(BF16)) for ch in chains}
        X = {ch: X[ch] + P[ch] + XP[ch] for ch in chains}
    sol = {ch: pre[ch]["rhs"] + _dot(X[ch].astype(BF16), pre[ch]["rhs"].astype(BF16)) for ch in chains}
    attn16 = {ch: jnp.where(incl, qk[ch] * pre[ch]["decay"], 0.0).astype(BF16) for ch in chains}

    heads = range(GDN_HEADS)
    states = [state_ref[h] for h in heads]
    for c in range(cps):
        r0 = c * C
        st16 = [states[h].astype(BF16) for h in heads]
        ws = [_dot(sol[c, h][:, D:].astype(BF16), st16[h]) for h in heads]
        qs = [_dot(pre[c, h]["qg16"], st16[h]) for h in heads]
        v_new = [(sol[c, h][:, :D] - ws[h]).astype(BF16) for h in heads]
        av = [_dot(attn16[c, h], v_new[h]) for h in heads]
        kv = [_dot_tn(pre[c, h]["kg16"], v_new[h]) for h in heads]
        outs = []
        for h in heads:
            states[h] = states[h] * pre[c, h]["e_last"] + kv[h]
            o = _rms(qs[h] + av[h], nw_ref[...]) * _silu(z_ref[r0:r0 + C, h * D:(h + 1) * D])
            outs.append(o)
        o_ref[r0:r0 + C, :] = jnp.concatenate(outs, axis=1).astype(o_ref.dtype)
    for h in heads:
        state_ref[h] = states[h]


def _gdn(gqkv, gz, small, conv_w, a_log, dt_bias, norm_w, B, S, cps=8):
    T = B * S
    C = GDN_CHUNK
    R = cps * C
    N = S // R
    W = GDN_HEADS * GDN_DIM
    par = jnp.zeros((SUBLANES, LANES), F32)
    lanes = slice(GDN_A_LANE, GDN_A_LANE + GDN_HEADS)
    par = par.at[0, lanes].set(a_log.astype(F32)).at[1, lanes].set(dt_bias.astype(F32))
    return pl.pallas_call(
        functools.partial(_gdn_kernel, cps=cps), grid=(B, N),
        in_specs=[pl.BlockSpec((R, 3 * W), lambda b, n: (b * N + n, 0)),
                  pl.BlockSpec((R, W), lambda b, n: (b * N + n, 0)),
                  pl.BlockSpec((R, LANES), lambda b, n: (b * N + n, 0)),
                  pl.BlockSpec((CONV_W, 3 * W), lambda b, n: (0, 0)),
                  pl.BlockSpec((SUBLANES, LANES), lambda b, n: (0, 0)),
                  pl.BlockSpec((1, GDN_DIM), lambda b, n: (0, 0))],
        out_specs=pl.BlockSpec((R, W), lambda b, n: (b * N + n, 0)),
        out_shape=jax.ShapeDtypeStruct((T, W), BF16),
        scratch_shapes=[pltpu.VMEM((GDN_HEADS, GDN_DIM, GDN_DIM), F32),
                        pltpu.VMEM((SUBLANES + R, 3 * W), F32)],
        compiler_params=_cparams("parallel", "arbitrary"), name="gdn",
    )(gqkv, gz, small, conv_w.astype(F32), par, norm_w.reshape(1, GDN_DIM).astype(F32))


def _compress_kernel(x_ref, pe_ref, w1_ref, wbd_ref, w2_ref, o_ref, *, nh):
    ab = jnp.zeros((nh, NSA_GROUPS * 2 * CMP_HIDDEN), F32)
    for l in range(CMP_STRIDE):
        ab = ab + _dot(x_ref[pl.ds(l, nh, stride=CMP_STRIDE), :].astype(BF16), wbd_ref[0, l])
    lane = lax.broadcasted_iota(jnp.int32, (nh, HEAD_DIM), 1)
    one_col = jnp.where(lane == 0, 1.0, 0.0)
    c = _dot(pe_ref[0].astype(BF16), w1_ref[0])
    for g in range(NSA_GROUPS):
        a = ab[:, 2 * g * CMP_HIDDEN:(2 * g + 1) * CMP_HIDDEN]
        b = ab[:, (2 * g + 1) * CMP_HIDDEN:(2 * g + 2) * CMP_HIDDEN]
        hid = _gelu_tanh(a + pltpu.roll(b, nh - 1, 0) + c[0:1])
        out = _dot(hid.astype(BF16), w2_ref[0])
        o_ref[0, g] = jnp.concatenate([out, one_col], axis=1).astype(o_ref.dtype)


def _compress(kc, vc, pe, w1, w2, B, S):
    nh = S // CMP_STRIDE
    G = NSA_GROUPS
    half = CMP_STRIDE * HEAD_DIM
    w1r = w1.reshape(2, 2, CMP_STRIDE, HEAD_DIM, CMP_HIDDEN)
    per_tok = jnp.concatenate([w1r[:, 0], w1r[:, 1]], axis=-1)
    eye = jnp.eye(G, dtype=w1.dtype)
    wbd = per_tok[:, :, None, :, None, :] * eye[None, None, :, None, :, None]
    wbd = wbd.reshape(2, CMP_STRIDE, G * HEAD_DIM, G * 2 * CMP_HIDDEN)
    x_spec = pl.BlockSpec((S, G * HEAD_DIM), lambda b, j: (b, 0))
    out = [pl.pallas_call(
        functools.partial(_compress_kernel, nh=nh), grid=(B, 1),
        in_specs=[x_spec, pl.BlockSpec((1, SUBLANES, 2 * half), lambda b, j, kv=kv: (kv, 0, 0)),
                  pl.BlockSpec((1, 2 * half, CMP_HIDDEN), lambda b, j, kv=kv: (kv, 0, 0)),
                  pl.BlockSpec((1, CMP_STRIDE, G * HEAD_DIM, G * 2 * CMP_HIDDEN), lambda b, j, kv=kv: (kv, 0, 0, 0)),
                  pl.BlockSpec((1, CMP_HIDDEN, HEAD_DIM), lambda b, j, kv=kv: (kv, 0, 0))],
        out_specs=pl.BlockSpec((1, G, nh, LANES), lambda b, j: (b, 0, 0, 0)),
        out_shape=jax.ShapeDtypeStruct((B, G, nh, LANES), BF16),
        compiler_params=_cparams("parallel", "arbitrary"), name="nsa_compress",
    )(x, pe, w1, wbd, w2) for kv, x in enumerate((kc, vc))]
    return out


def _stack_heads(qb):
    return jnp.concatenate([qb[:, h * HEAD_DIM:(h + 1) * HEAD_DIM] for h in range(NSA_HPG)], axis=0)


def _unstack_heads(o, tq):
    return jnp.concatenate([o[h * tq:(h + 1) * tq] for h in range(NSA_HPG)], axis=1)


def _nsa_cmp_kernel(q_ref, kc_ref, vc_ref, ov_ref, oc_ref, bias_ref, *, tq, n_sel, top_k):
    i = pl.program_id(1)
    G = NSA_GROUPS
    GW = NSA_HPG * HEAD_DIM
    ncp = kc_ref.shape[2]
    row = lax.broadcasted_iota(jnp.int32, (NSA_HPG * tq, 1), 0)
    t4 = i * tq + (row & (tq - 1))
    cmp_end = lax.broadcasted_iota(jnp.int32, (1, ncp), 1) * CMP_STRIDE + (CMP_BLOCK - 1)
    valid = cmp_end <= t4
    ss = [_dot_nt(_stack_heads(q_ref[:, g * GW:(g + 1) * GW]), kc_ref[0, g, :, :HEAD_DIM]) for g in range(G)]
    ps, accs = [], []
    for g in range(G):
        s = jnp.where(valid, ss[g], NEG)
        m = jnp.max(s, axis=1, keepdims=True)
        p = jnp.exp2(s - m)
        ps.append(p)
        accs.append(_dot(p.astype(BF16), vc_ref[0, g]))
    imps = []
    for g in range(G):
        inv = jnp.where(t4 >= CMP_BLOCK - 1, 1.0 / accs[g][:, HEAD_DIM:HEAD_DIM + 1], 0.0)
        oc_ref[:, g * GW:(g + 1) * GW] = _unstack_heads(accs[g][:, :HEAD_DIM] * inv, tq)
        p = ps[g] * inv
        psum = p[0:tq] + p[tq:2 * tq] + p[2 * tq:3 * tq] + p[3 * tq:4 * tq]
        imps.append(_dot(psum, ov_ref[...], precision=HIGHEST))
    blk = lax.broadcasted_iota(jnp.int32, (1, LANES), 1)
    t = i * tq + lax.broadcasted_iota(jnp.int32, (tq, 1), 0)
    cur = t >> (SEL_BLOCK.bit_length() - 1)
    forced = (blk == 0) | (blk == cur) | (blk == cur - 1)
    future = blk * SEL_BLOCK > t
    blk_t = lax.broadcasted_iota(jnp.int32, (LANES, tq), 0).astype(F32)
    for g in range(G):
        imp = jnp.where(future, NEG, jnp.where(forced, REMOVED, imps[g]))
        imp = jnp.where(blk < n_sel, imp, REMOVED)
        imp_t = imp.T
        for _ in range(max(top_k - 3, 0)):
            mx = jnp.max(imp_t, axis=0, keepdims=True)
            first = jnp.min(jnp.where(imp_t == mx, blk_t, float(LANES)), axis=0, keepdims=True)
            imp_t = jnp.where(blk_t == first, REMOVED, imp_t)
        sel = jnp.where(imp_t == REMOVED, 1.0, 0.0).T
        bias_ref[0, g] = jnp.where((sel > 0.0) & jnp.logical_not(future), 0.0, NEG).astype(bias_ref.dtype)


def _nsa_cmp(q, kc, vc, overlap, B, S, tq=512):
    T = B * S
    tq = min(tq, S)
    nq = S // tq
    ncp = kc.shape[2]
    n_sel = S // SEL_BLOCK
    QW = NSA_HEADS * HEAD_DIM
    G = NSA_GROUPS
    return pl.pallas_call(
        functools.partial(_nsa_cmp_kernel, tq=tq, n_sel=n_sel, top_k=min(SEL_TOPK, n_sel)),
        grid=(B, nq),
        in_specs=[pl.BlockSpec((tq, QW), lambda b, i: (b * nq + i, 0)),
                  pl.BlockSpec((1, G, ncp, LANES), lambda b, i: (b, 0, 0, 0)),
                  pl.BlockSpec((1, G, ncp, LANES), lambda b, i: (b, 0, 0, 0)),
                  pl.BlockSpec((ncp, LANES), lambda b, i: (0, 0))],
        out_specs=[pl.BlockSpec((tq, QW), lambda b, i: (b * nq + i, 0)),
                   pl.BlockSpec((1, G, tq, LANES), lambda b, i: (b, 0, i, 0))],
        out_shape=[jax.ShapeDtypeStruct((T, QW), F32),
                   jax.ShapeDtypeStruct((B, G, S, LANES), BF16)],
        compiler_params=_cparams("parallel", "parallel"), name="nsa_cmp",
    )(q, kc, vc, overlap)


def _nsa_main_kernel(q_ref, bias_ref, ks_ref, oh_ref, vs_ref, kw_ref, vw_ref, oc_ref, g_ref, o_ref, *, qn, tk, wspan):
    R = NSA_HPG * qn
    tb = 2 * tk
    i = pl.program_id(2)
    q4 = _stack_heads(q_ref[...])
    q4p = jnp.concatenate([q4, jnp.zeros_like(q4)], axis=1)
    b4 = jnp.concatenate([bias_ref[0, 0]] * NSA_HPG, axis=0)
    qa = jnp.concatenate([q4p, b4], axis=1)
    t4 = i * qn + (lax.broadcasted_iota(jnp.int32, (R, 1), 0) & (qn - 1))

    def scores(start, size):
        rows = pl.ds(start, size)
        ka = jnp.concatenate([ks_ref[rows, :], oh_ref[rows, :]], axis=1)
        return _dot_nt(qa, ka)

    def finish(s, start, size, carry, masked):
        m, acc = carry
        if masked:
            kpos = start + lax.broadcasted_iota(jnp.int32, (1, size), 1)
            s = jnp.where(kpos <= t4, s, NEG)
        m_new = jnp.maximum(m, jnp.max(s, axis=1, keepdims=True))
        pr = jnp.exp2(s - m_new).astype(BF16)
        acc = jnp.exp2(m - m_new) * acc + _dot(pr, vs_ref[pl.ds(start, size), :])
        return m_new, acc

    def step(start, size, carry, masked):
        return finish(scores(start, size), start, size, carry, masked)

    first = i * qn
    nbig = first // tb
    diag = (first // tk) * tk
    nsmall = (diag - nbig * tb) // tk
    init = (jnp.full((R, 1), NEG, F32), jnp.zeros((R, LANES), F32))
    carry = lax.fori_loop(0, nbig, lambda j, c: step(pl.multiple_of(j * tb, tb), tb, c, False), init)
    carry = lax.fori_loop(0, nsmall, lambda j, c: step(pl.multiple_of(nbig * tb, tk), tk, c, False), carry)
    dstart = pl.multiple_of(diag, tk)
    s_d = scores(dstart, tk)
    wrows = pl.ds(pl.multiple_of(jnp.maximum(first + qn - wspan, 0), qn), wspan)
    s_w = _dot_nt(q4p, kw_ref[wrows, :])
    _, acc_s = finish(s_d, dstart, tk, carry, True)
    kpos = jnp.maximum(first + qn - wspan, 0) + lax.broadcasted_iota(jnp.int32, (1, wspan), 1)
    wmask = lax.bitcast_convert_type(t4 - kpos, jnp.uint32) < jnp.uint32(WINDOW)
    s_w = jnp.where(wmask, s_w, NEG)
    m_w = jnp.max(s_w, axis=1, keepdims=True)
    acc_w = _dot(jnp.exp2(s_w - m_w).astype(BF16), vw_ref[wrows, :])

    gates = _sigmoid(g_ref[...])
    first_group = pl.program_id(1) == 0
    ng = NSA_HPG * 3

    def gate(h, branch):
        j = 3 * h + branch
        return jnp.where(first_group, gates[:, j:j + 1], gates[:, ng + j:ng + j + 1])

    oc = oc_ref[...]
    outs = []
    for h in range(NSA_HPG):
        hs = slice(h * qn, (h + 1) * qn)
        c_s = gate(h, 1) / acc_s[hs, HEAD_DIM:HEAD_DIM + 1]
        c_w = gate(h, 2) / acc_w[hs, HEAD_DIM:HEAD_DIM + 1]
        outs.append(gate(h, 0) * oc[:, h * HEAD_DIM:(h + 1) * HEAD_DIM]
                    + c_s * acc_s[hs, :HEAD_DIM] + c_w * acc_w[hs, :HEAD_DIM])
    o_ref[...] = jnp.concatenate(outs, axis=1).astype(o_ref.dtype)


def _nsa_main(q, bias, ksp, onehot, vsa, kwp, vwa, oc, gates, B, S, qn=256, tk=512):
    T = B * S
    nq = S // qn
    tk = min(tk, S // 2)
    wspan = min(WINDOW + qn, S)
    GW = NSA_HPG * HEAD_DIM
    res = pl.BlockSpec((S, LANES), lambda b, g, i: (b, g))
    return pl.pallas_call(
        functools.partial(_nsa_main_kernel, qn=qn, tk=tk, wspan=wspan), grid=(B, NSA_GROUPS, nq),
        in_specs=[pl.BlockSpec((qn, GW), lambda b, g, i: (b * nq + i, g)),
                  pl.BlockSpec((1, 1, qn, LANES), lambda b, g, i: (b, g, i, 0)),
                  res, pl.BlockSpec((S, LANES), lambda b, g, i: (0, 0)), res, res, res,
                  pl.BlockSpec((qn, GW), lambda b, g, i: (b * nq + i, g)),
                  pl.BlockSpec((qn, LANES), lambda b, g, i: (b * nq + i, 0))],
        out_specs=pl.BlockSpec((qn, GW), lambda b, g, i: (b * nq + i, g)),
        out_shape=jax.ShapeDtypeStruct((T, NSA_HEADS * HEAD_DIM), BF16),
        compiler_params=_cparams("parallel", "parallel", "arbitrary"), name="nsa_main",
    )(q, bias, ksp, onehot, vsa, kwp, vwa, oc, gates)


def _lru_kernel(gx_ref, cw_ref, cb_ref, wab_ref, bab_ref, lam_ref, o_ref, h_ref, tail_ref, a_ref, b_ref, *, ts):
    n = pl.program_id(0)
    Bb = gx_ref.shape[0]
    W = gx_ref.shape[2] // 2

    @pl.when(n == 0)
    def _():
        h_ref[...] = jnp.zeros_like(h_ref)
        tail_ref[...] = jnp.zeros_like(tail_ref)

    gx = jnp.swapaxes(gx_ref[...], 0, 1)
    xin = gx[:, :, W:]
    xp = jnp.concatenate([tail_ref[...], xin], axis=0)
    cw = cw_ref[...]
    x = cb_ref[...].reshape(1, 1, W)
    for j in range(CONV_W):
        x = x + xp[j:j + ts] * cw[j:j + 1].reshape(1, 1, W)
    tail_ref[...] = xin[ts - (CONV_W - 1):ts]
    x2 = x.reshape(ts * Bb, W)
    pre = _dot(x2.astype(BF16), wab_ref[...]) + bab_ref[...]
    r = _sigmoid(pre[:, :W])
    ig = _sigmoid(pre[:, W:])
    log_a = (-RG_C * _softplus(-lam_ref[...])) * r
    a = jnp.exp(log_a)
    th = jnp.tanh(log_a)
    bb = jnp.sqrt(-2.0 * th / (1.0 - th)) * (ig * x2)
    a_ref[...] = a.reshape(ts, Bb, W)
    b_ref[...] = bb.reshape(ts, Bb, W)

    def scan(t, h):
        h = a_ref[t] * h + b_ref[t]
        b_ref[t] = h
        return h

    h_ref[...] = lax.fori_loop(0, ts, scan, h_ref[...], unroll=8)
    y = b_ref[...] * _gelu_tanh(gx[:, :, :W])
    o_ref[...] = jnp.swapaxes(y, 0, 1).astype(o_ref.dtype)


def _lru(gx, conv_w, conv_b, wab, bab, lam, ts=128):
    Bb, S, W2 = gx.shape
    W = W2 // 2
    ts = min(ts, S)
    full = lambda shape: pl.BlockSpec(shape, lambda n: (0,) * len(shape))
    return pl.pallas_call(
        functools.partial(_lru_kernel, ts=ts), grid=(S // ts,),
        in_specs=[pl.BlockSpec((Bb, ts, W2), lambda n: (0, n, 0)),
                  full((CONV_W, W)), full((1, W)), full((W, 2 * W)), full((1, 2 * W)), full((1, W))],
        out_specs=pl.BlockSpec((Bb, ts, W), lambda n: (0, n, 0)),
        out_shape=jax.ShapeDtypeStruct((Bb, S, W), BF16),
        scratch_shapes=[pltpu.VMEM((Bb, W), F32), pltpu.VMEM((CONV_W - 1, Bb, W), F32),
                        pltpu.VMEM((ts, Bb, W), F32), pltpu.VMEM((ts, Bb, W), F32)],
        compiler_params=_cparams("arbitrary"), name="rg_lru",
    )(gx, conv_w, conv_b, wab, bab, lam)


def _even_mixer(h, B, S, norm_w, w_in, fox_bf, conv_w, a_log, dt_bias, gdn_norm_w, w_out):
    T, D = h.shape
    FW = FOX_HEADS * HEAD_DIM
    GW = GDN_HEADS * GDN_DIM
    o_ff = 3 * FW
    o_g = o_ff + FOX_HEADS
    o_gb = o_g + 4 * GW
    small = jnp.concatenate([w_in[:, o_ff:o_g], w_in[:, o_gb:o_gb + 2 * GDN_HEADS],
                             jnp.zeros((D, LANES - FOX_HEADS - 2 * GDN_HEADS), w_in.dtype)], axis=1)
    w = jnp.concatenate([w_in[:, :o_ff], w_in[:, o_g:o_gb], small], axis=1).astype(BF16)
    segs = [(0, 0, 0, FW, False, HEAD_DIM ** -0.5 * LOG2E, tuple((j, -1.0) for j in range(FOX_F_PIECES))),
            (1, 0, FW, FW, False, 1.0, ()),
            (2, 0, 2 * FW, FW, False, 1.0, ((0, 1.0),)),
            (3, 0, 3 * FW, 3 * GW, False, 1.0, None),
            (4, 0, 3 * FW + 3 * GW, GW, False, 1.0, None),
            (5, 0, 3 * FW + 4 * GW, LANES, False, 1.0, None)]
    fq, fk, fv, gqkv, gz, sm = _norm_proj(h, norm_w, w, segs,
                                          [(2 * FW, BF16), (2 * FW, BF16), (2 * FW, BF16),
                                           (3 * GW, F32), (GW, F32), (LANES, F32)])
    f3 = _fox_gate(sm, fox_bf, B, S)
    fox = _fox_attn(fq, fk, fv, f3, B, S)
    gdn = _gdn(gqkv, gz, sm, conv_w, a_log, dt_bias, gdn_norm_w, B, S)
    wo = w_out.astype(BF16)
    return fox, gdn, wo[:FW], wo[FW:]


def _odd_mixer(h, B, S, rope_tabs, norm_w, w_in, k_pe, k_w1, k_w2, v_pe, v_w1, v_w2,
               conv_w, conv_b, wa, ba, wx, bx, lam, w_out):
    T, D = h.shape
    QW = NSA_HEADS * HEAD_DIM
    KW = NSA_GROUPS * HEAD_DIM
    G = NSA_GROUPS
    o_ng = QW + 6 * KW
    ngw = NSA_HEADS * 3
    w = jnp.concatenate([w_in[:, :o_ng], w_in[:, o_ng:o_ng + ngw], jnp.zeros((D, LANES - ngw), w_in.dtype),
                         w_in[:, o_ng + ngw:]], axis=1).astype(BF16)
    ones = ((0, 1.0),)
    segs = [(0, 0, 0, QW, True, HEAD_DIM ** -0.5 * LOG2E, None),
            (1, 0, QW, KW, True, 1.0, None),
            (2, 0, QW + KW, KW, False, 1.0, None),
            (3, 0, QW + 2 * KW, KW, True, 1.0, ()),
            (4, 0, QW + 3 * KW, KW, False, 1.0, ones),
            (5, 0, QW + 4 * KW, KW, True, 1.0, ()),
            (6, 0, QW + 5 * KW, KW, False, 1.0, ones),
            (7, 0, o_ng, LANES, False, 1.0, None),
            (8, 0, o_ng + LANES, 2 * LRU_W, False, 1.0, None)]
    q, kc, vc, ksp, vsa, kwp, vwa, ng, rgx = _norm_proj(
        h, norm_w, w, segs,
        [(QW, BF16), (KW, F32), (KW, F32), (2 * KW, BF16), (2 * KW, BF16), (2 * KW, BF16), (2 * KW, BF16),
         (LANES, F32), (2 * LRU_W, F32)],
        rope_tabs=rope_tabs)
    nh = S // CMP_STRIDE
    pe = jnp.stack([k_pe.reshape(-1), v_pe.reshape(-1)]).astype(F32)
    pe = jnp.broadcast_to(pe[:, None, :], (2, SUBLANES, pe.shape[-1]))
    w1 = jnp.stack([k_w1, v_w1]).astype(BF16)
    w2 = jnp.stack([k_w2, v_w2]).astype(BF16)
    kcmp, vcmp = _compress(kc, vc, pe, w1, w2, B, S)
    n_cmp = (S - CMP_BLOCK) // CMP_STRIDE + 1
    ncp = -(-nh // LANES) * LANES
    if ncp != nh:
        kcmp, vcmp = (jnp.pad(t, ((0, 0), (0, 0), (0, ncp - nh), (0, 0))) for t in (kcmp, vcmp))
    n_sel = S // SEL_BLOCK
    cs = np.arange(ncp) * CMP_STRIDE
    ss = np.arange(LANES) * SEL_BLOCK
    ov = ((cs[:, None] <= ss[None, :] + SEL_BLOCK - 1) & (cs[:, None] + CMP_BLOCK - 1 >= ss[None, :])
          & (np.arange(ncp)[:, None] < n_cmp) & (np.arange(LANES)[None, :] < n_sel))
    overlap = jnp.asarray(ov.astype(np.float32))
    oc, bias = _nsa_cmp(q, kcmp, vcmp, overlap, B, S)
    blk_of_pos = np.arange(S) // SEL_BLOCK
    onehot = jnp.asarray((blk_of_pos[:, None] == np.arange(LANES)[None, :]).astype(np.float32), dtype=BF16)
    nsa = _nsa_main(q, bias, ksp, onehot, vsa, kwp, vwa, oc, ng, B, S)
    nblk, bw, _ = wa.shape
    eye = jnp.eye(nblk, dtype=wa.dtype)
    dense = lambda wb: (eye[:, None, :, None] * wb[:, :, None, :]).reshape(nblk * bw, nblk * bw)
    wab = jnp.concatenate([dense(wa), dense(wx)], axis=1).astype(BF16)
    bab = jnp.concatenate([ba, bx]).reshape(1, 2 * LRU_W).astype(F32)
    lru = _lru(rgx.reshape(B, S, 2 * LRU_W), conv_w.astype(F32), conv_b.reshape(1, LRU_W).astype(F32), wab, bab,
               lam.reshape(1, LRU_W).astype(F32))
    wo = w_out.astype(BF16)
    return nsa, lru.reshape(T, LRU_W), wo[:QW], wo[QW:]


def kernel(x, p, positions, even_norm_mix, even_w_in, even_fox_bf, even_gdn_conv_w, even_gdn_a_log, even_gdn_dt_bias, even_gdn_norm_w, even_w_out, odd_norm_mix, odd_w_in, odd_cmp_k_pe, odd_cmp_k_w1, odd_cmp_k_w2, odd_cmp_v_pe, odd_cmp_v_w1, odd_cmp_v_w2, odd_rg_conv_w, odd_rg_conv_b, odd_rg_wa, odd_rg_ba, odd_rg_wx, odd_rg_bx, odd_rg_lambda, odd_w_out, mlp_norm, mlp_w_up, mlp_w_down, ple_norm, ple_w_gate, ple_w_proj, final_norm):
    B, S, D = x.shape
    T = B * S
    depth = p.shape[0]
    h = x.reshape(T, D)
    rope_tabs = _rope_tables(positions) if depth > 1 else None
    for i in range(depth):
        j = i // 2
        if i % 2 == 0:
            mix = _even_mixer(h, B, S, even_norm_mix[j], even_w_in[j], even_fox_bf[j], even_gdn_conv_w[j],
                              even_gdn_a_log[j], even_gdn_dt_bias[j], even_gdn_norm_w[j], even_w_out[j])
        else:
            mix = _odd_mixer(h, B, S, rope_tabs, odd_norm_mix[j], odd_w_in[j], odd_cmp_k_pe[j], odd_cmp_k_w1[j],
                             odd_cmp_k_w2[j], odd_cmp_v_pe[j], odd_cmp_v_w1[j], odd_cmp_v_w2[j], odd_rg_conv_w[j],
                             odd_rg_conv_b[j], odd_rg_wa[j], odd_rg_ba[j], odd_rg_wx[j], odd_rg_bx[j],
                             odd_rg_lambda[j], odd_w_out[j])
        h = _layer_tail(h, *mix, mlp_norm[i], mlp_w_up[i].astype(BF16), mlp_w_down[i].astype(BF16),
                        p[i].reshape(T, -1), ple_norm[i], ple_w_gate[i].astype(BF16), ple_w_proj[i].astype(BF16),
                        final_norm, final=(i == depth - 1))
    return h.reshape(B, S, D)
```

```python
import functools

import numpy as np
import jax
import jax.numpy as jnp
from jax import lax
from jax.experimental import pallas as pl
from jax.experimental.pallas import tpu as pltpu

F32 = jnp.float32
BF16 = jnp.bfloat16
HIGHEST = lax.Precision.HIGHEST

NORM_EPS = 1e-6
LOG2E = float(np.log2(np.e))
NEG = -1e30
REMOVED = -3e38
LANES = 128
SUBLANES = 8
VMEM_LIMIT = 56 * 1024 * 1024

HEAD_DIM = 64
ROT_DIM = 16
ROPE_THETA = 500000.0
FOX_HEADS = 8
FOX_F_PIECES = 3
GDN_HEADS = 4
GDN_DIM = 128
GDN_CHUNK = 64
GDN_B_LANE = FOX_HEADS
GDN_A_LANE = FOX_HEADS + GDN_HEADS
NSA_HEADS = 8
NSA_GROUPS = 2
NSA_HPG = NSA_HEADS // NSA_GROUPS
CMP_BLOCK = 32
CMP_STRIDE = 16
CMP_HIDDEN = 128
SEL_BLOCK = 64
SEL_TOPK = 16
WINDOW = 512
LRU_W = 512
RG_C = 8.0
CONV_W = 4


def _cparams(*sem):
    return pltpu.CompilerParams(dimension_semantics=sem, vmem_limit_bytes=VMEM_LIMIT)


def _dot(a, b, **kw):
    return jnp.dot(a, b, preferred_element_type=F32, **kw)


def _dot_nt(a, b, **kw):
    return lax.dot_general(a, b, (((1,), (1,)), ((), ())), preferred_element_type=F32, **kw)


def _dot_tn(a, b, **kw):
    return lax.dot_general(a, b, (((0,), (0,)), ((), ())), preferred_element_type=F32, **kw)


def _softplus(x):
    return jnp.maximum(x, 0.0) + jnp.log1p(jnp.exp(-jnp.abs(x)))


def _sigmoid(x):
    return 1.0 / (1.0 + jnp.exp(-x))


def _silu(x):
    return x * _sigmoid(x)


def _gelu_tanh(x):
    return 0.5 * x * (1.0 + jnp.tanh(np.float32(np.sqrt(2.0 / np.pi)) * (x + 0.044715 * (x * x * x))))


def _rms(x, w):
    ms = jnp.mean(x * x, axis=-1, keepdims=True)
    return x * lax.rsqrt(ms + NORM_EPS) * w


def _apply_rope(y, c, s1, s2):
    outs = []
    for g in range(y.shape[1] // LANES):
        yg = y[:, g * LANES:(g + 1) * LANES]
        outs.append(yg * c + pltpu.roll(yg, LANES - ROT_DIM // 2, 1) * s1 + pltpu.roll(yg, ROT_DIM // 2, 1) * s2)
    return outs[0] if len(outs) == 1 else jnp.concatenate(outs, axis=1)


def _pad_heads(y, consts):
    tm = y.shape[0]
    lane = lax.broadcasted_iota(jnp.int32, (tm, LANES), 1)
    cst = jnp.zeros((tm, LANES), F32)
    for ln, val in consts:
        cst = jnp.where(lane == HEAD_DIM + ln, val, cst)
    low = lane < HEAD_DIM
    outs = []
    for g in range(y.shape[1] // LANES):
        yg = y[:, g * LANES:(g + 1) * LANES]
        outs.append(jnp.where(low, yg, cst))
        outs.append(jnp.where(low, pltpu.roll(yg, HEAD_DIM, 1), cst))
    return jnp.concatenate(outs, axis=1)


def _norm_proj_kernel(*refs, segs, n_out, rope):
    x_ref, nw_ref, w_ref = refs[:3]
    n_in = 3
    if rope:
        c_ref, s1_ref, s2_ref = refs[n_in:n_in + 3]
        n_in += 3
    out_refs = refs[n_in:n_in + n_out]
    xn = _rms(x_ref[...], nw_ref[...]).astype(BF16)
    for (oi, oc, wc, width, do_rope, scale, pad) in segs:
        y = _dot(xn, w_ref[:, wc:wc + width])
        if do_rope:
            y = _apply_rope(y, c_ref[...], s1_ref[...], s2_ref[...])
        if scale != 1.0:
            y = y * scale
        if pad is not None:
            y = _pad_heads(y, pad)
        out_refs[oi][:, oc:oc + y.shape[1]] = y.astype(out_refs[oi].dtype)


def _norm_proj(x, nw, w, segs, out_defs, rope_tabs=None, tm=512):
    T, D = x.shape
    N = w.shape[1]
    tm = min(tm, T)
    rope = rope_tabs is not None
    in_specs = [pl.BlockSpec((tm, D), lambda i: (i, 0)),
                pl.BlockSpec((1, D), lambda i: (0, 0)),
                pl.BlockSpec((D, N), lambda i: (0, 0))]
    args = [x, nw.reshape(1, D), w]
    if rope:
        in_specs += [pl.BlockSpec((tm, LANES), lambda i: (i, 0))] * 3
        args += list(rope_tabs)
    out_shape = [jax.ShapeDtypeStruct((T, wd), dt) for wd, dt in out_defs]
    out_specs = [pl.BlockSpec((tm, wd), lambda i: (i, 0)) for wd, _ in out_defs]
    return pl.pallas_call(
        functools.partial(_norm_proj_kernel, segs=tuple(segs), n_out=len(out_defs), rope=rope),
        grid=(T // tm,), in_specs=in_specs, out_specs=out_specs, out_shape=out_shape,
        compiler_params=_cparams("parallel"), name="norm_proj",
    )(*args)


def _rope_table_kernel(pos_ref, f_ref, e_ref, one_ref, c_ref, s1_ref, s2_ref):
    ang = pos_ref[0] * f_ref[...]
    cs = jnp.concatenate([jnp.cos(ang), jnp.sin(ang)], axis=0)
    hi = cs.astype(BF16)
    lo = (cs - hi.astype(F32)).astype(BF16)
    spread = lambda j: _dot_tn(hi, e_ref[j]) + _dot_tn(lo, e_ref[j])
    c_ref[...] = spread(0) + one_ref[...]
    s1_ref[...] = spread(1)
    s2_ref[...] = spread(2)


def _rope_tables(positions):
    B, S = positions.shape
    half = ROT_DIM // 2
    inv_freq = ROPE_THETA ** (-jnp.arange(half, dtype=F32) * (2.0 / ROT_DIM))
    pos = positions.astype(F32).reshape(B, 1, S)
    lane = np.arange(LANES)
    in_head, freq = lane % HEAD_DIM, lane % half
    e = np.zeros((3, 2 * half, LANES), np.float32)
    e[0, freq, lane] = in_head < ROT_DIM
    e[1, half + freq, lane] = -1.0 * (in_head < half)
    e[2, half + freq, lane] = 1.0 * ((in_head >= half) & (in_head < ROT_DIM))
    ones = (in_head >= ROT_DIM).astype(np.float32).reshape(1, LANES)
    sh = jax.ShapeDtypeStruct((B * S, LANES), F32)
    spec = pl.BlockSpec((S, LANES), lambda b: (b, 0))
    return pl.pallas_call(
        _rope_table_kernel, grid=(B,),
        in_specs=[pl.BlockSpec((1, 1, S), lambda b: (b, 0, 0)), pl.BlockSpec((half, 1), lambda b: (0, 0)),
                  pl.BlockSpec((3, 2 * half, LANES), lambda b: (0, 0, 0)), pl.BlockSpec((1, LANES), lambda b: (0, 0))],
        out_specs=[spec, spec, spec], out_shape=[sh, sh, sh],
        compiler_params=_cparams("parallel"), name="rope_tables",
    )(pos, inv_freq.reshape(half, 1), jnp.asarray(e, dtype=BF16), jnp.asarray(ones))


def _tail_kernel(h_ref, a1_ref, a2_ref, w1_ref, w2_ref, nwm_ref, wu_ref, wd_ref, p_ref, nwp_ref, wg_ref, wp_ref,
                 fw_ref, o_ref, *, final):
    hm = h_ref[...] + _dot(a1_ref[...], w1_ref[...]) + _dot(a2_ref[...], w2_ref[...])
    u = jnp.maximum(_dot(_rms(hm, nwm_ref[...]).astype(BF16), wu_ref[...]), 0.0)
    h2 = hm + _dot((u * u).astype(BF16), wd_ref[...])
    gate = _sigmoid(_dot(_rms(h2, nwp_ref[...]).astype(BF16), wg_ref[...]))
    y = h2 + gate * _dot(p_ref[...].astype(BF16), wp_ref[...])
    if final:
        y = _rms(y, fw_ref[...])
    o_ref[...] = y


def _layer_tail(h, a1, a2, w1, w2, nwm, wu, wd, p, layer, nwp, wg, wp, fw, final, tm=512):
    T, D = h.shape
    FF = wu.shape[1]
    P = p.shape[1]
    tm = min(tm, T)
    nt = T // tm
    K1, K2 = w1.shape[0], w2.shape[0]
    row = lambda w: pl.BlockSpec((tm, w), lambda i: (i, 0))
    full = lambda r, c: pl.BlockSpec((r, c), lambda i: (0, 0))
    return pl.pallas_call(
        functools.partial(_tail_kernel, final=final), grid=(nt,),
        in_specs=[row(D), row(K1), row(K2), full(K1, D), full(K2, D), full(1, D), full(D, FF), full(FF, D),
                  pl.BlockSpec((tm, P), lambda i: (layer * nt + i, 0)), full(1, D), full(D, D), full(P, D), full(1, D)],
        out_specs=row(D),
        out_shape=jax.ShapeDtypeStruct((T, D), F32),
        compiler_params=_cparams("parallel"), name="layer_tail",
    )(h, a1, a2, w1, w2, nwm.reshape(1, D), wu, wd, p, nwp.reshape(1, D), wg, wp, fw.reshape(1, D))


def _fox_gate_kernel(x_ref, b_ref, o_ref, carry_ref, *, ts):
    @pl.when(pl.program_id(1) == 0)
    def _():
        carry_ref[...] = jnp.zeros_like(carry_ref)

    r = lax.broadcasted_iota(jnp.int32, (LANES, LANES), 0)
    c = lax.broadcasted_iota(jnp.int32, (LANES, LANES), 1)
    lower = (r >= c).astype(F32)
    lane = lax.broadcasted_iota(jnp.int32, (LANES, LANES), 1)

    nblk = ts // LANES
    within = []
    for n in range(nblk):
        x = x_ref[n * LANES:(n + 1) * LANES, :] + b_ref[...]
        logf = jnp.minimum(x, 0.0) - jnp.log1p(jnp.exp(-jnp.abs(x)))
        within.append(_dot(lower, logf, precision=HIGHEST))
    carry = carry_ref[0:1, :]
    for n in range(nblk):
        cum = within[n] + carry
        carry = cum[LANES - 1:LANES, :]
        f = cum * LOG2E
        hi = f.astype(BF16).astype(F32)
        mid = (f - hi).astype(BF16).astype(F32)
        lo = f - hi - mid
        for h in range(FOX_HEADS):
            v = jnp.where(lane == HEAD_DIM, hi[:, h:h + 1],
                          jnp.where(lane == HEAD_DIM + 1, mid[:, h:h + 1],
                                    jnp.where(lane == HEAD_DIM + 2, lo[:, h:h + 1], 0.0)))
            o_ref[n * LANES:(n + 1) * LANES, h * LANES:(h + 1) * LANES] = v.astype(o_ref.dtype)
    carry_ref[0:1, :] = carry


def _fox_gate(sm, bias, B, S, ts=1024):
    T = B * S
    ts = min(ts, S)
    ns = S // ts
    b = jnp.zeros((1, LANES), F32).at[0, :FOX_HEADS].set(bias.astype(F32))
    return pl.pallas_call(
        functools.partial(_fox_gate_kernel, ts=ts), grid=(B, ns),
        in_specs=[pl.BlockSpec((ts, LANES), lambda bb, n: (bb * ns + n, 0)),
                  pl.BlockSpec((1, LANES), lambda bb, n: (0, 0))],
        out_specs=pl.BlockSpec((ts, FOX_HEADS * LANES), lambda bb, n: (bb * ns + n, 0)),
        out_shape=jax.ShapeDtypeStruct((T, FOX_HEADS * LANES), BF16),
        scratch_shapes=[pltpu.VMEM((SUBLANES, LANES), F32)],
        compiler_params=_cparams("parallel", "arbitrary"), name="fox_gate",
    )(sm, b)


def _fox_attn_kernel(q_ref, k_ref, v_ref, f_ref, o_ref, *, tq, tk, dk, hp):
    i = pl.program_id(2)
    qs = [q_ref[:, hh * LANES:(hh + 1) * LANES] for hh in range(hp)]
    nsub = tq // tk

    def scores(start, size, r0):
        rows = pl.ds(start, size)
        return [_dot_nt(qs[hh][r0:], k_ref[rows, hh * LANES:(hh + 1) * LANES] + f_ref[rows, hh * LANES:(hh + 1) * LANES])
                for hh in range(hp)]

    def finish(ss, start, size, r0, carry, masked):
        rows = pl.ds(start, size)
        out = []
        for hh in range(hp):
            m, acc = carry[hh]
            s = ss[hh]
            if masked:
                r = lax.broadcasted_iota(jnp.int32, (tq - r0, size), 0)
                c = lax.broadcasted_iota(jnp.int32, (tq - r0, size), 1)
                s = jnp.where(c <= r, s, NEG)
            m_new = jnp.maximum(m[r0:], jnp.max(s, axis=1, keepdims=True))
            pr = jnp.exp2(s - m_new).astype(BF16)
            acc_new = jnp.exp2(m[r0:] - m_new) * acc[r0:] + _dot(pr, v_ref[rows, hh * LANES:(hh + 1) * LANES])
            if r0:
                m_new = jnp.concatenate([m[:r0], m_new], axis=0)
                acc_new = jnp.concatenate([acc[:r0], acc_new], axis=0)
            out.append((m_new, acc_new))
        return tuple(out)

    def tile(start, size, r0, carry, masked):
        return finish(scores(start, size, r0), start, size, r0, carry, masked)

    init = tuple((jnp.full((tq, 1), NEG, F32), jnp.zeros((tq, LANES), F32)) for _ in range(hp))
    carry = lax.fori_loop(0, i * nsub, lambda j, c: tile(pl.multiple_of(j * tk, tk), tk, 0, c, False), init)
    dstarts = [pl.multiple_of(i * tq + d * dk, dk) for d in range(tq // dk)]
    dss = [scores(dstarts[d], dk, d * dk) for d in range(tq // dk)]
    for d in range(tq // dk):
        carry = finish(dss[d], dstarts[d], dk, d * dk, carry, True)
    o_ref[...] = jnp.concatenate([acc[:, :HEAD_DIM] / acc[:, HEAD_DIM:HEAD_DIM + 1] for (_, acc) in carry],
                                 axis=1).astype(o_ref.dtype)


def _fox_attn(q, k, v, f3, B, S, tq=1024, tk=1024, dk=512, hp=2):
    T = B * S
    tq = min(tq, S)
    tk = min(tk, tq)
    dk = min(dk, tq)
    nq = S // tq
    bw = hp * LANES
    ng = FOX_HEADS // hp
    res = pl.BlockSpec((S, bw), lambda b, p, i: (b, p))
    return pl.pallas_call(
        functools.partial(_fox_attn_kernel, tq=tq, tk=tk, dk=dk, hp=hp), grid=(B, ng, nq),
        in_specs=[pl.BlockSpec((tq, bw), lambda b, p, i: (b * nq + i, p)), res, res, res],
        out_specs=pl.BlockSpec((tq, hp * HEAD_DIM), lambda b, p, i: (b * nq + i, p)),
        out_shape=jax.ShapeDtypeStruct((T, FOX_HEADS * HEAD_DIM), BF16),
        compiler_params=_cparams("parallel", "parallel", "arbitrary"), name="fox_attn",
    )(q, k, v, f3)


def _cumsum_rows(x):
    n = x.shape[0]
    row = lax.broadcasted_iota(jnp.int32, x.shape, 0)
    d = 1
    while d < n:
        x = x + jnp.where(row >= d, pltpu.roll(x, d, 0), 0.0)
        d *= 2
    return x


def _gdn_kernel(x_ref, z_ref, sm_ref, cw_ref, par_ref, nw_ref, o_ref, state_ref, xp_ref, *, cps):
    C = GDN_CHUNK
    D = GDN_DIM
    W = GDN_HEADS * D
    R = cps * C
    n = pl.program_id(1)

    @pl.when(n == 0)
    def _():
        state_ref[...] = jnp.zeros_like(state_ref)
        xp_ref[0:SUBLANES, :] = jnp.zeros((SUBLANES, 3 * W), F32)

    x = x_ref[...]
    xp_ref[SUBLANES:SUBLANES + R, :] = x
    cw = cw_ref[...]
    y = x * cw[CONV_W - 1:CONV_W]
    for j in range(CONV_W - 1):
        off = SUBLANES - (CONV_W - 1) + j
        y = y + xp_ref[off:off + R, :] * cw[j:j + 1]
    xp_ref[0:SUBLANES, :] = x[R - SUBLANES:R]
    y = _silu(y)

    sm = sm_ref[...]
    beta_all = _sigmoid(sm)
    g_all = -jnp.exp(par_ref[0:1, :]) * _softplus(sm + par_ref[1:2, :])
    ri = lax.broadcasted_iota(jnp.int32, (C, C), 0)
    ci = lax.broadcasted_iota(jnp.int32, (C, C), 1)
    incl = ri >= ci
    strict = ri > ci

    chains = [(c, h) for c in range(cps) for h in range(GDN_HEADS)]
    gcs = [_cumsum_rows(g_all[c * C:(c + 1) * C]) for c in range(cps)]
    gcts = [gc.T for gc in gcs]
    pre = {}
    for (c, h) in chains:
        r0 = c * C
        q = y[r0:r0 + C, h * D:(h + 1) * D]
        k = y[r0:r0 + C, W + h * D:W + (h + 1) * D]
        v = y[r0:r0 + C, 2 * W + h * D:2 * W + (h + 1) * D]
        qn = q * lax.rsqrt(jnp.sum(q * q, axis=-1, keepdims=True) + 1e-6) * (D ** -0.5)
        kn = k * lax.rsqrt(jnp.sum(k * k, axis=-1, keepdims=True) + 1e-6)
        beta = beta_all[r0:r0 + C, GDN_B_LANE + h:GDN_B_LANE + h + 1]
        gcol = gcs[c][:, GDN_A_LANE + h:GDN_A_LANE + h + 1]
        grow = gcts[c][GDN_A_LANE + h:GDN_A_LANE + h + 1, :]
        decay = jnp.where(incl, jnp.exp(jnp.where(incl, gcol - grow, 0.0)), 0.0)
        kb = kn * beta
        g_last = gcol[C - 1:C, :]
        pre[c, h] = dict(qn16=qn.astype(BF16), kn16=kn.astype(BF16), kb16=kb.astype(BF16), decay=decay,
                         rhs=jnp.concatenate([v * beta, kb * jnp.exp(gcol)], axis=1),
                         qg16=(qn * jnp.exp(gcol)).astype(BF16),
                         kg16=(kn * jnp.exp(g_last - gcol)).astype(BF16), e_last=jnp.exp(g_last))
    kk = {ch: _dot_nt(pre[ch]["kb16"], pre[ch]["kn16"]) for ch in chains}
    qk = {ch: _dot_nt(pre[ch]["qn16"], pre[ch]["kn16"]) for ch in chains}
    P = {ch: -jnp.where(strict, kk[ch] * pre[ch]["decay"], 0.0) for ch in chains}
    X = dict(P)
    for _ in range(5):
        P = {ch: _dot(P[ch].astype(BF16), P[ch].astype(BF16)) for ch in chains}
        XP = {ch: _dot(X[ch].astype(BF16), P[ch].astype(BF16)) for ch in chains}
        X = {ch: X[ch] + P[ch] + XP[ch] for ch in chains}
    sol = {ch: pre[ch]["rhs"] + _dot(X[ch].astype(BF16), pre[ch]["rhs"].astype(BF16)) for ch in chains}
    attn16 = {ch: jnp.where(incl, qk[ch] * pre[ch]["decay"], 0.0).astype(BF16) for ch in chains}

    heads = range(GDN_HEADS)
    states = [state_ref[h] for h in heads]
    for c in range(cps):
        r0 = c * C
        st16 = [states[h].astype(BF16) for h in heads]
        ws = [_dot(sol[c, h][:, D:].astype(BF16), st16[h]) for h in heads]
        qs = [_dot(pre[c, h]["qg16"], st16[h]) for h in heads]
        v_new = [(sol[c, h][:, :D] - ws[h]).astype(BF16) for h in heads]
        av = [_dot(attn16[c, h], v_new[h]) for h in heads]
        kv = [_dot_tn(pre[c, h]["kg16"], v_new[h]) for h in heads]
        outs = []
        for h in heads:
            states[h] = states[h] * pre[c, h]["e_last"] + kv[h]
            o = _rms(qs[h] + av[h], nw_ref[...]) * _silu(z_ref[r0:r0 + C, h * D:(h + 1) * D])
            outs.append(o)
        o_ref[r0:r0 + C, :] = jnp.concatenate(outs, axis=1).astype(o_ref.dtype)
    for h in heads:
        state_ref[h] = states[h]


def _gdn(gqkv, gz, small, conv_w, a_log, dt_bias, norm_w, B, S, cps=8):
    T = B * S
    C = GDN_CHUNK
    R = cps * C
    N = S // R
    W = GDN_HEADS * GDN_DIM
    par = jnp.zeros((SUBLANES, LANES), F32)
    lanes = slice(GDN_A_LANE, GDN_A_LANE + GDN_HEADS)
    par = par.at[0, lanes].set(a_log.astype(F32)).at[1, lanes].set(dt_bias.astype(F32))
    return pl.pallas_call(
        functools.partial(_gdn_kernel, cps=cps), grid=(B, N),
        in_specs=[pl.BlockSpec((R, 3 * W), lambda b, n: (b * N + n, 0)),
                  pl.BlockSpec((R, W), lambda b, n: (b * N + n, 0)),
                  pl.BlockSpec((R, LANES), lambda b, n: (b * N + n, 0)),
                  pl.BlockSpec((CONV_W, 3 * W), lambda b, n: (0, 0)),
                  pl.BlockSpec((SUBLANES, LANES), lambda b, n: (0, 0)),
                  pl.BlockSpec((1, GDN_DIM), lambda b, n: (0, 0))],
        out_specs=pl.BlockSpec((R, W), lambda b, n: (b * N + n, 0)),
        out_shape=jax.ShapeDtypeStruct((T, W), BF16),
        scratch_shapes=[pltpu.VMEM((GDN_HEADS, GDN_DIM, GDN_DIM), F32),
                        pltpu.VMEM((SUBLANES + R, 3 * W), F32)],
        compiler_params=_cparams("parallel", "arbitrary"), name="gdn",
    )(gqkv, gz, small, conv_w.astype(F32), par, norm_w.reshape(1, GDN_DIM).astype(F32))


def _compress_kernel(x_ref, pe_ref, w1_ref, wbd_ref, w2_ref, o_ref, *, nh):
    ab = jnp.zeros((nh, NSA_GROUPS * 2 * CMP_HIDDEN), F32)
    for l in range(CMP_STRIDE):
        ab = ab + _dot(x_ref[pl.ds(l, nh, stride=CMP_STRIDE), :].astype(BF16), wbd_ref[0, l])
    lane = lax.broadcasted_iota(jnp.int32, (nh, HEAD_DIM), 1)
    one_col = jnp.where(lane == 0, 1.0, 0.0)
    c = _dot(pe_ref[0].astype(BF16), w1_ref[0])
    for g in range(NSA_GROUPS):
        a = ab[:, 2 * g * CMP_HIDDEN:(2 * g + 1) * CMP_HIDDEN]
        b = ab[:, (2 * g + 1) * CMP_HIDDEN:(2 * g + 2) * CMP_HIDDEN]
        hid = _gelu_tanh(a + pltpu.roll(b, nh - 1, 0) + c[0:1])
        out = _dot(hid.astype(BF16), w2_ref[0])
        o_ref[0, g] = jnp.concatenate([out, one_col], axis=1).astype(o_ref.dtype)


def _compress(kc, vc, pe, w1, w2, B, S):
    nh = S // CMP_STRIDE
    G = NSA_GROUPS
    half = CMP_STRIDE * HEAD_DIM
    w1r = w1.reshape(2, 2, CMP_STRIDE, HEAD_DIM, CMP_HIDDEN)
    per_tok = jnp.concatenate([w1r[:, 0], w1r[:, 1]], axis=-1)
    eye = jnp.eye(G, dtype=w1.dtype)
    wbd = per_tok[:, :, None, :, None, :] * eye[None, None, :, None, :, None]
    wbd = wbd.reshape(2, CMP_STRIDE, G * HEAD_DIM, G * 2 * CMP_HIDDEN)
    x_spec = pl.BlockSpec((S, G * HEAD_DIM), lambda b, j: (b, 0))
    out = [pl.pallas_call(
        functools.partial(_compress_kernel, nh=nh), grid=(B, 1),
        in_specs=[x_spec, pl.BlockSpec((1, SUBLANES, 2 * half), lambda b, j, kv=kv: (kv, 0, 0)),
                  pl.BlockSpec((1, 2 * half, CMP_HIDDEN), lambda b, j, kv=kv: (kv, 0, 0)),
                  pl.BlockSpec((1, CMP_STRIDE, G * HEAD_DIM, G * 2 * CMP_HIDDEN), lambda b, j, kv=kv: (kv, 0, 0, 0)),
                  pl.BlockSpec((1, CMP_HIDDEN, HEAD_DIM), lambda b, j, kv=kv: (kv, 0, 0))],
        out_specs=pl.BlockSpec((1, G, nh, LANES), lambda b, j: (b, 0, 0, 0)),
        out_shape=jax.ShapeDtypeStruct((B, G, nh, LANES), BF16),
        compiler_params=_cparams("parallel", "arbitrary"), name="nsa_compress",
    )(x, pe, w1, wbd, w2) for kv, x in enumerate((kc, vc))]
    return out


def _stack_heads(qb):
    return jnp.concatenate([qb[:, h * HEAD_DIM:(h + 1) * HEAD_DIM] for h in range(NSA_HPG)], axis=0)


def _unstack_heads(o, tq):
    return jnp.concatenate([o[h * tq:(h + 1) * tq] for h in range(NSA_HPG)], axis=1)


def _nsa_cmp_kernel(q_ref, kc_ref, vc_ref, ov_ref, oc_ref, bias_ref, *, tq, n_sel, top_k):
    i = pl.program_id(1)
    G = NSA_GROUPS
    GW = NSA_HPG * HEAD_DIM
    ncp = kc_ref.shape[2]
    row = lax.broadcasted_iota(jnp.int32, (NSA_HPG * tq, 1), 0)
    t4 = i * tq + (row & (tq - 1))
    cmp_end = lax.broadcasted_iota(jnp.int32, (1, ncp), 1) * CMP_STRIDE + (CMP_BLOCK - 1)
    valid = cmp_end <= t4
    ss = [_dot_nt(_stack_heads(q_ref[:, g * GW:(g + 1) * GW]), kc_ref[0, g, :, :HEAD_DIM]) for g in range(G)]
    ps, accs = [], []
    for g in range(G):
        s = jnp.where(valid, ss[g], NEG)
        m = jnp.max(s, axis=1, keepdims=True)
        p = jnp.exp2(s - m)
        ps.append(p)
        accs.append(_dot(p.astype(BF16), vc_ref[0, g]))
    imps = []
    for g in range(G):
        inv = jnp.where(t4 >= CMP_BLOCK - 1, 1.0 / accs[g][:, HEAD_DIM:HEAD_DIM + 1], 0.0)
        oc_ref[:, g * GW:(g + 1) * GW] = _unstack_heads(accs[g][:, :HEAD_DIM] * inv, tq)
        p = ps[g] * inv
        psum = p[0:tq] + p[tq:2 * tq] + p[2 * tq:3 * tq] + p[3 * tq:4 * tq]
        imps.append(_dot(psum, ov_ref[...], precision=HIGHEST))
    blk = lax.broadcasted_iota(jnp.int32, (1, LANES), 1)
    t = i * tq + lax.broadcasted_iota(jnp.int32, (tq, 1), 0)
    cur = t >> (SEL_BLOCK.bit_length() - 1)
    forced = (blk == 0) | (blk == cur) | (blk == cur - 1)
    future = blk * SEL_BLOCK > t
    blk_t = lax.broadcasted_iota(jnp.int32, (LANES, tq), 0).astype(F32)
    for g in range(G):
        imp = jnp.where(future, NEG, jnp.where(forced, REMOVED, imps[g]))
        imp = jnp.where(blk < n_sel, imp, REMOVED)
        imp_t = imp.T
        for _ in range(max(top_k - 3, 0)):
            mx = jnp.max(imp_t, axis=0, keepdims=True)
            first = jnp.min(jnp.where(imp_t == mx, blk_t, float(LANES)), axis=0, keepdims=True)
            imp_t = jnp.where(blk_t == first, REMOVED, imp_t)
        sel = jnp.where(imp_t == REMOVED, 1.0, 0.0).T
        bias_ref[0, g] = jnp.where((sel > 0.0) & jnp.logical_not(future), 0.0, NEG).astype(bias_ref.dtype)


def _nsa_cmp(q, kc, vc, overlap, B, S, tq=512):
    T = B * S
    tq = min(tq, S)
    nq = S // tq
    ncp = kc.shape[2]
    n_sel = S // SEL_BLOCK
    QW = NSA_HEADS * HEAD_DIM
    G = NSA_GROUPS
    return pl.pallas_call(
        functools.partial(_nsa_cmp_kernel, tq=tq, n_sel=n_sel, top_k=min(SEL_TOPK, n_sel)),
        grid=(B, nq),
        in_specs=[pl.BlockSpec((tq, QW), lambda b, i: (b * nq + i, 0)),
                  pl.BlockSpec((1, G, ncp, LANES), lambda b, i: (b, 0, 0, 0)),
                  pl.BlockSpec((1, G, ncp, LANES), lambda b, i: (b, 0, 0, 0)),
                  pl.BlockSpec((ncp, LANES), lambda b, i: (0, 0))],
        out_specs=[pl.BlockSpec((tq, QW), lambda b, i: (b * nq + i, 0)),
                   pl.BlockSpec((1, G, tq, LANES), lambda b, i: (b, 0, i, 0))],
        out_shape=[jax.ShapeDtypeStruct((T, QW), F32),
                   jax.ShapeDtypeStruct((B, G, S, LANES), BF16)],
        compiler_params=_cparams("parallel", "parallel"), name="nsa_cmp",
    )(q, kc, vc, overlap)


def _nsa_main_kernel(q_ref, bias_ref, ks_ref, oh_ref, vs_ref, kw_ref, vw_ref, oc_ref, g_ref, o_ref, *, qn, tk, wspan):
    R = NSA_HPG * qn
    tb = 2 * tk
    i = pl.program_id(2)
    q4 = _stack_heads(q_ref[...])
    q4p = jnp.concatenate([q4, jnp.zeros_like(q4)], axis=1)
    b4 = jnp.concatenate([bias_ref[0, 0]] * NSA_HPG, axis=0)
    qa = jnp.concatenate([q4p, b4], axis=1)
    t4 = i * qn + (lax.broadcasted_iota(jnp.int32, (R, 1), 0) & (qn - 1))

    def scores(start, size):
        rows = pl.ds(start, size)
        ka = jnp.concatenate([ks_ref[rows, :], oh_ref[rows, :]], axis=1)
        return _dot_nt(qa, ka)

    def finish(s, start, size, carry, masked):
        m, acc = carry
        if masked:
            kpos = start + lax.broadcasted_iota(jnp.int32, (1, size), 1)
            s = jnp.where(kpos <= t4, s, NEG)
        m_new = jnp.maximum(m, jnp.max(s, axis=1, keepdims=True))
        pr = jnp.exp2(s - m_new).astype(BF16)
        acc = jnp.exp2(m - m_new) * acc + _dot(pr, vs_ref[pl.ds(start, size), :])
        return m_new, acc

    def step(start, size, carry, masked):
        return finish(scores(start, size), start, size, carry, masked)

    first = i * qn
    nbig = first // tb
    diag = (first // tk) * tk
    nsmall = (diag - nbig * tb) // tk
    init = (jnp.full((R, 1), NEG, F32), jnp.zeros((R, LANES), F32))
    carry = lax.fori_loop(0, nbig, lambda j, c: step(pl.multiple_of(j * tb, tb), tb, c, False), init)
    carry = lax.fori_loop(0, nsmall, lambda j, c: step(pl.multiple_of(nbig * tb, tk), tk, c, False), carry)
    dstart = pl.multiple_of(diag, tk)
    s_d = scores(dstart, tk)
    wrows = pl.ds(pl.multiple_of(jnp.maximum(first + qn - wspan, 0), qn), wspan)
    s_w = _dot_nt(q4p, kw_ref[wrows, :])
    _, acc_s = finish(s_d, dstart, tk, carry, True)
    kpos = jnp.maximum(first + qn - wspan, 0) + lax.broadcasted_iota(jnp.int32, (1, wspan), 1)
    wmask = lax.bitcast_convert_type(t4 - kpos, jnp.uint32) < jnp.uint32(WINDOW)
    s_w = jnp.where(wmask, s_w, NEG)
    m_w = jnp.max(s_w, axis=1, keepdims=True)
    acc_w = _dot(jnp.exp2(s_w - m_w).astype(BF16), vw_ref[wrows, :])

    gates = _sigmoid(g_ref[...])
    first_group = pl.program_id(1) == 0
    ng = NSA_HPG * 3

    def gate(h, branch):
        j = 3 * h + branch
        return jnp.where(first_group, gates[:, j:j + 1], gates[:, ng + j:ng + j + 1])

    oc = oc_ref[...]
    outs = []
    for h in range(NSA_HPG):
        hs = slice(h * qn, (h + 1) * qn)
        c_s = gate(h, 1) / acc_s[hs, HEAD_DIM:HEAD_DIM + 1]
        c_w = gate(h, 2) / acc_w[hs, HEAD_DIM:HEAD_DIM + 1]
        outs.append(gate(h, 0) * oc[:, h * HEAD_DIM:(h + 1) * HEAD_DIM]
                    + c_s * acc_s[hs, :HEAD_DIM] + c_w * acc_w[hs, :HEAD_DIM])
    o_ref[...] = jnp.concatenate(outs, axis=1).astype(o_ref.dtype)


def _nsa_main(q, bias, ksp, onehot, vsa, kwp, vwa, oc, gates, B, S, qn=256, tk=512):
    T = B * S
    nq = S // qn
    tk = min(tk, S // 2)
    wspan = min(WINDOW + qn, S)
    GW = NSA_HPG * HEAD_DIM
    res = pl.BlockSpec((S, LANES), lambda b, g, i: (b, g))
    return pl.pallas_call(
        functools.partial(_nsa_main_kernel, qn=qn, tk=tk, wspan=wspan), grid=(B, NSA_GROUPS, nq),
        in_specs=[pl.BlockSpec((qn, GW), lambda b, g, i: (b * nq + i, g)),
                  pl.BlockSpec((1, 1, qn, LANES), lambda b, g, i: (b, g, i, 0)),
                  res, pl.BlockSpec((S, LANES), lambda b, g, i: (0, 0)), res, res, res,
                  pl.BlockSpec((qn, GW), lambda b, g, i: (b * nq + i, g)),
                  pl.BlockSpec((qn, LANES), lambda b, g, i: (b * nq + i, 0))],
        out_specs=pl.BlockSpec((qn, GW), lambda b, g, i: (b * nq + i, g)),
        out_shape=jax.ShapeDtypeStruct((T, NSA_HEADS * HEAD_DIM), BF16),
        compiler_params=_cparams("parallel", "parallel", "arbitrary"), name="nsa_main",
    )(q, bias, ksp, onehot, vsa, kwp, vwa, oc, gates)


def _lru_kernel(gx_ref, cw_ref, cb_ref, wab_ref, bab_ref, lam_ref, o_ref, h_ref, tail_ref, a_ref, b_ref, *, ts):
    n = pl.program_id(0)
    Bb = gx_ref.shape[0]
    W = gx_ref.shape[2] // 2

    @pl.when(n == 0)
    def _():
        h_ref[...] = jnp.zeros_like(h_ref)
        tail_ref[...] = jnp.zeros_like(tail_ref)

    gx = jnp.swapaxes(gx_ref[...], 0, 1)
    xin = gx[:, :, W:]
    xp = jnp.concatenate([tail_ref[...], xin], axis=0)
    cw = cw_ref[...]
    x = cb_ref[...].reshape(1, 1, W)
    for j in range(CONV_W):
        x = x + xp[j:j + ts] * cw[j:j + 1].reshape(1, 1, W)
    tail_ref[...] = xin[ts - (CONV_W - 1):ts]
    x2 = x.reshape(ts * Bb, W)
    pre = _dot(x2.astype(BF16), wab_ref[...]) + bab_ref[...]
    r = _sigmoid(pre[:, :W])
    ig = _sigmoid(pre[:, W:])
    log_a = (-RG_C * _softplus(-lam_ref[...])) * r
    a = jnp.exp(log_a)
    th = jnp.tanh(log_a)
    bb = jnp.sqrt(-2.0 * th / (1.0 - th)) * (ig * x2)
    a_ref[...] = a.reshape(ts, Bb, W)
    b_ref[...] = bb.reshape(ts, Bb, W)

    def scan(t, h):
        h = a_ref[t] * h + b_ref[t]
        b_ref[t] = h
        return h

    h_ref[...] = lax.fori_loop(0, ts, scan, h_ref[...], unroll=8)
    y = b_ref[...] * _gelu_tanh(gx[:, :, :W])
    o_ref[...] = jnp.swapaxes(y, 0, 1).astype(o_ref.dtype)


def _lru(gx, conv_w, conv_b, wab, bab, lam, ts=128):
    Bb, S, W2 = gx.shape
    W = W2 // 2
    ts = min(ts, S)
    full = lambda shape: pl.BlockSpec(shape, lambda n: (0,) * len(shape))
    return pl.pallas_call(
        functools.partial(_lru_kernel, ts=ts), grid=(S // ts,),
        in_specs=[pl.BlockSpec((Bb, ts, W2), lambda n: (0, n, 0)),
                  full((CONV_W, W)), full((1, W)), full((W, 2 * W)), full((1, 2 * W)), full((1, W))],
        out_specs=pl.BlockSpec((Bb, ts, W), lambda n: (0, n, 0)),
        out_shape=jax.ShapeDtypeStruct((Bb, S, W), BF16),
        scratch_shapes=[pltpu.VMEM((Bb, W), F32), pltpu.VMEM((CONV_W - 1, Bb, W), F32),
                        pltpu.VMEM((ts, Bb, W), F32), pltpu.VMEM((ts, Bb, W), F32)],
        compiler_params=_cparams("arbitrary"), name="rg_lru",
    )(gx, conv_w, conv_b, wab, bab, lam)


def _even_mixer(h, B, S, norm_w, w_in, fox_bf, conv_w, a_log, dt_bias, gdn_norm_w, w_out):
    T, D = h.shape
    FW = FOX_HEADS * HEAD_DIM
    GW = GDN_HEADS * GDN_DIM
    o_ff = 3 * FW
    o_g = o_ff + FOX_HEADS
    o_gb = o_g + 4 * GW
    small = jnp.concatenate([w_in[:, o_ff:o_g], w_in[:, o_gb:o_gb + 2 * GDN_HEADS],
                             jnp.zeros((D, LANES - FOX_HEADS - 2 * GDN_HEADS), w_in.dtype)], axis=1)
    w = jnp.concatenate([w_in[:, :o_ff], w_in[:, o_g:o_gb], small], axis=1).astype(BF16)
    segs = [(0, 0, 0, FW, False, HEAD_DIM ** -0.5 * LOG2E, tuple((j, -1.0) for j in range(FOX_F_PIECES))),
            (1, 0, FW, FW, False, 1.0, ()),
            (2, 0, 2 * FW, FW, False, 1.0, ((0, 1.0),)),
            (3, 0, 3 * FW, 3 * GW, False, 1.0, None),
            (4, 0, 3 * FW + 3 * GW, GW, False, 1.0, None),
            (5, 0, 3 * FW + 4 * GW, LANES, False, 1.0, None)]
    fq, fk, fv, gqkv, gz, sm = _norm_proj(h, norm_w, w, segs,
                                          [(2 * FW, BF16), (2 * FW, BF16), (2 * FW, BF16),
                                           (3 * GW, F32), (GW, F32), (LANES, F32)])
    f3 = _fox_gate(sm, fox_bf, B, S)
    fox = _fox_attn(fq, fk, fv, f3, B, S)
    gdn = _gdn(gqkv, gz, sm, conv_w, a_log, dt_bias, gdn_norm_w, B, S)
    wo = w_out.astype(BF16)
    return fox, gdn, wo[:FW], wo[FW:]


def _odd_mixer(h, B, S, rope_tabs, norm_w, w_in, k_pe, k_w1, k_w2, v_pe, v_w1, v_w2,
               conv_w, conv_b, wa, ba, wx, bx, lam, w_out):
    T, D = h.shape
    QW = NSA_HEADS * HEAD_DIM
    KW = NSA_GROUPS * HEAD_DIM
    G = NSA_GROUPS
    o_ng = QW + 6 * KW
    ngw = NSA_HEADS * 3
    w = jnp.concatenate([w_in[:, :o_ng], w_in[:, o_ng:o_ng + ngw], jnp.zeros((D, LANES - ngw), w_in.dtype),
                         w_in[:, o_ng + ngw:]], axis=1).astype(BF16)
    ones = ((0, 1.0),)
    segs = [(0, 0, 0, QW, True, HEAD_DIM ** -0.5 * LOG2E, None),
            (1, 0, QW, KW, True, 1.0, None),
            (2, 0, QW + KW, KW, False, 1.0, None),
            (3, 0, QW + 2 * KW, KW, True, 1.0, ()),
            (4, 0, QW + 3 * KW, KW, False, 1.0, ones),
            (5, 0, QW + 4 * KW, KW, True, 1.0, ()),
            (6, 0, QW + 5 * KW, KW, False, 1.0, ones),
            (7, 0, o_ng, LANES, False, 1.0, None),
            (8, 0, o_ng + LANES, 2 * LRU_W, False, 1.0, None)]
    q, kc, vc, ksp, vsa, kwp, vwa, ng, rgx = _norm_proj(
        h, norm_w, w, segs,
        [(QW, BF16), (KW, F32), (KW, F32), (2 * KW, BF16), (2 * KW, BF16), (2 * KW, BF16), (2 * KW, BF16),
         (LANES, F32), (2 * LRU_W, F32)],
        rope_tabs=rope_tabs)
    nh = S // CMP_STRIDE
    pe = jnp.stack([k_pe.reshape(-1), v_pe.reshape(-1)]).astype(F32)
    pe = jnp.broadcast_to(pe[:, None, :], (2, SUBLANES, pe.shape[-1]))
    w1 = jnp.stack([k_w1, v_w1]).astype(BF16)
    w2 = jnp.stack([k_w2, v_w2]).astype(BF16)
    kcmp, vcmp = _compress(kc, vc, pe, w1, w2, B, S)
    n_cmp = (S - CMP_BLOCK) // CMP_STRIDE + 1
    ncp = -(-nh // LANES) * LANES
    if ncp != nh:
        kcmp, vcmp = (jnp.pad(t, ((0, 0), (0, 0), (0, ncp - nh), (0, 0))) for t in (kcmp, vcmp))
    n_sel = S // SEL_BLOCK
    cs = np.arange(ncp) * CMP_STRIDE
    ss = np.arange(LANES) * SEL_BLOCK
    ov = ((cs[:, None] <= ss[None, :] + SEL_BLOCK - 1) & (cs[:, None] + CMP_BLOCK - 1 >= ss[None, :])
          & (np.arange(ncp)[:, None] < n_cmp) & (np.arange(LANES)[None, :] < n_sel))
    overlap = jnp.asarray(ov.astype(np.float32))
    oc, bias = _nsa_cmp(q, kcmp, vcmp, overlap, B, S)
    blk_of_pos = np.arange(S) // SEL_BLOCK
    onehot = jnp.asarray((blk_of_pos[:, None] == np.arange(LANES)[None, :]).astype(np.float32), dtype=BF16)
    nsa = _nsa_main(q, bias, ksp, onehot, vsa, kwp, vwa, oc, ng, B, S)
    nblk, bw, _ = wa.shape
    eye = jnp.eye(nblk, dtype=wa.dtype)
    dense = lambda wb: (eye[:, None, :, None] * wb[:, :, None, :]).reshape(nblk * bw, nblk * bw)
    wab = jnp.concatenate([dense(wa), dense(wx)], axis=1).astype(BF16)
    bab = jnp.concatenate([ba, bx]).reshape(1, 2 * LRU_W).astype(F32)
    lru = _lru(rgx.reshape(B, S, 2 * LRU_W), conv_w.astype(F32), conv_b.reshape(1, LRU_W).astype(F32), wab, bab,
               lam.reshape(1, LRU_W).astype(F32))
    wo = w_out.astype(BF16)
    return nsa, lru.reshape(T, LRU_W), wo[:QW], wo[QW:]


def kernel(x, p, positions, even_norm_mix, even_w_in, even_fox_bf, even_gdn_conv_w, even_gdn_a_log, even_gdn_dt_bias, even_gdn_norm_w, even_w_out, odd_norm_mix, odd_w_in, odd_cmp_k_pe, odd_cmp_k_w1, odd_cmp_k_w2, odd_cmp_v_pe, odd_cmp_v_w1, odd_cmp_v_w2, odd_rg_conv_w, odd_rg_conv_b, odd_rg_wa, odd_rg_ba, odd_rg_wx, odd_rg_bx, odd_rg_lambda, odd_w_out, mlp_norm, mlp_w_up, mlp_w_down, ple_norm, ple_w_gate, ple_w_proj, final_norm):
    B, S, D = x.shape
    T = B * S
    depth = p.shape[0]
    h = x.reshape(T, D)
    rope_tabs = _rope_tables(positions) if depth > 1 else None
    for i in range(depth):
        j = i // 2
        if i % 2 == 0:
            mix = _even_mixer(h, B, S, even_norm_mix[j], even_w_in[j], even_fox_bf[j], even_gdn_conv_w[j],
                              even_gdn_a_log[j], even_gdn_dt_bias[j], even_gdn_norm_w[j], even_w_out[j])
        else:
            mix = _odd_mixer(h, B, S, rope_tabs, odd_norm_mix[j], odd_w_in[j], odd_cmp_k_pe[j], odd_cmp_k_w1[j],
                             odd_cmp_k_w2[j], odd_cmp_v_pe[j], odd_cmp_v_w1[j], odd_cmp_v_w2[j], odd_rg_conv_w[j],
                             odd_rg_conv_b[j], odd_rg_wa[j], odd_rg_ba[j], odd_rg_wx[j], odd_rg_bx[j],
                             odd_rg_lambda[j], odd_w_out[j])
        h = _layer_tail(h, *mix, mlp_norm[i], mlp_w_up[i].astype(BF16), mlp_w_down[i].astype(BF16),
                        p.reshape(depth * T, -1), i, ple_norm[i], ple_w_gate[i].astype(BF16), ple_w_proj[i].astype(BF16),
                        final_norm, final=(i == depth - 1))
    return h.reshape(B, S, D)
```

```python
import functools

import numpy as np
import jax
import jax.numpy as jnp
from jax import lax
from jax.experimental import pallas as pl
from jax.experimental.pallas import tpu as pltpu

F32 = jnp.float32
BF16 = jnp.bfloat16
HIGHEST = lax.Precision.HIGHEST

NORM_EPS = 1e-6
LOG2E = float(np.log2(np.e))
NEG = -1e30
REMOVED = -3e38
LANES = 128
SUBLANES = 8
VMEM_LIMIT = 56 * 1024 * 1024

HEAD_DIM = 64
ROT_DIM = 16
ROPE_THETA = 500000.0
FOX_HEADS = 8
FOX_F_PIECES = 3
GDN_HEADS = 4
GDN_DIM = 128
GDN_CHUNK = 64
GDN_B_LANE = FOX_HEADS
GDN_A_LANE = FOX_HEADS + GDN_HEADS
NSA_HEADS = 8
NSA_GROUPS = 2
NSA_HPG = NSA_HEADS // NSA_GROUPS
CMP_BLOCK = 32
CMP_STRIDE = 16
CMP_HIDDEN = 128
SEL_BLOCK = 64
SEL_TOPK = 16
WINDOW = 512
LRU_W = 512
RG_C = 8.0
CONV_W = 4


def _cparams(*sem):
    return pltpu.CompilerParams(dimension_semantics=sem, vmem_limit_bytes=VMEM_LIMIT)


def _dot(a, b, **kw):
    return jnp.dot(a, b, preferred_element_type=F32, **kw)


def _dot_nt(a, b, **kw):
    return lax.dot_general(a, b, (((1,), (1,)), ((), ())), preferred_element_type=F32, **kw)


def _dot_tn(a, b, **kw):
    return lax.dot_general(a, b, (((0,), (0,)), ((), ())), preferred_element_type=F32, **kw)


def _softplus(x):
    return jnp.maximum(x, 0.0) + jnp.log1p(jnp.exp(-jnp.abs(x)))


def _sigmoid(x):
    return 1.0 / (1.0 + jnp.exp(-x))


def _silu(x):
    return x * _sigmoid(x)


def _gelu_tanh(x):
    return 0.5 * x * (1.0 + jnp.tanh(np.float32(np.sqrt(2.0 / np.pi)) * (x + 0.044715 * (x * x * x))))


def _rms(x, w):
    ms = jnp.mean(x * x, axis=-1, keepdims=True)
    return x * lax.rsqrt(ms + NORM_EPS) * w


def _apply_rope(y, c, s1, s2):
    outs = []
    for g in range(y.shape[1] // LANES):
        yg = y[:, g * LANES:(g + 1) * LANES]
        outs.append(yg * c + pltpu.roll(yg, LANES - ROT_DIM // 2, 1) * s1 + pltpu.roll(yg, ROT_DIM // 2, 1) * s2)
    return outs[0] if len(outs) == 1 else jnp.concatenate(outs, axis=1)


def _pad_heads(y, consts):
    tm = y.shape[0]
    lane = lax.broadcasted_iota(jnp.int32, (tm, LANES), 1)
    cst = jnp.zeros((tm, LANES), F32)
    for ln, val in consts:
        cst = jnp.where(lane == HEAD_DIM + ln, val, cst)
    low = lane < HEAD_DIM
    outs = []
    for g in range(y.shape[1] // LANES):
        yg = y[:, g * LANES:(g + 1) * LANES]
        outs.append(jnp.where(low, yg, cst))
        outs.append(jnp.where(low, pltpu.roll(yg, HEAD_DIM, 1), cst))
    return jnp.concatenate(outs, axis=1)


def _norm_proj_kernel(*refs, segs, n_out, rope):
    x_ref, nw_ref, w_ref = refs[:3]
    n_in = 3
    if rope:
        c_ref, s1_ref, s2_ref = refs[n_in:n_in + 3]
        n_in += 3
    out_refs = refs[n_in:n_in + n_out]
    xn = _rms(x_ref[...], nw_ref[...]).astype(BF16)
    for (oi, oc, wc, width, do_rope, scale, pad) in segs:
        y = _dot(xn, w_ref[:, wc:wc + width])
        if do_rope:
            y = _apply_rope(y, c_ref[...], s1_ref[...], s2_ref[...])
        if scale != 1.0:
            y = y * scale
        if pad is not None:
            y = _pad_heads(y, pad)
        out_refs[oi][:, oc:oc + y.shape[1]] = y.astype(out_refs[oi].dtype)


def _norm_proj(x, nw, w, segs, out_defs, rope_tabs=None, tm=512):
    T, D = x.shape
    N = w.shape[1]
    tm = min(tm, T)
    rope = rope_tabs is not None
    in_specs = [pl.BlockSpec((tm, D), lambda i: (i, 0)),
                pl.BlockSpec((1, D), lambda i: (0, 0)),
                pl.BlockSpec((D, N), lambda i: (0, 0))]
    args = [x, nw.reshape(1, D), w]
    if rope:
        in_specs += [pl.BlockSpec((tm, LANES), lambda i: (i, 0))] * 3
        args += list(rope_tabs)
    out_shape = [jax.ShapeDtypeStruct((T, wd), dt) for wd, dt in out_defs]
    out_specs = [pl.BlockSpec((tm, wd), lambda i: (i, 0)) for wd, _ in out_defs]
    return pl.pallas_call(
        functools.partial(_norm_proj_kernel, segs=tuple(segs), n_out=len(out_defs), rope=rope),
        grid=(T // tm,), in_specs=in_specs, out_specs=out_specs, out_shape=out_shape,
        compiler_params=_cparams("parallel"), name="norm_proj",
    )(*args)


def _rope_table_kernel(pos_ref, f_ref, e_ref, one_ref, c_ref, s1_ref, s2_ref):
    ang = pos_ref[0] * f_ref[...]
    cs = jnp.concatenate([jnp.cos(ang), jnp.sin(ang)], axis=0)
    hi = cs.astype(BF16)
    lo = (cs - hi.astype(F32)).astype(BF16)
    spread = lambda j: _dot_tn(hi, e_ref[j]) + _dot_tn(lo, e_ref[j])
    c_ref[...] = spread(0) + one_ref[...]
    s1_ref[...] = spread(1)
    s2_ref[...] = spread(2)


def _rope_tables(positions):
    B, S = positions.shape
    half = ROT_DIM // 2
    inv_freq = ROPE_THETA ** (-jnp.arange(half, dtype=F32) * (2.0 / ROT_DIM))
    pos = positions.astype(F32).reshape(B, 1, S)
    lane = np.arange(LANES)
    in_head, freq = lane % HEAD_DIM, lane % half
    e = np.zeros((3, 2 * half, LANES), np.float32)
    e[0, freq, lane] = in_head < ROT_DIM
    e[1, half + freq, lane] = -1.0 * (in_head < half)
    e[2, half + freq, lane] = 1.0 * ((in_head >= half) & (in_head < ROT_DIM))
    ones = (in_head >= ROT_DIM).astype(np.float32).reshape(1, LANES)
    sh = jax.ShapeDtypeStruct((B * S, LANES), F32)
    spec = pl.BlockSpec((S, LANES), lambda b: (b, 0))
    return pl.pallas_call(
        _rope_table_kernel, grid=(B,),
        in_specs=[pl.BlockSpec((1, 1, S), lambda b: (b, 0, 0)), pl.BlockSpec((half, 1), lambda b: (0, 0)),
                  pl.BlockSpec((3, 2 * half, LANES), lambda b: (0, 0, 0)), pl.BlockSpec((1, LANES), lambda b: (0, 0))],
        out_specs=[spec, spec, spec], out_shape=[sh, sh, sh],
        compiler_params=_cparams("parallel"), name="rope_tables",
    )(pos, inv_freq.reshape(half, 1), jnp.asarray(e, dtype=BF16), jnp.asarray(ones))


def _tail_kernel(h_ref, a1_ref, a2_ref, w1_ref, w2_ref, nwm_ref, wu_ref, wd_ref, p_ref, nwp_ref, wg_ref, wp_ref,
                 fw_ref, o_ref, *, final):
    hm = h_ref[...] + _dot(a1_ref[...], w1_ref[...]) + _dot(a2_ref[...], w2_ref[...])
    u = jnp.maximum(_dot(_rms(hm, nwm_ref[...]).astype(BF16), wu_ref[...]), 0.0)
    h2 = hm + _dot((u * u).astype(BF16), wd_ref[...])
    gate = _sigmoid(_dot(_rms(h2, nwp_ref[...]).astype(BF16), wg_ref[...]))
    y = h2 + gate * _dot(p_ref[...].astype(BF16), wp_ref[...])
    if final:
        y = _rms(y, fw_ref[...])
    o_ref[...] = y


def _layer_tail(h, a1, a2, w1, w2, nwm, wu, wd, p, layer, nwp, wg, wp, fw, final, tm=512):
    T, D = h.shape
    FF = wu.shape[1]
    P = p.shape[1]
    tm = min(tm, T)
    nt = T // tm
    K1, K2 = w1.shape[0], w2.shape[0]
    row = lambda w: pl.BlockSpec((tm, w), lambda i: (i, 0))
    full = lambda r, c: pl.BlockSpec((r, c), lambda i: (0, 0))
    return pl.pallas_call(
        functools.partial(_tail_kernel, final=final), grid=(nt,),
        in_specs=[row(D), row(K1), row(K2), full(K1, D), full(K2, D), full(1, D), full(D, FF), full(FF, D),
                  pl.BlockSpec((tm, P), lambda i: (layer * nt + i, 0)), full(1, D), full(D, D), full(P, D), full(1, D)],
        out_specs=row(D),
        out_shape=jax.ShapeDtypeStruct((T, D), F32),
        compiler_params=_cparams("parallel"), name="layer_tail",
    )(h, a1, a2, w1, w2, nwm.reshape(1, D), wu, wd, p, nwp.reshape(1, D), wg, wp, fw.reshape(1, D))


def _fox_gate_kernel(x_ref, b_ref, o_ref, carry_ref, *, ts):
    @pl.when(pl.program_id(1) == 0)
    def _():
        carry_ref[...] = jnp.zeros_like(carry_ref)

    r = lax.broadcasted_iota(jnp.int32, (LANES, LANES), 0)
    c = lax.broadcasted_iota(jnp.int32, (LANES, LANES), 1)
    lower = (r >= c).astype(F32)
    lane = lax.broadcasted_iota(jnp.int32, (LANES, LANES), 1)

    nblk = ts // LANES
    within = []
    for n in range(nblk):
        x = x_ref[n * LANES:(n + 1) * LANES, :] + b_ref[...]
        logf = jnp.minimum(x, 0.0) - jnp.log1p(jnp.exp(-jnp.abs(x)))
        within.append(_dot(lower, logf, precision=HIGHEST))
    carry = carry_ref[0:1, :]
    for n in range(nblk):
        cum = within[n] + carry
        carry = cum[LANES - 1:LANES, :]
        f = cum * LOG2E
        hi = f.astype(BF16).astype(F32)
        mid = (f - hi).astype(BF16).astype(F32)
        lo = f - hi - mid
        for h in range(FOX_HEADS):
            v = jnp.where(lane == HEAD_DIM, hi[:, h:h + 1],
                          jnp.where(lane == HEAD_DIM + 1, mid[:, h:h + 1],
                                    jnp.where(lane == HEAD_DIM + 2, lo[:, h:h + 1], 0.0)))
            o_ref[n * LANES:(n + 1) * LANES, h * LANES:(h + 1) * LANES] = v.astype(o_ref.dtype)
    carry_ref[0:1, :] = carry


def _fox_gate(sm, bias, B, S, ts=1024):
    T = B * S
    ts = min(ts, S)
    ns = S // ts
    b = jnp.zeros((1, LANES), F32).at[0, :FOX_HEADS].set(bias.astype(F32))
    return pl.pallas_call(
        functools.partial(_fox_gate_kernel, ts=ts), grid=(B, ns),
        in_specs=[pl.BlockSpec((ts, LANES), lambda bb, n: (bb * ns + n, 0)),
                  pl.BlockSpec((1, LANES), lambda bb, n: (0, 0))],
        out_specs=pl.BlockSpec((ts, FOX_HEADS * LANES), lambda bb, n: (bb * ns + n, 0)),
        out_shape=jax.ShapeDtypeStruct((T, FOX_HEADS * LANES), BF16),
        scratch_shapes=[pltpu.VMEM((SUBLANES, LANES), F32)],
        compiler_params=_cparams("parallel", "arbitrary"), name="fox_gate",
    )(sm, b)


def _fox_attn_kernel(q_ref, k_ref, v_ref, f_ref, o_ref, *, tq, tk, dk, hp):
    i = pl.program_id(2)
    qs = [q_ref[:, hh * LANES:(hh + 1) * LANES] for hh in range(hp)]
    nsub = tq // tk

    def scores(start, size, r0):
        rows = pl.ds(start, size)
        return [_dot_nt(qs[hh][r0:], k_ref[rows, hh * LANES:(hh + 1) * LANES] + f_ref[rows, hh * LANES:(hh + 1) * LANES])
                for hh in range(hp)]

    def finish(ss, start, size, r0, carry, masked):
        rows = pl.ds(start, size)
        out = []
        for hh in range(hp):
            m, acc = carry[hh]
            s = ss[hh]
            if masked:
                r = lax.broadcasted_iota(jnp.int32, (tq - r0, size), 0)
                c = lax.broadcasted_iota(jnp.int32, (tq - r0, size), 1)
                s = jnp.where(c <= r, s, NEG)
            m_new = jnp.maximum(m[r0:], jnp.max(s, axis=1, keepdims=True))
            pr = jnp.exp2(s - m_new).astype(BF16)
            acc_new = jnp.exp2(m[r0:] - m_new) * acc[r0:] + _dot(pr, v_ref[rows, hh * LANES:(hh + 1) * LANES])
            if r0:
                m_new = jnp.concatenate([m[:r0], m_new], axis=0)
                acc_new = jnp.concatenate([acc[:r0], acc_new], axis=0)
            out.append((m_new, acc_new))
        return tuple(out)

    def tile(start, size, r0, carry, masked):
        return finish(scores(start, size, r0), start, size, r0, carry, masked)

    init = tuple((jnp.full((tq, 1), NEG, F32), jnp.zeros((tq, LANES), F32)) for _ in range(hp))
    carry = lax.fori_loop(0, i * nsub, lambda j, c: tile(pl.multiple_of(j * tk, tk), tk, 0, c, False), init)
    dstarts = [pl.multiple_of(i * tq + d * dk, dk) for d in range(tq // dk)]
    dss = [scores(dstarts[d], dk, d * dk) for d in range(tq // dk)]
    for d in range(tq // dk):
        carry = finish(dss[d], dstarts[d], dk, d * dk, carry, True)
    o_ref[...] = jnp.concatenate([acc[:, :HEAD_DIM] / acc[:, HEAD_DIM:HEAD_DIM + 1] for (_, acc) in carry],
                                 axis=1).astype(o_ref.dtype)


def _fox_attn(q, k, v, f3, B, S, tq=1024, tk=1024, dk=512, hp=2):
    T = B * S
    tq = min(tq, S)
    tk = min(tk, tq)
    dk = min(dk, tq)
    nq = S // tq
    bw = hp * LANES
    ng = FOX_HEADS // hp
    res = pl.BlockSpec((S, bw), lambda b, p, i: (b, p))
    return pl.pallas_call(
        functools.partial(_fox_attn_kernel, tq=tq, tk=tk, dk=dk, hp=hp), grid=(B, ng, nq),
        in_specs=[pl.BlockSpec((tq, bw), lambda b, p, i: (b * nq + i, p)), res, res, res],
        out_specs=pl.BlockSpec((tq, hp * HEAD_DIM), lambda b, p, i: (b * nq + i, p)),
        out_shape=jax.ShapeDtypeStruct((T, FOX_HEADS * HEAD_DIM), BF16),
        compiler_params=_cparams("parallel", "parallel", "arbitrary"), name="fox_attn",
    )(q, k, v, f3)


def _cumsum_rows(x):
    n = x.shape[0]
    row = lax.broadcasted_iota(jnp.int32, x.shape, 0)
    d = 1
    while d < n:
        x = x + jnp.where(row >= d, pltpu.roll(x, d, 0), 0.0)
        d *= 2
    return x


def _gdn_kernel(x_ref, z_ref, sm_ref, cw_ref, par_ref, nw_ref, o_ref, state_ref, xp_ref, *, cps):
    C = GDN_CHUNK
    D = GDN_DIM
    W = GDN_HEADS * D
    R = cps * C
    n = pl.program_id(1)

    @pl.when(n == 0)
    def _():
        state_ref[...] = jnp.zeros_like(state_ref)
        xp_ref[0:SUBLANES, :] = jnp.zeros((SUBLANES, 3 * W), F32)

    x = x_ref[...]
    xp_ref[SUBLANES:SUBLANES + R, :] = x
    cw = cw_ref[...]
    y = x * cw[CONV_W - 1:CONV_W]
    for j in range(CONV_W - 1):
        off = SUBLANES - (CONV_W - 1) + j
        y = y + xp_ref[off:off + R, :] * cw[j:j + 1]
    xp_ref[0:SUBLANES, :] = x[R - SUBLANES:R]
    y = _silu(y)

    sm = sm_ref[...]
    beta_all = _sigmoid(sm)
    g_all = -jnp.exp(par_ref[0:1, :]) * _softplus(sm + par_ref[1:2, :])
    ri = lax.broadcasted_iota(jnp.int32, (C, C), 0)
    ci = lax.broadcasted_iota(jnp.int32, (C, C), 1)
    incl = ri >= ci
    strict = ri > ci

    chains = [(c, h) for c in range(cps) for h in range(GDN_HEADS)]
    gcs = [_cumsum_rows(g_all[c * C:(c + 1) * C]) for c in range(cps)]
    gcts = [gc.T for gc in gcs]
    pre = {}
    for (c, h) in chains:
        r0 = c * C
        q = y[r0:r0 + C, h * D:(h + 1) * D]
        k = y[r0:r0 + C, W + h * D:W + (h + 1) * D]
        v = y[r0:r0 + C, 2 * W + h * D:2 * W + (h + 1) * D]
        qn = q * lax.rsqrt(jnp.sum(q * q, axis=-1, keepdims=True) + 1e-6) * (D ** -0.5)
        kn = k * lax.rsqrt(jnp.sum(k * k, axis=-1, keepdims=True) + 1e-6)
        beta = beta_all[r0:r0 + C, GDN_B_LANE + h:GDN_B_LANE + h + 1]
        gcol = gcs[c][:, GDN_A_LANE + h:GDN_A_LANE + h + 1]
        grow = gcts[c][GDN_A_LANE + h:GDN_A_LANE + h + 1, :]
        decay = jnp.where(incl, jnp.exp(jnp.where(incl, gcol - grow, 0.0)), 0.0)
        kb = kn * beta
        g_last = gcol[C - 1:C, :]
        pre[c, h] = dict(qn16=qn.astype(BF16), kn16=kn.astype(BF16), kb16=kb.astype(BF16), decay=decay,
                         rhs=jnp.concatenate([v * beta, kb * jnp.exp(gcol)], axis=1),
                         qg16=(qn * jnp.exp(gcol)).astype(BF16),
                         kg16=(kn * jnp.exp(g_last - gcol)).astype(BF16), e_last=jnp.exp(g_last))
    kk = {ch: _dot_nt(pre[ch]["kb16"], pre[ch]["kn16"]) for ch in chains}
    qk = {ch: _dot_nt(pre[ch]["qn16"], pre[ch]["kn16"]) for ch in chains}
    P = {ch: -jnp.where(strict, kk[ch] * pre[ch]["decay"], 0.0) for ch in chains}
    X = dict(P)
    for _ in range(5):
        P = {ch: _dot(P[ch].astype(BF16), P[ch].astype(BF16)) for ch in chains}
        XP = {ch: _dot(X[ch].astype(BF16), P[ch].astype(BF16)) for ch in chains}
        X = {ch: X[ch] + P[ch] + XP[ch] for ch in chains}
    sol = {ch: pre[ch]["rhs"] + _dot(X[ch].astype(BF16), pre[ch]["rhs"].astype(BF16)) for ch in chains}
    attn16 = {ch: jnp.where(incl, qk[ch] * pre[ch]["decay"], 0.0).astype(BF16) for ch in chains}

    heads = range(GDN_HEADS)
    states = [state_ref[h] for h in heads]
    for c in range(cps):
        r0 = c * C
        st16 = [states[h].astype(BF16) for h in heads]
        ws = [_dot(sol[c, h][:, D:].astype(BF16), st16[h]) for h in heads]
        qs = [_dot(pre[c, h]["qg16"], st16[h]) for h in heads]
        v_new = [(sol[c, h][:, :D] - ws[h]).astype(BF16) for h in heads]
        av = [_dot(attn16[c, h], v_new[h]) for h in heads]
        kv = [_dot_tn(pre[c, h]["kg16"], v_new[h]) for h in heads]
        outs = []
        for h in heads:
            states[h] = states[h] * pre[c, h]["e_last"] + kv[h]
            o = _rms(qs[h] + av[h], nw_ref[...]) * _silu(z_ref[r0:r0 + C, h * D:(h + 1) * D])
            outs.append(o)
        o_ref[r0:r0 + C, :] = jnp.concatenate(outs, axis=1).astype(o_ref.dtype)
    for h in heads:
        state_ref[h] = states[h]


def _gdn(gqkv, gz, small, conv_w, a_log, dt_bias, norm_w, B, S, cps=8):
    T = B * S
    C = GDN_CHUNK
    R = cps * C
    N = S // R
    W = GDN_HEADS * GDN_DIM
    par = jnp.zeros((SUBLANES, LANES), F32)
    lanes = slice(GDN_A_LANE, GDN_A_LANE + GDN_HEADS)
    par = par.at[0, lanes].set(a_log.astype(F32)).at[1, lanes].set(dt_bias.astype(F32))
    return pl.pallas_call(
        functools.partial(_gdn_kernel, cps=cps), grid=(B, N),
        in_specs=[pl.BlockSpec((R, 3 * W), lambda b, n: (b * N + n, 0)),
                  pl.BlockSpec((R, W), lambda b, n: (b * N + n, 0)),
                  pl.BlockSpec((R, LANES), lambda b, n: (b * N + n, 0)),
                  pl.BlockSpec((CONV_W, 3 * W), lambda b, n: (0, 0)),
                  pl.BlockSpec((SUBLANES, LANES), lambda b, n: (0, 0)),
                  pl.BlockSpec((1, GDN_DIM), lambda b, n: (0, 0))],
        out_specs=pl.BlockSpec((R, W), lambda b, n: (b * N + n, 0)),
        out_shape=jax.ShapeDtypeStruct((T, W), BF16),
        scratch_shapes=[pltpu.VMEM((GDN_HEADS, GDN_DIM, GDN_DIM), F32),
                        pltpu.VMEM((SUBLANES + R, 3 * W), F32)],
        compiler_params=_cparams("parallel", "arbitrary"), name="gdn",
    )(gqkv, gz, small, conv_w.astype(F32), par, norm_w.reshape(1, GDN_DIM).astype(F32))


def _compress_kernel(x_ref, pe_ref, w1_ref, wbd_ref, w2_ref, o_ref, *, nh):
    ab = jnp.zeros((nh, NSA_GROUPS * 2 * CMP_HIDDEN), F32)
    for l in range(CMP_STRIDE):
        ab = ab + _dot(x_ref[pl.ds(l, nh, stride=CMP_STRIDE), :].astype(BF16), wbd_ref[0, l])
    lane = lax.broadcasted_iota(jnp.int32, (nh, HEAD_DIM), 1)
    one_col = jnp.where(lane == 0, 1.0, 0.0)
    c = _dot(pe_ref[0].astype(BF16), w1_ref[0])
    for g in range(NSA_GROUPS):
        a = ab[:, 2 * g * CMP_HIDDEN:(2 * g + 1) * CMP_HIDDEN]
        b = ab[:, (2 * g + 1) * CMP_HIDDEN:(2 * g + 2) * CMP_HIDDEN]
        hid = _gelu_tanh(a + pltpu.roll(b, nh - 1, 0) + c[0:1])
        out = _dot(hid.astype(BF16), w2_ref[0])
        o_ref[0, g] = jnp.concatenate([out, one_col], axis=1).astype(o_ref.dtype)


def _compress(kc, vc, pe, w1, w2, B, S):
    nh = S // CMP_STRIDE
    G = NSA_GROUPS
    half = CMP_STRIDE * HEAD_DIM
    w1r = w1.reshape(2, 2, CMP_STRIDE, HEAD_DIM, CMP_HIDDEN)
    per_tok = jnp.concatenate([w1r[:, 0], w1r[:, 1]], axis=-1)
    eye = jnp.eye(G, dtype=w1.dtype)
    wbd = per_tok[:, :, None, :, None, :] * eye[None, None, :, None, :, None]
    wbd = wbd.reshape(2, CMP_STRIDE, G * HEAD_DIM, G * 2 * CMP_HIDDEN)
    x_spec = pl.BlockSpec((S, G * HEAD_DIM), lambda b, j: (b, 0))
    out = [pl.pallas_call(
        functools.partial(_compress_kernel, nh=nh), grid=(B, 1),
        in_specs=[x_spec, pl.BlockSpec((1, SUBLANES, 2 * half), lambda b, j, kv=kv: (kv, 0, 0)),
                  pl.BlockSpec((1, 2 * half, CMP_HIDDEN), lambda b, j, kv=kv: (kv, 0, 0)),
                  pl.BlockSpec((1, CMP_STRIDE, G * HEAD_DIM, G * 2 * CMP_HIDDEN), lambda b, j, kv=kv: (kv, 0, 0, 0)),
                  pl.BlockSpec((1, CMP_HIDDEN, HEAD_DIM), lambda b, j, kv=kv: (kv, 0, 0))],
        out_specs=pl.BlockSpec((1, G, nh, LANES), lambda b, j: (b, 0, 0, 0)),
        out_shape=jax.ShapeDtypeStruct((B, G, nh, LANES), BF16),
        compiler_params=_cparams("parallel", "arbitrary"), name="nsa_compress",
    )(x, pe, w1, wbd, w2) for kv, x in enumerate((kc, vc))]
    return out


def _stack_heads(qb):
    return jnp.concatenate([qb[:, h * HEAD_DIM:(h + 1) * HEAD_DIM] for h in range(NSA_HPG)], axis=0)


def _unstack_heads(o, tq):
    return jnp.concatenate([o[h * tq:(h + 1) * tq] for h in range(NSA_HPG)], axis=1)


def _nsa_cmp_kernel(q_ref, kc_ref, vc_ref, ov_ref, oc_ref, bias_ref, *, tq, n_sel, top_k):
    i = pl.program_id(1)
    G = NSA_GROUPS
    GW = NSA_HPG * HEAD_DIM
    ncp = kc_ref.shape[2]
    row = lax.broadcasted_iota(jnp.int32, (NSA_HPG * tq, 1), 0)
    t4 = i * tq + (row & (tq - 1))
    cmp_end = lax.broadcasted_iota(jnp.int32, (1, ncp), 1) * CMP_STRIDE + (CMP_BLOCK - 1)
    valid = cmp_end <= t4
    ss = [_dot_nt(_stack_heads(q_ref[:, g * GW:(g + 1) * GW]), kc_ref[0, g, :, :HEAD_DIM]) for g in range(G)]
    ps, accs = [], []
    for g in range(G):
        s = jnp.where(valid, ss[g], NEG)
        m = jnp.max(s, axis=1, keepdims=True)
        p = jnp.exp2(s - m)
        ps.append(p)
        accs.append(_dot(p.astype(BF16), vc_ref[0, g]))
    imps = []
    for g in range(G):
        inv = jnp.where(t4 >= CMP_BLOCK - 1, 1.0 / accs[g][:, HEAD_DIM:HEAD_DIM + 1], 0.0)
        oc_ref[:, g * GW:(g + 1) * GW] = _unstack_heads(accs[g][:, :HEAD_DIM] * inv, tq)
        p = ps[g] * inv
        psum = p[0:tq] + p[tq:2 * tq] + p[2 * tq:3 * tq] + p[3 * tq:4 * tq]
        imps.append(_dot(psum, ov_ref[...], precision=HIGHEST))
    blk = lax.broadcasted_iota(jnp.int32, (1, LANES), 1)
    t = i * tq + lax.broadcasted_iota(jnp.int32, (tq, 1), 0)
    cur = t >> (SEL_BLOCK.bit_length() - 1)
    forced = (blk == 0) | (blk == cur) | (blk == cur - 1)
    future = blk * SEL_BLOCK > t
    blk_t = lax.broadcasted_iota(jnp.int32, (LANES, tq), 0).astype(F32)
    for g in range(G):
        imp = jnp.where(future, NEG, jnp.where(forced, REMOVED, imps[g]))
        imp = jnp.where(blk < n_sel, imp, REMOVED)
        imp_t = imp.T
        for _ in range(max(top_k - 3, 0)):
            mx = jnp.max(imp_t, axis=0, keepdims=True)
            first = jnp.min(jnp.where(imp_t == mx, blk_t, float(LANES)), axis=0, keepdims=True)
            imp_t = jnp.where(blk_t == first, REMOVED, imp_t)
        sel = jnp.where(imp_t == REMOVED, 1.0, 0.0).T
        bias_ref[0, g] = jnp.where((sel > 0.0) & jnp.logical_not(future), 0.0, NEG).astype(bias_ref.dtype)


def _nsa_cmp(q, kc, vc, overlap, B, S, tq=512):
    T = B * S
    tq = min(tq, S)
    nq = S // tq
    ncp = kc.shape[2]
    n_sel = S // SEL_BLOCK
    QW = NSA_HEADS * HEAD_DIM
    G = NSA_GROUPS
    return pl.pallas_call(
        functools.partial(_nsa_cmp_kernel, tq=tq, n_sel=n_sel, top_k=min(SEL_TOPK, n_sel)),
        grid=(B, nq),
        in_specs=[pl.BlockSpec((tq, QW), lambda b, i: (b * nq + i, 0)),
                  pl.BlockSpec((1, G, ncp, LANES), lambda b, i: (b, 0, 0, 0)),
                  pl.BlockSpec((1, G, ncp, LANES), lambda b, i: (b, 0, 0, 0)),
                  pl.BlockSpec((ncp, LANES), lambda b, i: (0, 0))],
        out_specs=[pl.BlockSpec((tq, QW), lambda b, i: (b * nq + i, 0)),
                   pl.BlockSpec((1, G, tq, LANES), lambda b, i: (b, 0, i, 0))],
        out_shape=[jax.ShapeDtypeStruct((T, QW), F32),
                   jax.ShapeDtypeStruct((B, G, S, LANES), BF16)],
        compiler_params=_cparams("parallel", "parallel"), name="nsa_cmp",
    )(q, kc, vc, overlap)


def _nsa_main_kernel(q_ref, bias_ref, ks_ref, oh_ref, vs_ref, kw_ref, vw_ref, oc_ref, g_ref, o_ref, *, qn, tk, wspan):
    R = NSA_HPG * qn
    tb = 2 * tk
    i = pl.program_id(2)
    q4 = _stack_heads(q_ref[...])
    q4p = jnp.concatenate([q4, jnp.zeros_like(q4)], axis=1)
    b4 = jnp.concatenate([bias_ref[0, 0]] * NSA_HPG, axis=0)
    qa = jnp.concatenate([q4p, b4], axis=1)
    t4 = i * qn + (lax.broadcasted_iota(jnp.int32, (R, 1), 0) & (qn - 1))

    def scores(start, size):
        rows = pl.ds(start, size)
        ka = jnp.concatenate([ks_ref[rows, :], oh_ref[rows, :]], axis=1)
        return _dot_nt(qa, ka)

    def finish(s, start, size, carry, masked):
        m, acc = carry
        if masked:
            kpos = start + lax.broadcasted_iota(jnp.int32, (1, size), 1)
            s = jnp.where(kpos <= t4, s, NEG)
        m_new = jnp.maximum(m, jnp.max(s, axis=1, keepdims=True))
        pr = jnp.exp2(s - m_new).astype(BF16)
        acc = jnp.exp2(m - m_new) * acc + _dot(pr, vs_ref[pl.ds(start, size), :])
        return m_new, acc

    def step(start, size, carry, masked):
        return finish(scores(start, size), start, size, carry, masked)

    first = i * qn
    nbig = first // tb
    diag = (first // tk) * tk
    nsmall = (diag - nbig * tb) // tk
    init = (jnp.full((R, 1), NEG, F32), jnp.zeros((R, LANES), F32))
    carry = lax.fori_loop(0, nbig, lambda j, c: step(pl.multiple_of(j * tb, tb), tb, c, False), init)
    carry = lax.fori_loop(0, nsmall, lambda j, c: step(pl.multiple_of(nbig * tb, tk), tk, c, False), carry)
    dstart = pl.multiple_of(diag, tk)
    s_d = scores(dstart, tk)
    wrows = pl.ds(pl.multiple_of(jnp.maximum(first + qn - wspan, 0), qn), wspan)
    s_w = _dot_nt(q4p, kw_ref[wrows, :])
    _, acc_s = finish(s_d, dstart, tk, carry, True)
    kpos = jnp.maximum(first + qn - wspan, 0) + lax.broadcasted_iota(jnp.int32, (1, wspan), 1)
    wmask = lax.bitcast_convert_type(t4 - kpos, jnp.uint32) < jnp.uint32(WINDOW)
    s_w = jnp.where(wmask, s_w, NEG)
    m_w = jnp.max(s_w, axis=1, keepdims=True)
    acc_w = _dot(jnp.exp2(s_w - m_w).astype(BF16), vw_ref[wrows, :])

    gates = _sigmoid(g_ref[...])
    first_group = pl.program_id(1) == 0
    ng = NSA_HPG * 3

    def gate(h, branch):
        j = 3 * h + branch
        return jnp.where(first_group, gates[:, j:j + 1], gates[:, ng + j:ng + j + 1])

    oc = oc_ref[...]
    outs = []
    for h in range(NSA_HPG):
        hs = slice(h * qn, (h + 1) * qn)
        c_s = gate(h, 1) / acc_s[hs, HEAD_DIM:HEAD_DIM + 1]
        c_w = gate(h, 2) / acc_w[hs, HEAD_DIM:HEAD_DIM + 1]
        outs.append(gate(h, 0) * oc[:, h * HEAD_DIM:(h + 1) * HEAD_DIM]
                    + c_s * acc_s[hs, :HEAD_DIM] + c_w * acc_w[hs, :HEAD_DIM])
    o_ref[...] = jnp.concatenate(outs, axis=1).astype(o_ref.dtype)


def _nsa_main(q, bias, ksp, onehot, vsa, kwp, vwa, oc, gates, B, S, qn=256, tk=512):
    T = B * S
    nq = S // qn
    tk = min(tk, S // 2)
    wspan = min(WINDOW + qn, S)
    GW = NSA_HPG * HEAD_DIM
    res = pl.BlockSpec((S, LANES), lambda b, g, i: (b, g))
    return pl.pallas_call(
        functools.partial(_nsa_main_kernel, qn=qn, tk=tk, wspan=wspan), grid=(B, NSA_GROUPS, nq),
        in_specs=[pl.BlockSpec((qn, GW), lambda b, g, i: (b * nq + i, g)),
                  pl.BlockSpec((1, 1, qn, LANES), lambda b, g, i: (b, g, i, 0)),
                  res, pl.BlockSpec((S, LANES), lambda b, g, i: (0, 0)), res, res, res,
                  pl.BlockSpec((qn, GW), lambda b, g, i: (b * nq + i, g)),
                  pl.BlockSpec((qn, LANES), lambda b, g, i: (b * nq + i, 0))],
        out_specs=pl.BlockSpec((qn, GW), lambda b, g, i: (b * nq + i, g)),
        out_shape=jax.ShapeDtypeStruct((T, NSA_HEADS * HEAD_DIM), BF16),
        compiler_params=_cparams("parallel", "parallel", "arbitrary"), name="nsa_main",
    )(q, bias, ksp, onehot, vsa, kwp, vwa, oc, gates)


def _lru_kernel(gx_ref, cw_ref, cb_ref, wab_ref, bab_ref, lam_ref, o_ref, h_ref, tail_ref, a_ref, b_ref, *, ts):
    n = pl.program_id(0)
    Bb = gx_ref.shape[0]
    W = gx_ref.shape[2] // 2

    @pl.when(n == 0)
    def _():
        h_ref[...] = jnp.zeros_like(h_ref)
        tail_ref[...] = jnp.zeros_like(tail_ref)

    xin = jnp.swapaxes(gx_ref[:, :, W:], 0, 1)
    xp = jnp.concatenate([tail_ref[...], xin], axis=0)
    cw = cw_ref[...]
    x = cb_ref[...].reshape(1, 1, W)
    for j in range(CONV_W):
        x = x + xp[j:j + ts] * cw[j:j + 1].reshape(1, 1, W)
    tail_ref[...] = xin[ts - (CONV_W - 1):ts]
    x2 = x.reshape(ts * Bb, W)
    pre = _dot(x2.astype(BF16), wab_ref[...]) + bab_ref[...]
    r = _sigmoid(pre[:, :W])
    ig = _sigmoid(pre[:, W:])
    log_a = (-RG_C * _softplus(-lam_ref[...])) * r
    a = jnp.exp(log_a)
    th = jnp.tanh(log_a)
    bb = jnp.sqrt(-2.0 * th / (1.0 - th)) * (ig * x2)
    a_ref[...] = a.reshape(ts, Bb, W)
    b_ref[...] = bb.reshape(ts, Bb, W)

    def scan(t, h):
        h = a_ref[t] * h + b_ref[t]
        b_ref[t] = h
        return h

    h_ref[...] = lax.fori_loop(0, ts, scan, h_ref[...], unroll=8)
    o_ref[...] = (jnp.swapaxes(b_ref[...], 0, 1) * _gelu_tanh(gx_ref[:, :, :W])).astype(o_ref.dtype)


def _lru(gx, conv_w, conv_b, wab, bab, lam, ts=128):
    Bb, S, W2 = gx.shape
    W = W2 // 2
    ts = min(ts, S)
    full = lambda shape: pl.BlockSpec(shape, lambda n: (0,) * len(shape))
    return pl.pallas_call(
        functools.partial(_lru_kernel, ts=ts), grid=(S // ts,),
        in_specs=[pl.BlockSpec((Bb, ts, W2), lambda n: (0, n, 0)),
                  full((CONV_W, W)), full((1, W)), full((W, 2 * W)), full((1, 2 * W)), full((1, W))],
        out_specs=pl.BlockSpec((Bb, ts, W), lambda n: (0, n, 0)),
        out_shape=jax.ShapeDtypeStruct((Bb, S, W), BF16),
        scratch_shapes=[pltpu.VMEM((Bb, W), F32), pltpu.VMEM((CONV_W - 1, Bb, W), F32),
                        pltpu.VMEM((ts, Bb, W), F32), pltpu.VMEM((ts, Bb, W), F32)],
        compiler_params=_cparams("arbitrary"), name="rg_lru",
    )(gx, conv_w, conv_b, wab, bab, lam)


def _even_mixer(h, B, S, norm_w, w_in, fox_bf, conv_w, a_log, dt_bias, gdn_norm_w, w_out):
    T, D = h.shape
    FW = FOX_HEADS * HEAD_DIM
    GW = GDN_HEADS * GDN_DIM
    o_ff = 3 * FW
    o_g = o_ff + FOX_HEADS
    o_gb = o_g + 4 * GW
    small = jnp.concatenate([w_in[:, o_ff:o_g], w_in[:, o_gb:o_gb + 2 * GDN_HEADS],
                             jnp.zeros((D, LANES - FOX_HEADS - 2 * GDN_HEADS), w_in.dtype)], axis=1)
    w = jnp.concatenate([w_in[:, :o_ff], w_in[:, o_g:o_gb], small], axis=1).astype(BF16)
    segs = [(0, 0, 0, FW, False, HEAD_DIM ** -0.5 * LOG2E, tuple((j, -1.0) for j in range(FOX_F_PIECES))),
            (1, 0, FW, FW, False, 1.0, ()),
            (2, 0, 2 * FW, FW, False, 1.0, ((0, 1.0),)),
            (3, 0, 3 * FW, 3 * GW, False, 1.0, None),
            (4, 0, 3 * FW + 3 * GW, GW, False, 1.0, None),
            (5, 0, 3 * FW + 4 * GW, LANES, False, 1.0, None)]
    fq, fk, fv, gqkv, gz, sm = _norm_proj(h, norm_w, w, segs,
                                          [(2 * FW, BF16), (2 * FW, BF16), (2 * FW, BF16),
                                           (3 * GW, F32), (GW, F32), (LANES, F32)])
    f3 = _fox_gate(sm, fox_bf, B, S)
    fox = _fox_attn(fq, fk, fv, f3, B, S)
    gdn = _gdn(gqkv, gz, sm, conv_w, a_log, dt_bias, gdn_norm_w, B, S)
    wo = w_out.astype(BF16)
    return fox, gdn, wo[:FW], wo[FW:]


def _odd_mixer(h, B, S, rope_tabs, norm_w, w_in, k_pe, k_w1, k_w2, v_pe, v_w1, v_w2,
               conv_w, conv_b, wa, ba, wx, bx, lam, w_out):
    T, D = h.shape
    QW = NSA_HEADS * HEAD_DIM
    KW = NSA_GROUPS * HEAD_DIM
    G = NSA_GROUPS
    o_ng = QW + 6 * KW
    ngw = NSA_HEADS * 3
    w = jnp.concatenate([w_in[:, :o_ng], w_in[:, o_ng:o_ng + ngw], jnp.zeros((D, LANES - ngw), w_in.dtype),
                         w_in[:, o_ng + ngw:]], axis=1).astype(BF16)
    ones = ((0, 1.0),)
    segs = [(0, 0, 0, QW, True, HEAD_DIM ** -0.5 * LOG2E, None),
            (1, 0, QW, KW, True, 1.0, None),
            (2, 0, QW + KW, KW, False, 1.0, None),
            (3, 0, QW + 2 * KW, KW, True, 1.0, ()),
            (4, 0, QW + 3 * KW, KW, False, 1.0, ones),
            (5, 0, QW + 4 * KW, KW, True, 1.0, ()),
            (6, 0, QW + 5 * KW, KW, False, 1.0, ones),
            (7, 0, o_ng, LANES, False, 1.0, None),
            (8, 0, o_ng + LANES, 2 * LRU_W, False, 1.0, None)]
    q, kc, vc, ksp, vsa, kwp, vwa, ng, rgx = _norm_proj(
        h, norm_w, w, segs,
        [(QW, BF16), (KW, F32), (KW, F32), (2 * KW, BF16), (2 * KW, BF16), (2 * KW, BF16), (2 * KW, BF16),
         (LANES, F32), (2 * LRU_W, F32)],
        rope_tabs=rope_tabs)
    nh = S // CMP_STRIDE
    pe = jnp.stack([k_pe.reshape(-1), v_pe.reshape(-1)]).astype(F32)
    pe = jnp.broadcast_to(pe[:, None, :], (2, SUBLANES, pe.shape[-1]))
    w1 = jnp.stack([k_w1, v_w1]).astype(BF16)
    w2 = jnp.stack([k_w2, v_w2]).astype(BF16)
    kcmp, vcmp = _compress(kc, vc, pe, w1, w2, B, S)
    n_cmp = (S - CMP_BLOCK) // CMP_STRIDE + 1
    ncp = -(-nh // LANES) * LANES
    if ncp != nh:
        kcmp, vcmp = (jnp.pad(t, ((0, 0), (0, 0), (0, ncp - nh), (0, 0))) for t in (kcmp, vcmp))
    n_sel = S // SEL_BLOCK
    cs = np.arange(ncp) * CMP_STRIDE
    ss = np.arange(LANES) * SEL_BLOCK
    ov = ((cs[:, None] <= ss[None, :] + SEL_BLOCK - 1) & (cs[:, None] + CMP_BLOCK - 1 >= ss[None, :])
          & (np.arange(ncp)[:, None] < n_cmp) & (np.arange(LANES)[None, :] < n_sel))
    overlap = jnp.asarray(ov.astype(np.float32))
    oc, bias = _nsa_cmp(q, kcmp, vcmp, overlap, B, S)
    blk_of_pos = np.arange(S) // SEL_BLOCK
    onehot = jnp.asarray((blk_of_pos[:, None] == np.arange(LANES)[None, :]).astype(np.float32), dtype=BF16)
    nsa = _nsa_main(q, bias, ksp, onehot, vsa, kwp, vwa, oc, ng, B, S)
    nblk, bw, _ = wa.shape
    eye = jnp.eye(nblk, dtype=wa.dtype)
    dense = lambda wb: (eye[:, None, :, None] * wb[:, :, None, :]).reshape(nblk * bw, nblk * bw)
    wab = jnp.concatenate([dense(wa), dense(wx)], axis=1).astype(BF16)
    bab = jnp.concatenate([ba, bx]).reshape(1, 2 * LRU_W).astype(F32)
    lru = _lru(rgx.reshape(B, S, 2 * LRU_W), conv_w.astype(F32), conv_b.reshape(1, LRU_W).astype(F32), wab, bab,
               lam.reshape(1, LRU_W).astype(F32))
    wo = w_out.astype(BF16)
    return nsa, lru.reshape(T, LRU_W), wo[:QW], wo[QW:]


def kernel(x, p, positions, even_norm_mix, even_w_in, even_fox_bf, even_gdn_conv_w, even_gdn_a_log, even_gdn_dt_bias, even_gdn_norm_w, even_w_out, odd_norm_mix, odd_w_in, odd_cmp_k_pe, odd_cmp_k_w1, odd_cmp_k_w2, odd_cmp_v_pe, odd_cmp_v_w1, odd_cmp_v_w2, odd_rg_conv_w, odd_rg_conv_b, odd_rg_wa, odd_rg_ba, odd_rg_wx, odd_rg_bx, odd_rg_lambda, odd_w_out, mlp_norm, mlp_w_up, mlp_w_down, ple_norm, ple_w_gate, ple_w_proj, final_norm):
    B, S, D = x.shape
    T = B * S
    depth = p.shape[0]
    h = x.reshape(T, D)
    rope_tabs = _rope_tables(positions) if depth > 1 else None
    for i in range(depth):
        j = i // 2
        if i % 2 == 0:
            mix = _even_mixer(h, B, S, even_norm_mix[j], even_w_in[j], even_fox_bf[j], even_gdn_conv_w[j],
                              even_gdn_a_log[j], even_gdn_dt_bias[j], even_gdn_norm_w[j], even_w_out[j])
        else:
            mix = _odd_mixer(h, B, S, rope_tabs, odd_norm_mix[j], odd_w_in[j], odd_cmp_k_pe[j], odd_cmp_k_w1[j],
                             odd_cmp_k_w2[j], odd_cmp_v_pe[j], odd_cmp_v_w1[j], odd_cmp_v_w2[j], odd_rg_conv_w[j],
                             odd_rg_conv_b[j], odd_rg_wa[j], odd_rg_ba[j], odd_rg_wx[j], odd_rg_bx[j],
                             odd_rg_lambda[j], odd_w_out[j])
        h = _layer_tail(h, *mix, mlp_norm[i], mlp_w_up[i].astype(BF16), mlp_w_down[i].astype(BF16),
                        p.reshape(depth * T, -1), i, ple_norm[i], ple_w_gate[i].astype(BF16), ple_w_proj[i].astype(BF16),
                        final_norm, final=(i == depth - 1))
    return h.reshape(B, S, D)
```

```python
import functools

import numpy as np
import jax
import jax.numpy as jnp
from jax import lax
from jax.experimental import pallas as pl
from jax.experimental.pallas import tpu as pltpu

F32 = jnp.float32
BF16 = jnp.bfloat16
HIGHEST = lax.Precision.HIGHEST

NORM_EPS = 1e-6
LOG2E = float(np.log2(np.e))
NEG = -1e30
REMOVED = -3e38
LANES = 128
SUBLANES = 8
VMEM_LIMIT = 56 * 1024 * 1024

HEAD_DIM = 64
ROT_DIM = 16
ROPE_THETA = 500000.0
FOX_HEADS = 8
FOX_F_PIECES = 3
GDN_HEADS = 4
GDN_DIM = 128
GDN_CHUNK = 64
GDN_B_LANE = FOX_HEADS
GDN_A_LANE = FOX_HEADS + GDN_HEADS
NSA_HEADS = 8
NSA_GROUPS = 2
NSA_HPG = NSA_HEADS // NSA_GROUPS
CMP_BLOCK = 32
CMP_STRIDE = 16
CMP_HIDDEN = 128
SEL_BLOCK = 64
SEL_TOPK = 16
WINDOW = 512
LRU_W = 512
RG_C = 8.0
CONV_W = 4


def _cparams(*sem):
    return pltpu.CompilerParams(dimension_semantics=sem, vmem_limit_bytes=VMEM_LIMIT)


def _dot(a, b, **kw):
    return jnp.dot(a, b, preferred_element_type=F32, **kw)


def _dot_nt(a, b, **kw):
    return lax.dot_general(a, b, (((1,), (1,)), ((), ())), preferred_element_type=F32, **kw)


def _dot_tn(a, b, **kw):
    return lax.dot_general(a, b, (((0,), (0,)), ((), ())), preferred_element_type=F32, **kw)


def _softplus(x):
    return jnp.maximum(x, 0.0) + jnp.log1p(jnp.exp(-jnp.abs(x)))


def _sigmoid(x):
    return 1.0 / (1.0 + jnp.exp(-x))


def _silu(x):
    return x * _sigmoid(x)


def _gelu_tanh(x):
    return 0.5 * x * (1.0 + jnp.tanh(np.float32(np.sqrt(2.0 / np.pi)) * (x + 0.044715 * (x * x * x))))


def _rms(x, w):
    ms = jnp.mean(x * x, axis=-1, keepdims=True)
    return x * lax.rsqrt(ms + NORM_EPS) * w


def _apply_rope(y, c, s1, s2):
    outs = []
    for g in range(y.shape[1] // LANES):
        yg = y[:, g * LANES:(g + 1) * LANES]
        outs.append(yg * c + pltpu.roll(yg, LANES - ROT_DIM // 2, 1) * s1 + pltpu.roll(yg, ROT_DIM // 2, 1) * s2)
    return outs[0] if len(outs) == 1 else jnp.concatenate(outs, axis=1)


def _pad_heads(y, consts):
    tm = y.shape[0]
    lane = lax.broadcasted_iota(jnp.int32, (tm, LANES), 1)
    cst = jnp.zeros((tm, LANES), F32)
    for ln, val in consts:
        cst = jnp.where(lane == HEAD_DIM + ln, val, cst)
    low = lane < HEAD_DIM
    outs = []
    for g in range(y.shape[1] // LANES):
        yg = y[:, g * LANES:(g + 1) * LANES]
        outs.append(jnp.where(low, yg, cst))
        outs.append(jnp.where(low, pltpu.roll(yg, HEAD_DIM, 1), cst))
    return jnp.concatenate(outs, axis=1)


def _norm_proj_kernel(*refs, segs, n_out, rope):
    x_ref, nw_ref, w_ref = refs[:3]
    n_in = 3
    if rope:
        c_ref, s1_ref, s2_ref = refs[n_in:n_in + 3]
        n_in += 3
    out_refs = refs[n_in:n_in + n_out]
    xn = _rms(x_ref[...], nw_ref[...]).astype(BF16)
    for (oi, oc, wc, width, do_rope, scale, pad) in segs:
        y = _dot(xn, w_ref[:, wc:wc + width])
        if do_rope:
            y = _apply_rope(y, c_ref[...], s1_ref[...], s2_ref[...])
        if scale != 1.0:
            y = y * scale
        if pad is not None:
            y = _pad_heads(y, pad)
        out_refs[oi][:, oc:oc + y.shape[1]] = y.astype(out_refs[oi].dtype)


def _norm_proj(x, nw, w, segs, out_defs, rope_tabs=None, tm=512):
    T, D = x.shape
    N = w.shape[1]
    tm = min(tm, T)
    rope = rope_tabs is not None
    in_specs = [pl.BlockSpec((tm, D), lambda i: (i, 0)),
                pl.BlockSpec((1, D), lambda i: (0, 0)),
                pl.BlockSpec((D, N), lambda i: (0, 0))]
    args = [x, nw.reshape(1, D), w]
    if rope:
        in_specs += [pl.BlockSpec((tm, LANES), lambda i: (i, 0))] * 3
        args += list(rope_tabs)
    out_shape = [jax.ShapeDtypeStruct((T, wd), dt) for wd, dt in out_defs]
    out_specs = [pl.BlockSpec((tm, wd), lambda i: (i, 0)) for wd, _ in out_defs]
    return pl.pallas_call(
        functools.partial(_norm_proj_kernel, segs=tuple(segs), n_out=len(out_defs), rope=rope),
        grid=(T // tm,), in_specs=in_specs, out_specs=out_specs, out_shape=out_shape,
        compiler_params=_cparams("parallel"), name="norm_proj",
    )(*args)


def _rope_table_kernel(pos_ref, f_ref, e_ref, one_ref, c_ref, s1_ref, s2_ref):
    ang = pos_ref[0] * f_ref[...]
    cs = jnp.concatenate([jnp.cos(ang), jnp.sin(ang)], axis=0)
    hi = cs.astype(BF16)
    lo = (cs - hi.astype(F32)).astype(BF16)
    spread = lambda j: _dot_tn(hi, e_ref[j]) + _dot_tn(lo, e_ref[j])
    c_ref[...] = spread(0) + one_ref[...]
    s1_ref[...] = spread(1)
    s2_ref[...] = spread(2)


def _rope_tables(positions):
    B, S = positions.shape
    half = ROT_DIM // 2
    inv_freq = ROPE_THETA ** (-jnp.arange(half, dtype=F32) * (2.0 / ROT_DIM))
    pos = positions.astype(F32).reshape(B, 1, S)
    lane = np.arange(LANES)
    in_head, freq = lane % HEAD_DIM, lane % half
    e = np.zeros((3, 2 * half, LANES), np.float32)
    e[0, freq, lane] = in_head < ROT_DIM
    e[1, half + freq, lane] = -1.0 * (in_head < half)
    e[2, half + freq, lane] = 1.0 * ((in_head >= half) & (in_head < ROT_DIM))
    ones = (in_head >= ROT_DIM).astype(np.float32).reshape(1, LANES)
    sh = jax.ShapeDtypeStruct((B * S, LANES), F32)
    spec = pl.BlockSpec((S, LANES), lambda b: (b, 0))
    return pl.pallas_call(
        _rope_table_kernel, grid=(B,),
        in_specs=[pl.BlockSpec((1, 1, S), lambda b: (b, 0, 0)), pl.BlockSpec((half, 1), lambda b: (0, 0)),
                  pl.BlockSpec((3, 2 * half, LANES), lambda b: (0, 0, 0)), pl.BlockSpec((1, LANES), lambda b: (0, 0))],
        out_specs=[spec, spec, spec], out_shape=[sh, sh, sh],
        compiler_params=_cparams("parallel"), name="rope_tables",
    )(pos, inv_freq.reshape(half, 1), jnp.asarray(e, dtype=BF16), jnp.asarray(ones))


def _tail_kernel(h_ref, a1_ref, a2_ref, w1_ref, w2_ref, nwm_ref, wu_ref, wd_ref, p_ref, nwp_ref, wg_ref, wp_ref,
                 fw_ref, o_ref, *, final):
    hm = h_ref[...] + _dot(a1_ref[...], w1_ref[...]) + _dot(a2_ref[...], w2_ref[...])
    u = jnp.maximum(_dot(_rms(hm, nwm_ref[...]).astype(BF16), wu_ref[...]), 0.0)
    h2 = hm + _dot((u * u).astype(BF16), wd_ref[...])
    gate = _sigmoid(_dot(_rms(h2, nwp_ref[...]).astype(BF16), wg_ref[...]))
    y = h2 + gate * _dot(p_ref[...].astype(BF16), wp_ref[...])
    if final:
        y = _rms(y, fw_ref[...])
    o_ref[...] = y


def _layer_tail(h, a1, a2, w1, w2, nwm, wu, wd, p, layer, nwp, wg, wp, fw, final, tm=512):
    T, D = h.shape
    FF = wu.shape[1]
    P = p.shape[1]
    tm = min(tm, T)
    nt = T // tm
    K1, K2 = w1.shape[0], w2.shape[0]
    row = lambda w: pl.BlockSpec((tm, w), lambda i: (i, 0))
    full = lambda r, c: pl.BlockSpec((r, c), lambda i: (0, 0))
    return pl.pallas_call(
        functools.partial(_tail_kernel, final=final), grid=(nt,),
        in_specs=[row(D), row(K1), row(K2), full(K1, D), full(K2, D), full(1, D), full(D, FF), full(FF, D),
                  pl.BlockSpec((tm, P), lambda i: (layer * nt + i, 0)), full(1, D), full(D, D), full(P, D), full(1, D)],
        out_specs=row(D),
        out_shape=jax.ShapeDtypeStruct((T, D), F32),
        compiler_params=_cparams("parallel"), name="layer_tail",
    )(h, a1, a2, w1, w2, nwm.reshape(1, D), wu, wd, p, nwp.reshape(1, D), wg, wp, fw.reshape(1, D))


def _fox_gate_kernel(x_ref, b_ref, o_ref, carry_ref, *, ts):
    @pl.when(pl.program_id(1) == 0)
    def _():
        carry_ref[...] = jnp.zeros_like(carry_ref)

    r = lax.broadcasted_iota(jnp.int32, (LANES, LANES), 0)
    c = lax.broadcasted_iota(jnp.int32, (LANES, LANES), 1)
    lower = (r >= c).astype(F32)
    lane = lax.broadcasted_iota(jnp.int32, (LANES, LANES), 1)

    nblk = ts // LANES
    within = []
    for n in range(nblk):
        x = x_ref[n * LANES:(n + 1) * LANES, :] + b_ref[...]
        logf = jnp.minimum(x, 0.0) - jnp.log1p(jnp.exp(-jnp.abs(x)))
        within.append(_dot(lower, logf, precision=HIGHEST))
    carry = carry_ref[0:1, :]
    for n in range(nblk):
        cum = within[n] + carry
        carry = cum[LANES - 1:LANES, :]
        f = cum * LOG2E
        hi = f.astype(BF16).astype(F32)
        mid = (f - hi).astype(BF16).astype(F32)
        lo = f - hi - mid
        for h in range(FOX_HEADS):
            v = jnp.where(lane == HEAD_DIM, hi[:, h:h + 1],
                          jnp.where(lane == HEAD_DIM + 1, mid[:, h:h + 1],
                                    jnp.where(lane == HEAD_DIM + 2, lo[:, h:h + 1], 0.0)))
            o_ref[n * LANES:(n + 1) * LANES, h * LANES:(h + 1) * LANES] = v.astype(o_ref.dtype)
    carry_ref[0:1, :] = carry


def _fox_gate(sm, bias, B, S, ts=1024):
    T = B * S
    ts = min(ts, S)
    ns = S // ts
    b = jnp.zeros((1, LANES), F32).at[0, :FOX_HEADS].set(bias.astype(F32))
    return pl.pallas_call(
        functools.partial(_fox_gate_kernel, ts=ts), grid=(B, ns),
        in_specs=[pl.BlockSpec((ts, LANES), lambda bb, n: (bb * ns + n, 0)),
                  pl.BlockSpec((1, LANES), lambda bb, n: (0, 0))],
        out_specs=pl.BlockSpec((ts, FOX_HEADS * LANES), lambda bb, n: (bb * ns + n, 0)),
        out_shape=jax.ShapeDtypeStruct((T, FOX_HEADS * LANES), BF16),
        scratch_shapes=[pltpu.VMEM((SUBLANES, LANES), F32)],
        compiler_params=_cparams("parallel", "arbitrary"), name="fox_gate",
    )(sm, b)


def _fox_attn_kernel(q_ref, k_ref, v_ref, f_ref, o_ref, *, tq, tk, dk, hp):
    i = pl.program_id(2)
    qs = [q_ref[:, hh * LANES:(hh + 1) * LANES] for hh in range(hp)]
    nsub = tq // tk

    def scores(start, size, r0):
        rows = pl.ds(start, size)
        return [_dot_nt(qs[hh][r0:], k_ref[rows, hh * LANES:(hh + 1) * LANES] + f_ref[rows, hh * LANES:(hh + 1) * LANES])
                for hh in range(hp)]

    def finish(ss, start, size, r0, carry, masked):
        rows = pl.ds(start, size)
        out = []
        for hh in range(hp):
            m, acc = carry[hh]
            s = ss[hh]
            if masked:
                r = lax.broadcasted_iota(jnp.int32, (tq - r0, size), 0)
                c = lax.broadcasted_iota(jnp.int32, (tq - r0, size), 1)
                s = jnp.where(c <= r, s, NEG)
            m_new = jnp.maximum(m[r0:], jnp.max(s, axis=1, keepdims=True))
            pr = jnp.exp2(s - m_new).astype(BF16)
            acc_new = jnp.exp2(m[r0:] - m_new) * acc[r0:] + _dot(pr, v_ref[rows, hh * LANES:(hh + 1) * LANES])
            if r0:
                m_new = jnp.concatenate([m[:r0], m_new], axis=0)
                acc_new = jnp.concatenate([acc[:r0], acc_new], axis=0)
            out.append((m_new, acc_new))
        return tuple(out)

    def tile(start, size, r0, carry, masked):
        return finish(scores(start, size, r0), start, size, r0, carry, masked)

    init = tuple((jnp.full((tq, 1), NEG, F32), jnp.zeros((tq, LANES), F32)) for _ in range(hp))
    carry = lax.fori_loop(0, i * nsub, lambda j, c: tile(pl.multiple_of(j * tk, tk), tk, 0, c, False), init)
    dstarts = [pl.multiple_of(i * tq + d * dk, dk) for d in range(tq // dk)]
    dss = [scores(dstarts[d], dk, d * dk) for d in range(tq // dk)]
    for d in range(tq // dk):
        carry = finish(dss[d], dstarts[d], dk, d * dk, carry, True)
    o_ref[...] = jnp.concatenate([acc[:, :HEAD_DIM] / acc[:, HEAD_DIM:HEAD_DIM + 1] for (_, acc) in carry],
                                 axis=1).astype(o_ref.dtype)


def _fox_attn(q, k, v, f3, B, S, tq=1024, tk=1024, dk=512, hp=2):
    T = B * S
    tq = min(tq, S)
    tk = min(tk, tq)
    dk = min(dk, tq)
    nq = S // tq
    bw = hp * LANES
    ng = FOX_HEADS // hp
    res = pl.BlockSpec((S, bw), lambda b, p, i: (b, p))
    return pl.pallas_call(
        functools.partial(_fox_attn_kernel, tq=tq, tk=tk, dk=dk, hp=hp), grid=(B, ng, nq),
        in_specs=[pl.BlockSpec((tq, bw), lambda b, p, i: (b * nq + i, p)), res, res, res],
        out_specs=pl.BlockSpec((tq, hp * HEAD_DIM), lambda b, p, i: (b * nq + i, p)),
        out_shape=jax.ShapeDtypeStruct((T, FOX_HEADS * HEAD_DIM), BF16),
        compiler_params=_cparams("parallel", "parallel", "arbitrary"), name="fox_attn",
    )(q, k, v, f3)


def _cumsum_rows(x):
    n = x.shape[0]
    row = lax.broadcasted_iota(jnp.int32, x.shape, 0)
    d = 1
    while d < n:
        x = x + jnp.where(row >= d, pltpu.roll(x, d, 0), 0.0)
        d *= 2
    return x


def _gdn_kernel(x_ref, z_ref, sm_ref, cw_ref, par_ref, nw_ref, o_ref, state_ref, xp_ref, *, cps):
    C = GDN_CHUNK
    D = GDN_DIM
    W = GDN_HEADS * D
    R = cps * C
    n = pl.program_id(1)

    @pl.when(n == 0)
    def _():
        state_ref[...] = jnp.zeros_like(state_ref)
        xp_ref[0:SUBLANES, :] = jnp.zeros((SUBLANES, 3 * W), F32)

    x = x_ref[...]
    xp_ref[SUBLANES:SUBLANES + R, :] = x
    cw = cw_ref[...]
    y = x * cw[CONV_W - 1:CONV_W]
    for j in range(CONV_W - 1):
        off = SUBLANES - (CONV_W - 1) + j
        y = y + xp_ref[off:off + R, :] * cw[j:j + 1]
    xp_ref[0:SUBLANES, :] = x[R - SUBLANES:R]
    y = _silu(y)

    sm = sm_ref[...]
    beta_all = _sigmoid(sm)
    g_all = -jnp.exp(par_ref[0:1, :]) * _softplus(sm + par_ref[1:2, :])
    ri = lax.broadcasted_iota(jnp.int32, (C, C), 0)
    ci = lax.broadcasted_iota(jnp.int32, (C, C), 1)
    incl = ri >= ci
    strict = ri > ci

    chains = [(c, h) for c in range(cps) for h in range(GDN_HEADS)]
    gcs = [_cumsum_rows(g_all[c * C:(c + 1) * C]) for c in range(cps)]
    gcts = [gc.T for gc in gcs]
    pre = {}
    for (c, h) in chains:
        r0 = c * C
        q = y[r0:r0 + C, h * D:(h + 1) * D]
        k = y[r0:r0 + C, W + h * D:W + (h + 1) * D]
        v = y[r0:r0 + C, 2 * W + h * D:2 * W + (h + 1) * D]
        qn = q * lax.rsqrt(jnp.sum(q * q, axis=-1, keepdims=True) + 1e-6) * (D ** -0.5)
        kn = k * lax.rsqrt(jnp.sum(k * k, axis=-1, keepdims=True) + 1e-6)
        beta = beta_all[r0:r0 + C, GDN_B_LANE + h:GDN_B_LANE + h + 1]
        gcol = gcs[c][:, GDN_A_LANE + h:GDN_A_LANE + h + 1]
        grow = gcts[c][GDN_A_LANE + h:GDN_A_LANE + h + 1, :]
        decay = jnp.where(incl, jnp.exp(jnp.where(incl, gcol - grow, 0.0)), 0.0)
        kb = kn * beta
        g_last = gcol[C - 1:C, :]
        pre[c, h] = dict(qn16=qn.astype(BF16), kn16=kn.astype(BF16), kb16=kb.astype(BF16), decay=decay,
                         rhs=jnp.concatenate([v * beta, kb * jnp.exp(gcol)], axis=1),
                         qg16=(qn * jnp.exp(gcol)).astype(BF16),
                         kg16=(kn * jnp.exp(g_last - gcol)).astype(BF16), e_last=jnp.exp(g_last))
    kk = {ch: _dot_nt(pre[ch]["kb16"], pre[ch]["kn16"]) for ch in chains}
    qk = {ch: _dot_nt(pre[ch]["qn16"], pre[ch]["kn16"]) for ch in chains}
    P = {ch: -jnp.where(strict, kk[ch] * pre[ch]["decay"], 0.0) for ch in chains}
    X = dict(P)
    for _ in range(5):
        P = {ch: _dot(P[ch].astype(BF16), P[ch].astype(BF16)) for ch in chains}
        XP = {ch: _dot(X[ch].astype(BF16), P[ch].astype(BF16)) for ch in chains}
        X = {ch: X[ch] + P[ch] + XP[ch] for ch in chains}
    sol = {ch: pre[ch]["rhs"] + _dot(X[ch].astype(BF16), pre[ch]["rhs"].astype(BF16)) for ch in chains}
    attn16 = {ch: jnp.where(incl, qk[ch] * pre[ch]["decay"], 0.0).astype(BF16) for ch in chains}

    heads = range(GDN_HEADS)
    states = [state_ref[h] for h in heads]
    for c in range(cps):
        r0 = c * C
        st16 = [states[h].astype(BF16) for h in heads]
        ws = [_dot(sol[c, h][:, D:].astype(BF16), st16[h]) for h in heads]
        qs = [_dot(pre[c, h]["qg16"], st16[h]) for h in heads]
        v_new = [(sol[c, h][:, :D] - ws[h]).astype(BF16) for h in heads]
        av = [_dot(attn16[c, h], v_new[h]) for h in heads]
        kv = [_dot_tn(pre[c, h]["kg16"], v_new[h]) for h in heads]
        outs = []
        for h in heads:
            states[h] = states[h] * pre[c, h]["e_last"] + kv[h]
            o = _rms(qs[h] + av[h], nw_ref[...]) * _silu(z_ref[r0:r0 + C, h * D:(h + 1) * D])
            outs.append(o)
        o_ref[r0:r0 + C, :] = jnp.concatenate(outs, axis=1).astype(o_ref.dtype)
    for h in heads:
        state_ref[h] = states[h]


def _gdn(gqkv, gz, small, conv_w, a_log, dt_bias, norm_w, B, S, cps=8):
    T = B * S
    C = GDN_CHUNK
    R = cps * C
    N = S // R
    W = GDN_HEADS * GDN_DIM
    par = jnp.zeros((SUBLANES, LANES), F32)
    lanes = slice(GDN_A_LANE, GDN_A_LANE + GDN_HEADS)
    par = par.at[0, lanes].set(a_log.astype(F32)).at[1, lanes].set(dt_bias.astype(F32))
    return pl.pallas_call(
        functools.partial(_gdn_kernel, cps=cps), grid=(B, N),
        in_specs=[pl.BlockSpec((R, 3 * W), lambda b, n: (b * N + n, 0)),
                  pl.BlockSpec((R, W), lambda b, n: (b * N + n, 0)),
                  pl.BlockSpec((R, LANES), lambda b, n: (b * N + n, 0)),
                  pl.BlockSpec((CONV_W, 3 * W), lambda b, n: (0, 0)),
                  pl.BlockSpec((SUBLANES, LANES), lambda b, n: (0, 0)),
                  pl.BlockSpec((1, GDN_DIM), lambda b, n: (0, 0))],
        out_specs=pl.BlockSpec((R, W), lambda b, n: (b * N + n, 0)),
        out_shape=jax.ShapeDtypeStruct((T, W), BF16),
        scratch_shapes=[pltpu.VMEM((GDN_HEADS, GDN_DIM, GDN_DIM), F32),
                        pltpu.VMEM((SUBLANES + R, 3 * W), F32)],
        compiler_params=_cparams("parallel", "arbitrary"), name="gdn",
    )(gqkv, gz, small, conv_w.astype(F32), par, norm_w.reshape(1, GDN_DIM).astype(F32))


def _compress_kernel(x_ref, pe_ref, w1_ref, wbd_ref, w2_ref, o_ref, *, nh, with_one):
    ab = jnp.zeros((nh, NSA_GROUPS * 2 * CMP_HIDDEN), F32)
    for l in range(CMP_STRIDE):
        ab = ab + _dot(x_ref[pl.ds(l, nh, stride=CMP_STRIDE), :].astype(BF16), wbd_ref[0, l])
    lane = lax.broadcasted_iota(jnp.int32, (nh, HEAD_DIM), 1)
    one_col = jnp.where(lane == 0, 1.0 if with_one else 0.0, 0.0)
    c = _dot(pe_ref[0].astype(BF16), w1_ref[0])
    for g in range(NSA_GROUPS):
        a = ab[:, 2 * g * CMP_HIDDEN:(2 * g + 1) * CMP_HIDDEN]
        b = ab[:, (2 * g + 1) * CMP_HIDDEN:(2 * g + 2) * CMP_HIDDEN]
        hid = _gelu_tanh(a + pltpu.roll(b, nh - 1, 0) + c[0:1])
        out = _dot(hid.astype(BF16), w2_ref[0])
        o_ref[0, g] = jnp.concatenate([out, one_col], axis=1).astype(o_ref.dtype)


def _compress(kc, vc, pe, w1, w2, B, S):
    nh = S // CMP_STRIDE
    G = NSA_GROUPS
    half = CMP_STRIDE * HEAD_DIM
    w1r = w1.reshape(2, 2, CMP_STRIDE, HEAD_DIM, CMP_HIDDEN)
    per_tok = jnp.concatenate([w1r[:, 0], w1r[:, 1]], axis=-1)
    eye = jnp.eye(G, dtype=w1.dtype)
    wbd = per_tok[:, :, None, :, None, :] * eye[None, None, :, None, :, None]
    wbd = wbd.reshape(2, CMP_STRIDE, G * HEAD_DIM, G * 2 * CMP_HIDDEN)
    x_spec = pl.BlockSpec((S, G * HEAD_DIM), lambda b, j: (b, 0))
    out = [pl.pallas_call(
        functools.partial(_compress_kernel, nh=nh, with_one=bool(kv)), grid=(B, 1),
        in_specs=[x_spec, pl.BlockSpec((1, SUBLANES, 2 * half), lambda b, j, kv=kv: (kv, 0, 0)),
                  pl.BlockSpec((1, 2 * half, CMP_HIDDEN), lambda b, j, kv=kv: (kv, 0, 0)),
                  pl.BlockSpec((1, CMP_STRIDE, G * HEAD_DIM, G * 2 * CMP_HIDDEN), lambda b, j, kv=kv: (kv, 0, 0, 0)),
                  pl.BlockSpec((1, CMP_HIDDEN, HEAD_DIM), lambda b, j, kv=kv: (kv, 0, 0))],
        out_specs=pl.BlockSpec((1, G, nh, LANES), lambda b, j: (b, 0, 0, 0)),
        out_shape=jax.ShapeDtypeStruct((B, G, nh, LANES), BF16),
        compiler_params=_cparams("parallel", "arbitrary"), name="nsa_compress",
    )(x, pe, w1, wbd, w2) for kv, x in enumerate((kc, vc))]
    return out


def _stack_heads(qb):
    return jnp.concatenate([qb[:, h * HEAD_DIM:(h + 1) * HEAD_DIM] for h in range(NSA_HPG)], axis=0)


def _unstack_heads(o, tq):
    return jnp.concatenate([o[h * tq:(h + 1) * tq] for h in range(NSA_HPG)], axis=1)


def _nsa_cmp_kernel(q_ref, kc_ref, vc_ref, ov_ref, qoh_ref, stair_ref, oc_ref, bias_ref, *, tq, n_sel, top_k, off):
    i = pl.program_id(1)
    G = NSA_GROUPS
    GW = NSA_HPG * HEAD_DIM
    ncp = kc_ref.shape[2]
    row = lax.broadcasted_iota(jnp.int32, (NSA_HPG * tq, 1), 0)
    t4 = i * tq + (row & (tq - 1))
    stair = stair_ref[pl.ds(pl.multiple_of(off - (tq // CMP_STRIDE) * i, 2 * SUBLANES), ncp), :]
    qoh = qoh_ref[...]
    ss = [_dot_nt(jnp.concatenate([_stack_heads(q_ref[:, g * GW:(g + 1) * GW]), qoh], axis=1), kc_ref[0, g] + stair)
          for g in range(G)]
    ps, accs = [], []
    for g in range(G):
        s = ss[g]
        m = jnp.max(s, axis=1, keepdims=True)
        p = jnp.exp2(s - m)
        ps.append(p)
        accs.append(_dot(p.astype(BF16), vc_ref[0, g]))
    imps = []
    for g in range(G):
        inv = jnp.where(t4 >= CMP_BLOCK - 1, 1.0 / accs[g][:, HEAD_DIM:HEAD_DIM + 1], 0.0)
        oc_ref[:, g * GW:(g + 1) * GW] = _unstack_heads(accs[g][:, :HEAD_DIM] * inv, tq)
        p = ps[g] * inv
        psum = p[0:tq] + p[tq:2 * tq] + p[2 * tq:3 * tq] + p[3 * tq:4 * tq]
        imps.append(_dot(psum, ov_ref[...], precision=HIGHEST))
    blk = lax.broadcasted_iota(jnp.int32, (1, LANES), 1)
    t = i * tq + lax.broadcasted_iota(jnp.int32, (tq, 1), 0)
    cur = t >> (SEL_BLOCK.bit_length() - 1)
    forced = (blk == 0) | (blk == cur) | (blk == cur - 1)
    future = blk * SEL_BLOCK > t
    blk_t = lax.broadcasted_iota(jnp.int32, (LANES, tq), 0).astype(F32)
    for g in range(G):
        imp = jnp.where(future, NEG, jnp.where(forced, REMOVED, imps[g]))
        imp = jnp.where(blk < n_sel, imp, REMOVED)
        imp_t = imp.T
        for _ in range(max(top_k - 3, 0)):
            mx = jnp.max(imp_t, axis=0, keepdims=True)
            first = jnp.min(jnp.where(imp_t == mx, blk_t, float(LANES)), axis=0, keepdims=True)
            imp_t = jnp.where(blk_t == first, REMOVED, imp_t)
        sel = jnp.where(imp_t == REMOVED, 1.0, 0.0).T
        bias_ref[0, g] = jnp.where((sel > 0.0) & jnp.logical_not(future), 0.0, NEG).astype(bias_ref.dtype)


def _nsa_cmp(q, kc, vc, overlap, B, S, tq=512):
    T = B * S
    tq = min(tq, S)
    nq = S // tq
    ncp = kc.shape[2]
    n_sel = S // SEL_BLOCK
    QW = NSA_HEADS * HEAD_DIM
    G = NSA_GROUPS
    step = tq // CMP_STRIDE
    off = (nq - 1) * step
    r = np.arange(NSA_HPG * tq) % tq
    jrel = (r - (CMP_BLOCK - 1)) // CMP_STRIDE + 2
    qoh = (jrel[:, None] == np.arange(HEAD_DIM)[None, :]).astype(np.float32)
    d = np.arange(off + ncp)[:, None] - off + 2
    stair = np.zeros((off + ncp, LANES), np.float32)
    stair[:, HEAD_DIM:] = np.where(np.arange(HEAD_DIM)[None, :] < d, NEG, 0.0)
    return pl.pallas_call(
        functools.partial(_nsa_cmp_kernel, tq=tq, n_sel=n_sel, top_k=min(SEL_TOPK, n_sel), off=off),
        grid=(B, nq),
        in_specs=[pl.BlockSpec((tq, QW), lambda b, i: (b * nq + i, 0)),
                  pl.BlockSpec((1, G, ncp, LANES), lambda b, i: (b, 0, 0, 0)),
                  pl.BlockSpec((1, G, ncp, LANES), lambda b, i: (b, 0, 0, 0)),
                  pl.BlockSpec((ncp, LANES), lambda b, i: (0, 0)),
                  pl.BlockSpec((NSA_HPG * tq, HEAD_DIM), lambda b, i: (0, 0)),
                  pl.BlockSpec((off + ncp, LANES), lambda b, i: (0, 0))],
        out_specs=[pl.BlockSpec((tq, QW), lambda b, i: (b * nq + i, 0)),
                   pl.BlockSpec((1, G, tq, LANES), lambda b, i: (b, 0, i, 0))],
        out_shape=[jax.ShapeDtypeStruct((T, QW), F32),
                   jax.ShapeDtypeStruct((B, G, S, LANES), BF16)],
        compiler_params=_cparams("parallel", "parallel"), name="nsa_cmp",
    )(q, kc, vc, overlap, jnp.asarray(qoh, dtype=BF16), jnp.asarray(stair, dtype=BF16))


def _nsa_main_kernel(q_ref, bias_ref, ks_ref, oh_ref, vs_ref, kw_ref, vw_ref, oc_ref, g_ref, o_ref, *, qn, tk, wspan):
    R = NSA_HPG * qn
    tb = 2 * tk
    i = pl.program_id(2)
    q4 = _stack_heads(q_ref[...])
    q4p = jnp.concatenate([q4, jnp.zeros_like(q4)], axis=1)
    b4 = jnp.concatenate([bias_ref[0, 0]] * NSA_HPG, axis=0)
    qa = jnp.concatenate([q4p, b4], axis=1)
    t4 = i * qn + (lax.broadcasted_iota(jnp.int32, (R, 1), 0) & (qn - 1))

    def scores(start, size):
        rows = pl.ds(start, size)
        ka = jnp.concatenate([ks_ref[rows, :], oh_ref[rows, :]], axis=1)
        return _dot_nt(qa, ka)

    def finish(s, start, size, carry, masked):
        m, acc = carry
        if masked:
            kpos = start + lax.broadcasted_iota(jnp.int32, (1, size), 1)
            s = jnp.where(kpos <= t4, s, NEG)
        m_new = jnp.maximum(m, jnp.max(s, axis=1, keepdims=True))
        pr = jnp.exp2(s - m_new).astype(BF16)
        acc = jnp.exp2(m - m_new) * acc + _dot(pr, vs_ref[pl.ds(start, size), :])
        return m_new, acc

    def step(start, size, carry, masked):
        return finish(scores(start, size), start, size, carry, masked)

    first = i * qn
    nbig = first // tb
    diag = (first // tk) * tk
    nsmall = (diag - nbig * tb) // tk
    init = (jnp.full((R, 1), NEG, F32), jnp.zeros((R, LANES), F32))
    carry = lax.fori_loop(0, nbig, lambda j, c: step(pl.multiple_of(j * tb, tb), tb, c, False), init)
    carry = lax.fori_loop(0, nsmall, lambda j, c: step(pl.multiple_of(nbig * tb, tk), tk, c, False), carry)
    dstart = pl.multiple_of(diag, tk)
    s_d = scores(dstart, tk)
    wrows = pl.ds(pl.multiple_of(jnp.maximum(first + qn - wspan, 0), qn), wspan)
    s_w = _dot_nt(q4p, kw_ref[wrows, :])
    _, acc_s = finish(s_d, dstart, tk, carry, True)
    kpos = jnp.maximum(first + qn - wspan, 0) + lax.broadcasted_iota(jnp.int32, (1, wspan), 1)
    wmask = lax.bitcast_convert_type(t4 - kpos, jnp.uint32) < jnp.uint32(WINDOW)
    s_w = jnp.where(wmask, s_w, NEG)
    m_w = jnp.max(s_w, axis=1, keepdims=True)
    acc_w = _dot(jnp.exp2(s_w - m_w).astype(BF16), vw_ref[wrows, :])

    gates = _sigmoid(g_ref[...])
    first_group = pl.program_id(1) == 0
    ng = NSA_HPG * 3

    def gate(h, branch):
        j = 3 * h + branch
        return jnp.where(first_group, gates[:, j:j + 1], gates[:, ng + j:ng + j + 1])

    oc = oc_ref[...]
    outs = []
    for h in range(NSA_HPG):
        hs = slice(h * qn, (h + 1) * qn)
        c_s = gate(h, 1) / acc_s[hs, HEAD_DIM:HEAD_DIM + 1]
        c_w = gate(h, 2) / acc_w[hs, HEAD_DIM:HEAD_DIM + 1]
        outs.append(gate(h, 0) * oc[:, h * HEAD_DIM:(h + 1) * HEAD_DIM]
                    + c_s * acc_s[hs, :HEAD_DIM] + c_w * acc_w[hs, :HEAD_DIM])
    o_ref[...] = jnp.concatenate(outs, axis=1).astype(o_ref.dtype)


def _nsa_main(q, bias, ksp, onehot, vsa, kwp, vwa, oc, gates, B, S, qn=256, tk=512):
    T = B * S
    nq = S // qn
    tk = min(tk, S // 2)
    wspan = min(WINDOW + qn, S)
    GW = NSA_HPG * HEAD_DIM
    res = pl.BlockSpec((S, LANES), lambda b, g, i: (b, g))
    return pl.pallas_call(
        functools.partial(_nsa_main_kernel, qn=qn, tk=tk, wspan=wspan), grid=(B, NSA_GROUPS, nq),
        in_specs=[pl.BlockSpec((qn, GW), lambda b, g, i: (b * nq + i, g)),
                  pl.BlockSpec((1, 1, qn, LANES), lambda b, g, i: (b, g, i, 0)),
                  res, pl.BlockSpec((S, LANES), lambda b, g, i: (0, 0)), res, res, res,
                  pl.BlockSpec((qn, GW), lambda b, g, i: (b * nq + i, g)),
                  pl.BlockSpec((qn, LANES), lambda b, g, i: (b * nq + i, 0))],
        out_specs=pl.BlockSpec((qn, GW), lambda b, g, i: (b * nq + i, g)),
        out_shape=jax.ShapeDtypeStruct((T, NSA_HEADS * HEAD_DIM), BF16),
        compiler_params=_cparams("parallel", "parallel", "arbitrary"), name="nsa_main",
    )(q, bias, ksp, onehot, vsa, kwp, vwa, oc, gates)


def _lru_kernel(gx_ref, cw_ref, cb_ref, wab_ref, bab_ref, lam_ref, o_ref, h_ref, tail_ref, a_ref, b_ref, *, ts):
    n = pl.program_id(0)
    Bb = gx_ref.shape[0]
    W = gx_ref.shape[2] // 2

    @pl.when(n == 0)
    def _():
        h_ref[...] = jnp.zeros_like(h_ref)
        tail_ref[...] = jnp.zeros_like(tail_ref)

    xin = jnp.swapaxes(gx_ref[:, :, W:], 0, 1)
    xp = jnp.concatenate([tail_ref[...], xin], axis=0)
    cw = cw_ref[...]
    x = cb_ref[...].reshape(1, 1, W)
    for j in range(CONV_W):
        x = x + xp[j:j + ts] * cw[j:j + 1].reshape(1, 1, W)
    tail_ref[...] = xin[ts - (CONV_W - 1):ts]
    x2 = x.reshape(ts * Bb, W)
    pre = _dot(x2.astype(BF16), wab_ref[...]) + bab_ref[...]
    r = _sigmoid(pre[:, :W])
    ig = _sigmoid(pre[:, W:])
    log_a = (-RG_C * _softplus(-lam_ref[...])) * r
    a = jnp.exp(log_a)
    th = jnp.tanh(log_a)
    bb = jnp.sqrt(-2.0 * th / (1.0 - th)) * (ig * x2)
    a_ref[...] = a.reshape(ts, Bb, W)
    b_ref[...] = bb.reshape(ts, Bb, W)

    def scan(t, h):
        h = a_ref[t] * h + b_ref[t]
        b_ref[t] = h
        return h

    h_ref[...] = lax.fori_loop(0, ts, scan, h_ref[...], unroll=8)
    o_ref[...] = (jnp.swapaxes(b_ref[...], 0, 1) * _gelu_tanh(gx_ref[:, :, :W])).astype(o_ref.dtype)


def _lru(gx, conv_w, conv_b, wab, bab, lam, ts=128):
    Bb, S, W2 = gx.shape
    W = W2 // 2
    ts = min(ts, S)
    full = lambda shape: pl.BlockSpec(shape, lambda n: (0,) * len(shape))
    return pl.pallas_call(
        functools.partial(_lru_kernel, ts=ts), grid=(S // ts,),
        in_specs=[pl.BlockSpec((Bb, ts, W2), lambda n: (0, n, 0)),
                  full((CONV_W, W)), full((1, W)), full((W, 2 * W)), full((1, 2 * W)), full((1, W))],
        out_specs=pl.BlockSpec((Bb, ts, W), lambda n: (0, n, 0)),
        out_shape=jax.ShapeDtypeStruct((Bb, S, W), BF16),
        scratch_shapes=[pltpu.VMEM((Bb, W), F32), pltpu.VMEM((CONV_W - 1, Bb, W), F32),
                        pltpu.VMEM((ts, Bb, W), F32), pltpu.VMEM((ts, Bb, W), F32)],
        compiler_params=_cparams("arbitrary"), name="rg_lru",
    )(gx, conv_w, conv_b, wab, bab, lam)


def _even_mixer(h, B, S, norm_w, w_in, fox_bf, conv_w, a_log, dt_bias, gdn_norm_w, w_out):
    T, D = h.shape
    FW = FOX_HEADS * HEAD_DIM
    GW = GDN_HEADS * GDN_DIM
    o_ff = 3 * FW
    o_g = o_ff + FOX_HEADS
    o_gb = o_g + 4 * GW
    small = jnp.concatenate([w_in[:, o_ff:o_g], w_in[:, o_gb:o_gb + 2 * GDN_HEADS],
                             jnp.zeros((D, LANES - FOX_HEADS - 2 * GDN_HEADS), w_in.dtype)], axis=1)
    w = jnp.concatenate([w_in[:, :o_ff], w_in[:, o_g:o_gb], small], axis=1).astype(BF16)
    segs = [(0, 0, 0, FW, False, HEAD_DIM ** -0.5 * LOG2E, tuple((j, -1.0) for j in range(FOX_F_PIECES))),
            (1, 0, FW, FW, False, 1.0, ()),
            (2, 0, 2 * FW, FW, False, 1.0, ((0, 1.0),)),
            (3, 0, 3 * FW, 3 * GW, False, 1.0, None),
            (4, 0, 3 * FW + 3 * GW, GW, False, 1.0, None),
            (5, 0, 3 * FW + 4 * GW, LANES, False, 1.0, None)]
    fq, fk, fv, gqkv, gz, sm = _norm_proj(h, norm_w, w, segs,
                                          [(2 * FW, BF16), (2 * FW, BF16), (2 * FW, BF16),
                                           (3 * GW, F32), (GW, F32), (LANES, F32)])
    f3 = _fox_gate(sm, fox_bf, B, S)
    fox = _fox_attn(fq, fk, fv, f3, B, S)
    gdn = _gdn(gqkv, gz, sm, conv_w, a_log, dt_bias, gdn_norm_w, B, S)
    wo = w_out.astype(BF16)
    return fox, gdn, wo[:FW], wo[FW:]


def _odd_mixer(h, B, S, rope_tabs, norm_w, w_in, k_pe, k_w1, k_w2, v_pe, v_w1, v_w2,
               conv_w, conv_b, wa, ba, wx, bx, lam, w_out):
    T, D = h.shape
    QW = NSA_HEADS * HEAD_DIM
    KW = NSA_GROUPS * HEAD_DIM
    G = NSA_GROUPS
    o_ng = QW + 6 * KW
    ngw = NSA_HEADS * 3
    w = jnp.concatenate([w_in[:, :o_ng], w_in[:, o_ng:o_ng + ngw], jnp.zeros((D, LANES - ngw), w_in.dtype),
                         w_in[:, o_ng + ngw:]], axis=1).astype(BF16)
    ones = ((0, 1.0),)
    segs = [(0, 0, 0, QW, True, HEAD_DIM ** -0.5 * LOG2E, None),
            (1, 0, QW, KW, True, 1.0, None),
            (2, 0, QW + KW, KW, False, 1.0, None),
            (3, 0, QW + 2 * KW, KW, True, 1.0, ()),
            (4, 0, QW + 3 * KW, KW, False, 1.0, ones),
            (5, 0, QW + 4 * KW, KW, True, 1.0, ()),
            (6, 0, QW + 5 * KW, KW, False, 1.0, ones),
            (7, 0, o_ng, LANES, False, 1.0, None),
            (8, 0, o_ng + LANES, 2 * LRU_W, False, 1.0, None)]
    q, kc, vc, ksp, vsa, kwp, vwa, ng, rgx = _norm_proj(
        h, norm_w, w, segs,
        [(QW, BF16), (KW, F32), (KW, F32), (2 * KW, BF16), (2 * KW, BF16), (2 * KW, BF16), (2 * KW, BF16),
         (LANES, F32), (2 * LRU_W, F32)],
        rope_tabs=rope_tabs)
    nh = S // CMP_STRIDE
    pe = jnp.stack([k_pe.reshape(-1), v_pe.reshape(-1)]).astype(F32)
    pe = jnp.broadcast_to(pe[:, None, :], (2, SUBLANES, pe.shape[-1]))
    w1 = jnp.stack([k_w1, v_w1]).astype(BF16)
    w2 = jnp.stack([k_w2, v_w2]).astype(BF16)
    kcmp, vcmp = _compress(kc, vc, pe, w1, w2, B, S)
    n_cmp = (S - CMP_BLOCK) // CMP_STRIDE + 1
    ncp = -(-nh // LANES) * LANES
    if ncp != nh:
        kcmp, vcmp = (jnp.pad(t, ((0, 0), (0, 0), (0, ncp - nh), (0, 0))) for t in (kcmp, vcmp))
    n_sel = S // SEL_BLOCK
    cs = np.arange(ncp) * CMP_STRIDE
    ss = np.arange(LANES) * SEL_BLOCK
    ov = ((cs[:, None] <= ss[None, :] + SEL_BLOCK - 1) & (cs[:, None] + CMP_BLOCK - 1 >= ss[None, :])
          & (np.arange(ncp)[:, None] < n_cmp) & (np.arange(LANES)[None, :] < n_sel))
    overlap = jnp.asarray(ov.astype(np.float32))
    oc, bias = _nsa_cmp(q, kcmp, vcmp, overlap, B, S)
    blk_of_pos = np.arange(S) // SEL_BLOCK
    onehot = jnp.asarray((blk_of_pos[:, None] == np.arange(LANES)[None, :]).astype(np.float32), dtype=BF16)
    nsa = _nsa_main(q, bias, ksp, onehot, vsa, kwp, vwa, oc, ng, B, S)
    nblk, bw, _ = wa.shape
    eye = jnp.eye(nblk, dtype=wa.dtype)
    dense = lambda wb: (eye[:, None, :, None] * wb[:, :, None, :]).reshape(nblk * bw, nblk * bw)
    wab = jnp.concatenate([dense(wa), dense(wx)], axis=1).astype(BF16)
    bab = jnp.concatenate([ba, bx]).reshape(1, 2 * LRU_W).astype(F32)
    lru = _lru(rgx.reshape(B, S, 2 * LRU_W), conv_w.astype(F32), conv_b.reshape(1, LRU_W).astype(F32), wab, bab,
               lam.reshape(1, LRU_W).astype(F32))
    wo = w_out.astype(BF16)
    return nsa, lru.reshape(T, LRU_W), wo[:QW], wo[QW:]


def kernel(x, p, positions, even_norm_mix, even_w_in, even_fox_bf, even_gdn_conv_w, even_gdn_a_log, even_gdn_dt_bias, even_gdn_norm_w, even_w_out, odd_norm_mix, odd_w_in, odd_cmp_k_pe, odd_cmp_k_w1, odd_cmp_k_w2, odd_cmp_v_pe, odd_cmp_v_w1, odd_cmp_v_w2, odd_rg_conv_w, odd_rg_conv_b, odd_rg_wa, odd_rg_ba, odd_rg_wx, odd_rg_bx, odd_rg_lambda, odd_w_out, mlp_norm, mlp_w_up, mlp_w_down, ple_norm, ple_w_gate, ple_w_proj, final_norm):
    B, S, D = x.shape
    T = B * S
    depth = p.shape[0]
    h = x.reshape(T, D)
    rope_tabs = _rope_tables(positions) if depth > 1 else None
    for i in range(depth):
        j = i // 2
        if i % 2 == 0:
            mix = _even_mixer(h, B, S, even_norm_mix[j], even_w_in[j], even_fox_bf[j], even_gdn_conv_w[j],
                              even_gdn_a_log[j], even_gdn_dt_bias[j], even_gdn_norm_w[j], even_w_out[j])
        else:
            mix = _odd_mixer(h, B, S, rope_tabs, odd_norm_mix[j], odd_w_in[j], odd_cmp_k_pe[j], odd_cmp_k_w1[j],
                             odd_cmp_k_w2[j], odd_cmp_v_pe[j], odd_cmp_v_w1[j], odd_cmp_v_w2[j], odd_rg_conv_w[j],
                             odd_rg_conv_b[j], odd_rg_wa[j], odd_rg_ba[j], odd_rg_wx[j], odd_rg_bx[j],
                             odd_rg_lambda[j], odd_w_out[j])
        h = _layer_tail(h, *mix, mlp_norm[i], mlp_w_up[i].astype(BF16), mlp_w_down[i].astype(BF16),
                        p.reshape(depth * T, -1), i, ple_norm[i], ple_w_gate[i].astype(BF16), ple_w_proj[i].astype(BF16),
                        final_norm, final=(i == depth - 1))
    return h.reshape(B, S, D)
```
